```python
import math
import jax
import jax.numpy as jnp
from jax import lax
import numpy as np

D_MODEL = 1024
BATCH = 16
SEQ = 2048
DEPTH = 2

GRID_W = 64
CTX_LEN = 256
NORM_EPS = 1e-6
CHUNK = 64
Q_BLOCK = 128
ROPE_THETA = 10000.0

ML_HEADS = 4
ML_DK = 128
ML_DV = 128
DF_HEADS = 4
DF_HD = 64
DF_DV = 2 * DF_HD
GL_HEADS = 4
GL_DK = 64
GL_DV = 128
GL_RANK = 16
GL_TAU = 16.0
N_BRANCH = 3
BRANCH_W = 512
N_GROUPS = 4
EXPERTS_PER_GROUP = 8
N_EXPERTS = N_GROUPS * EXPERTS_PER_GROUP
TOP_K = 2
D_EXPERT = 512

IN_SIZES = (
    ML_HEADS * ML_DK, ML_HEADS * ML_DK, ML_HEADS * ML_DV, ML_HEADS * ML_DV, 4 * ML_HEADS,
    DF_HEADS * 2 * DF_HD, DF_HEADS * 2 * DF_HD, DF_HEADS * DF_DV,
    GL_HEADS * GL_DK, GL_HEADS * GL_DK, GL_HEADS * GL_DV, GL_HEADS * GL_DV, 2 * GL_RANK,
    N_BRANCH * D_MODEL,
)
IN_WIDTH = sum(IN_SIZES)

kernel_name = "hybrid_mlstm_diffattn_gla_hmoe_dit"

F32 = jnp.float32


def rms_norm(x, g):
    xf = x.astype(F32)
    y = xf * lax.rsqrt(jnp.mean(xf * xf, axis=-1, keepdims=True) + NORM_EPS)
    return (y * g.astype(F32)).astype(x.dtype)


def to_heads(t, h):
    b, n, w = t.shape
    return t.reshape(b, n, h, w // h).transpose(0, 2, 1, 3)


def merge_heads(t):
    b, h, n, d = t.shape
    return t.transpose(0, 2, 1, 3).reshape(b, n, h * d)


def split_proj(p):
    idx = [int(i) for i in np.cumsum(IN_SIZES)[:-1]]
    return jnp.split(p, idx, axis=-1)


def flip_t(t):
    return jnp.flip(t, axis=2)


def to_chunks(t):
    n = t.shape[2]
    t = t.reshape(t.shape[:2] + (n // CHUNK, CHUNK) + t.shape[3:])
    return jnp.moveaxis(t, 2, 0)


def from_chunks(t):
    t = jnp.moveaxis(t, 0, 2)
    return t.reshape(t.shape[:2] + (t.shape[2] * t.shape[3],) + t.shape[4:])


def axial_rope_tables(rows):
    r = jnp.repeat(jnp.arange(rows), GRID_W).astype(F32)
    col = jnp.tile(jnp.arange(GRID_W), rows).astype(F32)
    n_freq = DF_HD // 4
    inv = ROPE_THETA ** (-jnp.arange(n_freq, dtype=F32) / n_freq)
    ang_r = r[:, None] * inv
    ang_c = col[:, None] * inv
    return jnp.cos(ang_r), jnp.sin(ang_r), jnp.cos(ang_c), jnp.sin(ang_c)


def rotate_half(x, cos, sin):
    x1, x2 = jnp.split(x, 2, axis=-1)
    return jnp.concatenate([x1 * cos - x2 * sin, x2 * cos + x1 * sin], axis=-1)


def apply_axial_rope(x, tabs):
    cr, sr, cc, sc = tabs
    xf = x.astype(F32)
    half = DF_HD // 2
    y = jnp.concatenate([rotate_half(xf[..., :half], cr, sr), rotate_half(xf[..., half:], cc, sc)], axis=-1)
    return y.astype(x.dtype)


def mlstm_scan(q, k, v, ig, fg, state):
    causal = jnp.tril(jnp.ones((CHUNK, CHUNK), dtype=bool))
    logf = jax.nn.log_sigmoid(fg)
    xs = (to_chunks(q), to_chunks(k), to_chunks(v), to_chunks(ig), to_chunks(logf))

    def step(carry, inp):
        C, nv, m = carry
        qc, kc, vc, ic, lfc = inp
        b = jnp.cumsum(lfc, axis=-1)
        dmat = b[..., :, None] - b[..., None, :] + ic[..., None, :]
        dmat = jnp.where(causal, dmat, -jnp.inf)
        inter = b + m[..., None]
        mj = jnp.maximum(jnp.max(dmat, axis=-1), inter)
        w = jnp.exp(dmat - mj[..., None])
        w_inter = jnp.exp(inter - mj)
        s = jnp.einsum('bhjd,bhid->bhji', qc, kc).astype(F32) * w
        num = jnp.einsum('bhji,bhiv->bhjv', s, vc) + w_inter[..., None] * jnp.einsum('bhvd,bhjd->bhjv', C, qc)
        den = jnp.sum(s, axis=-1) + w_inter * jnp.einsum('bhd,bhjd->bhj', nv, qc)
        h = num / jnp.maximum(jnp.abs(den), jnp.exp(-mj))[..., None]
        b_last = b[..., -1]
        gi = b_last[..., None] - b + ic
        m_new = jnp.maximum(b_last + m, jnp.max(gi, axis=-1))
        wk = jnp.exp(gi - m_new[..., None])
        decay = jnp.exp(b_last + m - m_new)
        C_new = decay[..., None, None] * C + jnp.einsum('bhi,bhiv,bhid->bhvd', wk, vc, kc)
        n_new = decay[..., None] * nv + jnp.einsum('bhi,bhid->bhd', wk, kc)
        return (C_new, n_new, m_new), h

    final, hs = lax.scan(step, state, xs)
    return from_chunks(hs), final


def gla_scan(q, k, v, loga, S):
    causal = jnp.tril(jnp.ones((CHUNK, CHUNK), dtype=bool))
    xs = (to_chunks(q), to_chunks(k), to_chunks(v), to_chunks(loga))

    def step(S, inp):
        qc, kc, vc, lac = inp
        bc = jnp.cumsum(lac, axis=-2)
        diff = bc[..., :, None, :] - bc[..., None, :, :]
        decay = jnp.exp(jnp.where(causal[:, :, None], diff, -jnp.inf))
        A = jnp.einsum('bhjid,bhjd,bhid->bhji', decay, qc.astype(F32), kc.astype(F32))
        o = jnp.einsum('bhji,bhiv->bhjv', A, vc) + jnp.einsum('bhjd,bhdv->bhjv', qc * jnp.exp(bc), S)
        b_last = bc[..., -1, :]
        S_new = jnp.exp(b_last)[..., None] * S + jnp.einsum('bhid,bhiv->bhdv', kc * jnp.exp(b_last[..., None, :] - bc), vc)
        return S_new, o

    final, os_ = lax.scan(step, S, xs)
    return from_chunks(os_), final


def prefix_bidir(scan_fn, init, ctx_fwd, lat_fwd, ctx_bwd, lat_bwd):
    hc_f, st = scan_fn(*ctx_fwd, init)
    hl_f, _ = scan_fn(*lat_fwd, st)
    hc_b, st = scan_fn(*[flip_t(a) for a in ctx_bwd], init)
    hl_b, _ = scan_fn(*[flip_t(a) for a in lat_bwd], st)
    return hc_f + flip_t(hc_b), hl_f + flip_t(hl_b)


def mlstm_branch(pc, pl, gate_b, norm_g, need_ctx):
    def prep(p):
        q, k, v, o, g = p
        b_, n_ = g.shape[:2]
        q = to_heads(q, ML_HEADS)
        k = to_heads(k, ML_HEADS) * (ML_DK ** -0.5)
        v = to_heads(v, ML_HEADS)
        pre = g.astype(F32).reshape(b_, n_, 4, ML_HEADS) + gate_b.astype(F32).reshape(4, ML_HEADS)
        pre = jnp.transpose(pre, (2, 0, 3, 1))
        return (q, k, v, pre[0], pre[1]), (q, k, v, pre[2], pre[3]), o

    c_f, c_b, o_c = prep(pc)
    l_f, l_b, o_l = prep(pl)
    b_ = pl[0].shape[0]
    init = (jnp.zeros((b_, ML_HEADS, ML_DV, ML_DK), F32), jnp.zeros((b_, ML_HEADS, ML_DK), F32),
            jnp.zeros((b_, ML_HEADS), F32))
    h_c, h_l = prefix_bidir(mlstm_scan, init, c_f, l_f, c_b, l_b)

    def finish(h, o):
        return merge_heads(rms_norm(h, norm_g)).astype(o.dtype) * jax.nn.sigmoid(o)

    return (finish(h_c, o_c) if need_ctx else None), finish(h_l, o_l)


def gla_branch(pc, pl, w_alpha, b_alpha, norm_g, need_ctx):
    def prep(p):
        q, k, v, g, a = p
        q = to_heads(q, GL_HEADS) * (GL_DK ** -0.5)
        k = to_heads(k, GL_HEADS)
        v = to_heads(v, GL_HEADS)
        a_f, a_b = jnp.split(a, 2, axis=-1)
        la_f = jax.nn.log_sigmoid((a_f @ w_alpha[0]).astype(F32) + b_alpha[0].astype(F32)) / GL_TAU
        la_b = jax.nn.log_sigmoid((a_b @ w_alpha[1]).astype(F32) + b_alpha[1].astype(F32)) / GL_TAU
        return (q, k, v, to_heads(la_f, GL_HEADS)), (q, k, v, to_heads(la_b, GL_HEADS)), g

    c_f, c_b, g_c = prep(pc)
    l_f, l_b, g_l = prep(pl)
    b_ = pl[0].shape[0]
    init = jnp.zeros((b_, GL_HEADS, GL_DK, GL_DV), F32)
    o_c, o_l = prefix_bidir(gla_scan, init, c_f, l_f, c_b, l_b)

    def finish(o, g):
        return merge_heads(rms_norm(o, norm_g)).astype(g.dtype) * jax.nn.silu(g)

    return (finish(o_c, g_c) if need_ctx else None), finish(o_l, g_l)


def diff_heads(t):
    b_, n_ = t.shape[:2]
    t = t.reshape(b_, n_, DF_HEADS, 2, DF_HD).transpose(3, 0, 2, 1, 4)
    return t[0], t[1]


def diff_attend(q1, q2, k1, k2, v, lam):
    scale = DF_HD ** -0.5
    a1 = jax.nn.softmax(jnp.einsum('bhqd,bhkd->bhqk', q1, k1).astype(F32) * scale, axis=-1)
    a2 = jax.nn.softmax(jnp.einsum('bhqd,bhkd->bhqk', q2, k2).astype(F32) * scale, axis=-1)
    a = a1 - lam * a2
    return jnp.einsum('bhqk,bhkd->bhqd', a.astype(v.dtype), v)


def diff_attn_branch(pc, pl, lam_p, norm_g, lam_init, rope, need_ctx):
    lp = lam_p.astype(F32)
    lam = jnp.exp(jnp.sum(lp[0] * lp[1])) - jnp.exp(jnp.sum(lp[2] * lp[3])) + lam_init
    q1c, q2c = diff_heads(pc[0])
    k1c, k2c = diff_heads(pc[1])
    vc = to_heads(pc[2], DF_HEADS)
    q1l, q2l = [apply_axial_rope(t, rope) for t in diff_heads(pl[0])]
    k1l, k2l = [apply_axial_rope(t, rope) for t in diff_heads(pl[1])]
    vl = to_heads(pl[2], DF_HEADS)
    k1 = jnp.concatenate([k1c, k1l], axis=2)
    k2 = jnp.concatenate([k2c, k2l], axis=2)
    v = jnp.concatenate([vc, vl], axis=2)
    b_, h_, n_, _ = q1l.shape
    nb = n_ // Q_BLOCK

    def blocks(t):
        return jnp.moveaxis(t.reshape(b_, h_, nb, Q_BLOCK, DF_HD), 2, 0)

    o_l = lax.map(lambda qs: diff_attend(qs[0], qs[1], k1, k2, v, lam), (blocks(q1l), blocks(q2l)))
    o_l = jnp.moveaxis(o_l, 0, 2).reshape(b_, h_, n_, DF_DV)

    def finish(o):
        return merge_heads(rms_norm(o, norm_g) * (1.0 - lam_init))

    o_c = finish(diff_attend(q1c, q2c, k1c, k2c, vc, lam)) if need_ctx else None
    return o_c, finish(o_l)


def gated_merge(gate_pre, branches, w_branch, w_out):
    b_, n_ = gate_pre.shape[:2]
    gates = jax.nn.sigmoid(gate_pre.reshape(b_, n_, N_BRANCH, D_MODEL))
    y = gates[:, :, 0] * (branches[0] @ w_branch[0])
    for br in range(1, N_BRANCH):
        y = y + gates[:, :, br] * (branches[br] @ w_branch[br])
    return y @ w_out


def token_mixers(hc, hl, w_in, ml_gate_b, ml_norm_g, df_lambda, df_norm_g, gl_w_alpha, gl_b_alpha,
                 gl_norm_g, w_branch, w_out, lam_init, rope, need_ctx):
    pc = split_proj(hc @ w_in)
    pl = split_proj(hl @ w_in)
    ml_c, ml_l = mlstm_branch(pc[0:5], pl[0:5], ml_gate_b, ml_norm_g, need_ctx)
    df_c, df_l = diff_attn_branch(pc[5:8], pl[5:8], df_lambda, df_norm_g, lam_init, rope, need_ctx)
    gl_c, gl_l = gla_branch(pc[8:13], pl[8:13], gl_w_alpha, gl_b_alpha, gl_norm_g, need_ctx)
    out_l = gated_merge(pl[13], (ml_l, df_l, gl_l), w_branch, w_out)
    out_c = gated_merge(pc[13], (ml_c, df_c, gl_c), w_branch, w_out) if need_ctx else None
    return out_c, out_l


def hier_moe(h, wg, bg, we, be, w_gate, w_up, w_down):
    t = h.shape[0]
    pg = jax.nn.softmax((h @ wg).astype(F32) + bg.astype(F32), axis=-1)
    pg_top, g_idx = lax.top_k(pg, 1)
    le = ((h @ we).astype(F32) + be.astype(F32)).reshape(t, N_GROUPS, EXPERTS_PER_GROUP)
    le_g = jnp.take_along_axis(le, g_idx[:, :, None], axis=1)[:, 0]
    pe = jax.nn.softmax(le_g, axis=-1)
    pe_top, e_idx = lax.top_k(pe, TOP_K)
    w = pg_top * pe_top / jnp.sum(pe_top, axis=-1, keepdims=True)
    eid = g_idx * EXPERTS_PER_GROUP + e_idx
    comb = jnp.sum(jax.nn.one_hot(eid, N_EXPERTS, dtype=F32) * w[..., None], axis=1)
    y = jnp.zeros(h.shape, F32)
    for e in range(N_EXPERTS):
        hid = jax.nn.silu(h @ w_gate[e]) * (h @ w_up[e])
        y = y + comb[:, e:e + 1] * (hid @ w_down[e])
    return y.astype(h.dtype)


def setup_inputs(seed: int = 0) -> dict:
    key = jax.random.key(seed)
    ks = jax.random.split(key, 32)
    D = D_MODEL

    def nrm(k, shape, s):
        return jax.random.normal(k, shape, F32) * s

    ig_b = nrm(ks[9], (DEPTH, 2, ML_HEADS), 0.1)
    fg_b = jnp.linspace(3.0, 6.0, ML_HEADS, dtype=F32) + nrm(ks[10], (DEPTH, 2, ML_HEADS), 0.1)
    ml_gate_b = jnp.stack([ig_b[:, 0], fg_b[:, 0], ig_b[:, 1], fg_b[:, 1]], axis=1).reshape(DEPTH, 4 * ML_HEADS)
    return {
        "x": nrm(ks[0], (BATCH, SEQ, D), 1.0),
        "c": nrm(ks[1], (BATCH, D), 1.0),
        "ctx": nrm(ks[2], (BATCH, CTX_LEN, D), 1.0),
        "c_ctx": nrm(ks[3], (D,), 1.0),
        "w_mod": nrm(ks[4], (DEPTH, D, 6 * D), 0.5 * D ** -0.5),
        "b_mod": nrm(ks[5], (DEPTH, 6 * D), 0.02),
        "norm_mix_g": 1.0 + nrm(ks[6], (DEPTH, D), 0.02),
        "norm_ffn_g": 1.0 + nrm(ks[7], (DEPTH, D), 0.02),
        "w_in": nrm(ks[8], (DEPTH, D, IN_WIDTH), D ** -0.5),
        "ml_gate_b": ml_gate_b,
        "ml_norm_g": 1.0 + nrm(ks[11], (DEPTH, ML_DV), 0.02),
        "df_lambda": nrm(ks[12], (DEPTH, 4, DF_HD), 0.1),
        "df_norm_g": 1.0 + nrm(ks[13], (DEPTH, DF_DV), 0.02),
        "gl_w_alpha": nrm(ks[14], (DEPTH, 2, GL_RANK, GL_HEADS * GL_DK), GL_RANK ** -0.5),
        "gl_b_alpha": 2.0 + nrm(ks[15], (DEPTH, 2, GL_HEADS * GL_DK), 0.1),
        "gl_norm_g": 1.0 + nrm(ks[16], (DEPTH, GL_DV), 0.02),
        "w_branch": nrm(ks[17], (DEPTH, N_BRANCH, BRANCH_W, D), BRANCH_W ** -0.5),
        "w_out": nrm(ks[18], (DEPTH, D, D), D ** -0.5),
        "router_group_w": nrm(ks[19], (DEPTH, D, N_GROUPS), D ** -0.5),
        "router_group_b": nrm(ks[20], (DEPTH, N_GROUPS), 0.01),
        "router_expert_w": nrm(ks[21], (DEPTH, D, N_EXPERTS), D ** -0.5),
        "router_expert_b": nrm(ks[22], (DEPTH, N_EXPERTS), 0.01),
        "moe_w_gate": nrm(ks[23], (DEPTH, N_EXPERTS, D, D_EXPERT), D ** -0.5),
        "moe_w_up": nrm(ks[24], (DEPTH, N_EXPERTS, D, D_EXPERT), D ** -0.5),
        "moe_w_down": nrm(ks[25], (DEPTH, N_EXPERTS, D_EXPERT, D), D_EXPERT ** -0.5),
        "final_norm_g": 1.0 + nrm(ks[26], (D,), 0.02),
    }


def reference(x, c, ctx, c_ctx, w_mod, b_mod, norm_mix_g, norm_ffn_g, w_in, ml_gate_b, ml_norm_g,
              df_lambda, df_norm_g, gl_w_alpha, gl_b_alpha, gl_norm_g, w_branch, w_out,
              router_group_w, router_group_b, router_expert_w, router_expert_b,
              moe_w_gate, moe_w_up, moe_w_down, final_norm_g):
    n = x.shape[1]
    rows = n // GRID_W
    rope = axial_rope_tables(rows)
    for li in range(DEPTH):
        need_ctx = li < DEPTH - 1
        lam_init = 0.8 - 0.6 * math.exp(-0.3 * li)
        mod = jax.nn.silu(c) @ w_mod[li] + b_mod[li]
        mod_c = jax.nn.silu(c_ctx) @ w_mod[li] + b_mod[li]
        sh1, sc1, g1, sh2, sc2, g2 = jnp.split(mod[:, None, :], 6, axis=-1)
        sh1c, sc1c, g1c, sh2c, sc2c, g2c = jnp.split(mod_c, 6, axis=-1)

        hl = rms_norm(x, norm_mix_g[li]) * (1.0 + sc1) + sh1
        hc = rms_norm(ctx, norm_mix_g[li]) * (1.0 + sc1c) + sh1c
        out_c, out_l = token_mixers(hc, hl, w_in[li], ml_gate_b[li], ml_norm_g[li], df_lambda[li],
                                    df_norm_g[li], gl_w_alpha[li], gl_b_alpha[li], gl_norm_g[li],
                                    w_branch[li], w_out[li], lam_init, rope, need_ctx)
        x = x + g1 * out_l
        hl = rms_norm(x, norm_ffn_g[li]) * (1.0 + sc2) + sh2
        moe_w = (router_group_w[li], router_group_b[li], router_expert_w[li], router_expert_b[li],
                 moe_w_gate[li], moe_w_up[li], moe_w_down[li])
        if need_ctx:
            ctx = ctx + g1c * out_c
            hc = rms_norm(ctx, norm_ffn_g[li]) * (1.0 + sc2c) + sh2c
            n_ctx_tok = hc.shape[0] * hc.shape[1]
            tokens = jnp.concatenate([hc.reshape(-1, D_MODEL), hl.reshape(-1, D_MODEL)], axis=0)
            y = hier_moe(tokens, *moe_w)
            ctx = ctx + g2c * y[:n_ctx_tok].reshape(hc.shape)
            x = x + g2 * y[n_ctx_tok:].reshape(hl.shape)
        else:
            x = x + g2 * hier_moe(hl.reshape(-1, D_MODEL), *moe_w).reshape(hl.shape)
    return rms_norm(x, final_norm_g)
```

```python
import functools
import math

import numpy as np
import jax
import jax.numpy as jnp
from jax import lax
from jax.experimental import pallas as pl
from jax.experimental.pallas import tpu as pltpu

F32 = jnp.float32
BF16 = jnp.bfloat16
HIGHEST = lax.Precision.HIGHEST

D_MODEL = 1024
GRID_W = 64
NORM_EPS = 1e-6
CHUNK = 64
ROPE_THETA = 10000.0
ML_HEADS, ML_DK, ML_DV = 4, 128, 128
DF_HEADS, DF_HD, DF_DV = 4, 64, 128
GL_HEADS, GL_DK, GL_DV, GL_RANK, GL_TAU = 4, 64, 128, 16, 16.0
N_BRANCH, BRANCH_W = 3, 512
N_GROUPS, EXPERTS_PER_GROUP, TOP_K, D_EXPERT = 4, 8, 2, 512
N_EXPERTS = N_GROUPS * EXPERTS_PER_GROUP
IN_SIZES = (
    ML_HEADS * ML_DK, ML_HEADS * ML_DK, ML_HEADS * ML_DV, ML_HEADS * ML_DV, 4 * ML_HEADS,
    DF_HEADS * 2 * DF_HD, DF_HEADS * 2 * DF_HD, DF_HEADS * DF_DV,
    GL_HEADS * GL_DK, GL_HEADS * GL_DK, GL_HEADS * GL_DV, GL_HEADS * GL_DV, 2 * GL_RANK,
    N_BRANCH * D_MODEL,
)

LANE = 128
VMEM_LIMIT = 56 * 1024 * 1024
MOE_TILE = 256

A_MQ, A_MK, A_MV, A_GQ, A_GK, A_GV, A_DV = 0, 4, 8, 12, 16, 20, 24
A_WIDTH = 28 * LANE
F_DQ, F_DK, F_MO, F_GG, F_GT, F_MG, F_GA = 0, 4, 8, 12, 16, 40, 41
F_WIDTH = 42 * LANE


def _cparams(sem):
    return pltpu.CompilerParams(dimension_semantics=sem, vmem_limit_bytes=VMEM_LIMIT)


def _const_spec(shape):
    nd = len(shape)
    return pl.BlockSpec(shape, lambda *_: (0,) * nd, pipeline_mode=pl.Buffered(1))


def _logsig(x):
    return jnp.minimum(x, 0.0) - jnp.log1p(jnp.exp(-jnp.abs(x)))


def _dot(a, b):
    return jnp.dot(a, b, preferred_element_type=F32)


def _dot_nt(a, b):
    return lax.dot_general(a, b, (((1,), (1,)), ((), ())), preferred_element_type=F32)


def _dot_tn(a, b):
    return lax.dot_general(a, b, (((0,), (0,)), ((), ())), preferred_element_type=F32)


def _rms(x, g):
    return x * lax.rsqrt(jnp.mean(x * x, axis=-1, keepdims=True) + NORM_EPS) * g


def _mod_kernel(c_ref, w_ref, b_ref, o_ref):
    c = c_ref[...]
    s = c * jax.nn.sigmoid(c)
    o_ref[...] = jnp.dot(s, w_ref[...], precision=HIGHEST, preferred_element_type=F32) + b_ref[...]


def _modulation(cc, w_mod, b_mod):
    rows, d = cc.shape
    n = w_mod.shape[1]
    tn = 512
    return pl.pallas_call(
        _mod_kernel,
        out_shape=jax.ShapeDtypeStruct((rows, n), F32),
        grid=(n // tn,),
        in_specs=[pl.BlockSpec((rows, d), lambda j: (0, 0)),
                  pl.BlockSpec((d, tn), lambda j: (0, j)),
                  pl.BlockSpec((1, tn), lambda j: (0, j))],
        out_specs=pl.BlockSpec((rows, tn), lambda j: (0, j)),
        compiler_params=_cparams(("parallel",)),
        name="adaln_mod",
    )(cc, w_mod, b_mod.reshape(1, n))


def _inproj_kernel(x_ref, mod_ref, g_ref, wa_ref, wf_ref, oa_ref, of_ref):
    y = _rms(x_ref[...], g_ref[...])
    h = (y * (1.0 + mod_ref[1:2, :]) + mod_ref[0:1, :]).astype(BF16)
    oa_ref[...] = _dot(h, wa_ref[...]).astype(BF16)
    of_ref[...] = _dot(h, wf_ref[...])


def _inproj(xs, modtab, g, wa, wf, tm, nct):
    b, s, d = xs.shape
    kind = lambda i: jnp.where(i >= nct, 1, 0)
    return pl.pallas_call(
        _inproj_kernel,
        out_shape=(jax.ShapeDtypeStruct((b, s, A_WIDTH), BF16), jax.ShapeDtypeStruct((b, s, F_WIDTH), F32)),
        grid=(b, s // tm),
        in_specs=[pl.BlockSpec((None, tm, d), lambda bi, i: (bi, i, 0)),
                  pl.BlockSpec((None, None, 8, d), lambda bi, i: (bi, kind(i), 0, 0)),
                  _const_spec((1, d)), _const_spec((d, A_WIDTH)), _const_spec((d, F_WIDTH))],
        out_specs=(pl.BlockSpec((None, tm, A_WIDTH), lambda bi, i: (bi, i, 0)),
                   pl.BlockSpec((None, tm, F_WIDTH), lambda bi, i: (bi, i, 0))),
        compiler_params=_cparams(("parallel", "parallel")),
        name="in_proj",
    )(xs, modtab, g.reshape(1, d), wa, wf)


def _chunk_ids(t, nct, nch):
    return t, jnp.where(t < nct, nct - 1 - t, nch - 1 - (t - nct))


def _mlstm_kernel(q_ref, k_ref, v_ref, o_ref, gcol_ref, grow_ref, bcol_ref, brow_ref, ng_ref, out_ref,
                  hf_ref, hb_ref, ct_ref, *, nct, nch):
    L = CHUNK
    head = pl.program_id(1)
    jj = lax.broadcasted_iota(jnp.int32, (L, L), 0)
    ii = lax.broadcasted_iota(jnp.int32, (L, L), 1)
    lane = lax.broadcasted_iota(jnp.int32, (L, LANE), 1)
    ones_col = jnp.where(lane == 0, 1.0, 0.0).astype(BF16)
    scale = ML_DK ** -0.5
    ct_ref[...] = jnp.zeros(ct_ref.shape, F32)

    def step(t, carry):
        new = []
        for d, c, m in zip((0, 1), _chunk_ids(t, nct, nch), carry):
            r0 = pl.multiple_of(c * L, L)
            q = q_ref[pl.ds(r0, L), :]
            k = k_ref[pl.ds(r0, L), :]
            v = v_ref[pl.ds(r0, L), :]
            gc = gcol_ref[pl.ds(r0, L), :] + bcol_ref[...]
            gi = (2 * d) * ML_HEADS + head
            ic_col = jnp.sum(jnp.where(lane == gi, gc, 0.0), axis=1, keepdims=True)
            fc_col = jnp.sum(jnp.where(lane == gi + ML_HEADS, gc, 0.0), axis=1, keepdims=True)
            ic_row = grow_ref[c, pl.ds(gi, 1), :] + brow_ref[pl.ds(gi, 1), :]
            fc_row = grow_ref[c, pl.ds(gi + ML_HEADS, 1), :] + brow_ref[pl.ds(gi + ML_HEADS, 1), :]
            lf_col = _logsig(fc_col)
            lf_row = _logsig(fc_row)
            M = (ii <= jj) if d == 0 else (ii >= jj)
            MT = (jj <= ii) if d == 0 else (jj >= ii)
            b_col = jnp.sum(jnp.where(M, lf_row, 0.0), axis=1, keepdims=True)
            b_row = jnp.sum(jnp.where(MT, lf_col, 0.0), axis=0, keepdims=True)
            dmat = jnp.where(M, b_col - b_row + ic_row, -jnp.inf)
            inter = b_col + m
            mj = jnp.maximum(jnp.max(dmat, axis=1, keepdims=True), inter)
            w = jnp.exp(dmat - mj)
            w_inter = jnp.exp(inter - mj)
            s = _dot_nt(q, k) * (w * scale)
            ct = ct_ref[d]
            qc = _dot(q, ct.astype(BF16))
            num = _dot(s.astype(BF16), v) + w_inter * qc[:, :ML_DV]
            den = jnp.sum(s, axis=1, keepdims=True) + w_inter * qc[:, ML_DV:ML_DV + 1]
            hh = num / jnp.maximum(jnp.abs(den), jnp.exp(-mj))
            if d == 0:
                hf_ref[pl.ds(r0, L), :] = hh
            else:
                hb_ref[pl.ds(r0, L), :] = hh
            b_last = jnp.sum(lf_row, axis=1, keepdims=True)
            m_new = jnp.maximum(b_last + m, jnp.max(b_last - b_row + ic_row, axis=1, keepdims=True))
            wk_col = jnp.exp(b_last - b_col + ic_col - m_new) * scale
            decay = jnp.exp(b_last + m - m_new)
            vext = jnp.concatenate([v, ones_col], axis=1).astype(F32)
            ct_ref[d] = decay * ct + _dot_tn(k, (wk_col * vext).astype(BF16))
            new.append(m_new)
        return tuple(new)

    zero = jnp.zeros((1, 1), F32)
    lax.fori_loop(0, nch, step, (zero, zero))
    y = _rms(hf_ref[...] + hb_ref[...], ng_ref[...])
    out_ref[...] = (y * jax.nn.sigmoid(o_ref[...])).astype(BF16)


def _mlstm(oa, of, grow, gate_b, norm_g, nct, nch):
    b, s, _ = oa.shape
    bcol = jnp.zeros((1, LANE), F32).at[0, :4 * ML_HEADS].set(gate_b)
    brow = gate_b.reshape(4 * ML_HEADS, 1)
    col = lambda base: (lambda bi, h: (bi, 0, base + h))
    return pl.pallas_call(
        functools.partial(_mlstm_kernel, nct=nct, nch=nch),
        out_shape=jax.ShapeDtypeStruct((b, s, ML_HEADS * ML_DV), BF16),
        grid=(b, ML_HEADS),
        in_specs=[pl.BlockSpec((None, s, LANE), col(A_MQ)),
                  pl.BlockSpec((None, s, LANE), col(A_MK)),
                  pl.BlockSpec((None, s, LANE), col(A_MV)),
                  pl.BlockSpec((None, s, LANE), col(F_MO)),
                  pl.BlockSpec((None, s, LANE), lambda bi, h: (bi, 0, F_MG)),
                  pl.BlockSpec((None, nch, 4 * ML_HEADS, CHUNK), lambda bi, h: (bi, 0, 0, 0)),
                  pl.BlockSpec((1, LANE), lambda bi, h: (0, 0)),
                  pl.BlockSpec((4 * ML_HEADS, 1), lambda bi, h: (0, 0)),
                  pl.BlockSpec((1, ML_DV), lambda bi, h: (0, 0))],
        out_specs=pl.BlockSpec((None, s, LANE), lambda bi, h: (bi, 0, h)),
        scratch_shapes=[pltpu.VMEM((s, ML_DV), F32), pltpu.VMEM((s, ML_DV), F32),
                        pltpu.VMEM((2, ML_DK, 2 * LANE), F32)],
        compiler_params=_cparams(("parallel", "parallel")),
        name="mlstm",
    )(oa, oa, oa, of, of, grow, bcol, brow, norm_g.reshape(1, ML_DV))


def _gla_tables():
    L = CHUNK
    G = np.zeros((2, 7 * L, L), np.float32)
    lvl = np.full((2, L, L), 7, np.int32)
    for d in range(2):
        for p in range(L):
            if d == 0:
                G[d, p, :p + 1] = 1
            else:
                G[d, p, p:] = 1
        for li, s in enumerate((32, 16, 8, 4, 2, 1)):
            for p in range(L):
                base = (p // (2 * s)) * 2 * s
                row = (li + 1) * L + p
                if d == 0:
                    mid = base + s
                    if p >= mid:
                        G[d, row, mid + 1:p + 1] = 1
                    else:
                        G[d, row, p + 1:mid + 1] = 1
                else:
                    mid = base + s - 1
                    if p <= mid:
                        G[d, row, p:mid] = 1
                    else:
                        G[d, row, mid:p] = 1
            blk = np.arange(L) // (2 * s)
            upper = (np.arange(L) % (2 * s)) >= s
            same = blk[:, None] == blk[None, :]
            if d == 0:
                sel = same & upper[:, None] & ~upper[None, :]
            else:
                sel = same & ~upper[:, None] & upper[None, :]
            lvl[d][sel] = li
        lvl[d][np.arange(L), np.arange(L)] = 6
    return G, lvl


def _gla_kernel(q_ref, k_ref, v_ref, g_ref, a_ref, wa_ref, ba_ref, gm_ref, lvl_ref, ng_ref, out_ref,
                of_ref, ob_ref, x_ref, st_ref, law_ref, *, nct, nch):
    L = CHUNK
    for d in range(2):
        pre = jnp.dot(a_ref[...], wa_ref[d], precision=HIGHEST, preferred_element_type=F32) + ba_ref[d]
        la = _logsig(pre) * (1.0 / GL_TAU)
        for c in range(nch):
            law_ref[:, c * LANE:(c + 1) * LANE] = la[c * L:(c + 1) * L, :]
        lw = law_ref[...]
        hi = lw.astype(BF16)
        r1 = lw - hi.astype(F32)
        mid = r1.astype(BF16)
        lo = (r1 - mid.astype(F32)).astype(BF16)
        gm = gm_ref[d]
        xs = _dot(gm, hi) + _dot(gm, mid) + _dot(gm, lo)
        for c in range(nch):
            x_ref[d, c] = xs[:, c * LANE:(c + 1) * LANE]
    st_ref[...] = jnp.zeros(st_ref.shape, F32)

    def step(t, carry):
        for d, c in zip((0, 1), _chunk_ids(t, nct, nch)):
            r0 = pl.multiple_of(c * L, L)
            q = q_ref[pl.ds(r0, L), :].astype(F32) * (GL_DK ** -0.5)
            k = k_ref[pl.ds(r0, L), :].astype(F32)
            v = v_ref[pl.ds(r0, L), :]
            xc = x_ref[d, c]
            cs = xc[0:L]
            tot = cs[L - 1:L] if d == 0 else cs[0:1]
            lv = lvl_ref[d]
            amat = jnp.where(lv == 6, _dot_nt(q.astype(BF16), k.astype(BF16)), 0.0)
            for li in range(6):
                e = jnp.exp(xc[(li + 1) * L:(li + 2) * L])
                amat = jnp.where(lv == li, _dot_nt((q * e).astype(BF16), (k * e).astype(BF16)), amat)
            st = st_ref[d]
            qe = (q * jnp.exp(cs)).astype(BF16)
            o = _dot(amat.astype(BF16), v) + _dot_nt(qe, st.astype(BF16))
            if d == 0:
                of_ref[pl.ds(r0, L), :] = o
            else:
                ob_ref[pl.ds(r0, L), :] = o
            ke = (k * jnp.exp(tot - cs)).astype(BF16)
            st_ref[d] = st * jnp.exp(tot) + _dot_tn(v, ke)
        return carry

    lax.fori_loop(0, nch, step, 0)
    y = _rms(of_ref[...] + ob_ref[...], ng_ref[...])
    g = g_ref[...]
    out_ref[...] = (y * (g * jax.nn.sigmoid(g))).astype(BF16)


def _gla(oa, of, w_alpha, b_alpha, norm_g, nct, nch):
    b, s, _ = oa.shape
    gmat, lvl = _gla_tables()
    wa = jnp.zeros((GL_HEADS, 2, LANE, LANE), F32)
    ba = jnp.zeros((GL_HEADS, 2, 1, LANE), F32)
    for d in range(2):
        wd = w_alpha[d].reshape(GL_RANK, GL_HEADS, GL_DK).transpose(1, 0, 2)
        wa = wa.at[:, d, d * GL_RANK:(d + 1) * GL_RANK, :GL_DK].set(wd)
        ba = ba.at[:, d, 0, :GL_DK].set(b_alpha[d].reshape(GL_HEADS, GL_DK))
    col = lambda base: (lambda bi, h: (bi, 0, base + h))
    return pl.pallas_call(
        functools.partial(_gla_kernel, nct=nct, nch=nch),
        out_shape=jax.ShapeDtypeStruct((b, s, GL_HEADS * GL_DV), BF16),
        grid=(b, GL_HEADS),
        in_specs=[pl.BlockSpec((None, s, LANE), col(A_GQ)),
                  pl.BlockSpec((None, s, LANE), col(A_GK)),
                  pl.BlockSpec((None, s, LANE), col(A_GV)),
                  pl.BlockSpec((None, s, LANE), col(F_GG)),
                  pl.BlockSpec((None, s, LANE), lambda bi, h: (bi, 0, F_GA)),
                  pl.BlockSpec((None, 2, LANE, LANE), lambda bi, h: (h, 0, 0, 0)),
                  pl.BlockSpec((None, 2, 1, LANE), lambda bi, h: (h, 0, 0, 0)),
                  pl.BlockSpec((2, 7 * CHUNK, CHUNK), lambda bi, h: (0, 0, 0)),
                  pl.BlockSpec((2, CHUNK, CHUNK), lambda bi, h: (0, 0, 0)),
                  pl.BlockSpec((1, GL_DV), lambda bi, h: (0, 0))],
        out_specs=pl.BlockSpec((None, s, LANE), lambda bi, h: (bi, 0, h)),
        scratch_shapes=[pltpu.VMEM((s, GL_DV), F32), pltpu.VMEM((s, GL_DV), F32),
                        pltpu.VMEM((2, nch, 7 * CHUNK, LANE), F32),
                        pltpu.VMEM((2, GL_DV, LANE), F32),
                        pltpu.VMEM((CHUNK, nch * LANE), F32)],
        compiler_params=_cparams(("parallel", "parallel")),
        name="gla",
    )(oa, oa, oa, of, of, wa, ba, jnp.asarray(gmat, BF16), jnp.asarray(lvl), norm_g.reshape(1, GL_DV))


def _rope_tables(n):
    t = np.arange(n)
    n_freq = DF_HD // 4
    inv = jnp.asarray(ROPE_THETA, F32) ** (-jnp.arange(n_freq, dtype=F32) / n_freq)
    ang_r = jnp.asarray(t // GRID_W, F32)[:, None] * inv
    ang_c = jnp.asarray(t % GRID_W, F32)[:, None] * inv
    ang = jnp.concatenate([ang_r, ang_r, ang_c, ang_c] * 2, axis=1)
    first = (np.arange(LANE) % 32) < 16
    cos, sin = jnp.cos(ang), jnp.sin(ang)
    return cos, jnp.where(first, -sin, 0.0), jnp.where(first, 0.0, sin)


def _rope(x, cos, sa, sb):
    return x * cos + pltpu.roll(x, LANE - 16, 1) * sa + pltpu.roll(x, 16, 1) * sb


def _diff_kernel(q_ref, k_ref, v_ref, cq_ref, saq_ref, sbq_ref, ck_ref, sak_ref, sbk_ref, lam_ref, ng_ref,
                 out_ref, kr_ref, *, nct, n_ctx, lam_init):
    s = k_ref.shape[0]
    i = pl.program_id(2)

    @pl.when(i == 0)
    def _():
        kr_ref[0:n_ctx, :] = k_ref[0:n_ctx, :].astype(BF16)
        kr_ref[n_ctx:s, :] = _rope(k_ref[n_ctx:s, :], ck_ref[...], sak_ref[...], sbk_ref[...]).astype(BF16)

    lp = lam_ref[...]
    lam = (jnp.exp(jnp.sum(lp[0:1] * lp[1:2], axis=1, keepdims=True))
           - jnp.exp(jnp.sum(lp[2:3] * lp[3:4], axis=1, keepdims=True)) + lam_init)
    lane = lax.broadcasted_iota(jnp.int32, q_ref.shape, 1)

    def attend(qb, nk):
        qb = qb * (DF_HD ** -0.5)
        q1 = jnp.where(lane < DF_HD, qb, 0.0).astype(BF16)
        q2 = jnp.where(lane >= DF_HD, qb, 0.0).astype(BF16)
        kk = kr_ref[0:nk, :]
        s1 = _dot_nt(q1, kk)
        s2 = _dot_nt(q2, kk)
        p1 = jnp.exp(s1 - jnp.max(s1, axis=1, keepdims=True))
        p2 = jnp.exp(s2 - jnp.max(s2, axis=1, keepdims=True))
        a = (p1 * (1.0 / jnp.sum(p1, axis=1, keepdims=True))
             - p2 * (lam / jnp.sum(p2, axis=1, keepdims=True)))
        o = _dot(a.astype(BF16), v_ref[0:nk, :])
        out_ref[...] = (_rms(o, ng_ref[...]) * (1.0 - lam_init)).astype(BF16)

    @pl.when(i < nct)
    def _():
        attend(q_ref[...], n_ctx)

    @pl.when(i >= nct)
    def _():
        attend(_rope(q_ref[...], cq_ref[...], saq_ref[...], sbq_ref[...]), s)


def _diff_attn(oa, of, rope, df_lambda, norm_g, tq, nct, n_ctx, lam_init):
    b, s, _ = oa.shape
    n = s - n_ctx
    cos, sa, sb = rope
    lam_p = jnp.zeros((4, LANE), F32).at[:, :DF_HD].set(df_lambda)
    qblk = lambda bi, h, i: (jnp.maximum(i - nct, 0), 0)
    full = lambda bi, h, i: (0, 0)
    return pl.pallas_call(
        functools.partial(_diff_kernel, nct=nct, n_ctx=n_ctx, lam_init=lam_init),
        out_shape=jax.ShapeDtypeStruct((b, s, DF_HEADS * DF_DV), BF16),
        grid=(b, DF_HEADS, s // tq),
        in_specs=[pl.BlockSpec((None, tq, LANE), lambda bi, h, i: (bi, i, F_DQ + h)),
                  pl.BlockSpec((None, s, LANE), lambda bi, h, i: (bi, 0, F_DK + h)),
                  pl.BlockSpec((None, s, LANE), lambda bi, h, i: (bi, 0, A_DV + h)),
                  pl.BlockSpec((tq, LANE), qblk), pl.BlockSpec((tq, LANE), qblk), pl.BlockSpec((tq, LANE), qblk),
                  pl.BlockSpec((n, LANE), full), pl.BlockSpec((n, LANE), full), pl.BlockSpec((n, LANE), full),
                  pl.BlockSpec((4, LANE), full), pl.BlockSpec((1, DF_DV), full)],
        out_specs=pl.BlockSpec((None, tq, LANE), lambda bi, h, i: (bi, i, h)),
        scratch_shapes=[pltpu.VMEM((s, LANE), BF16)],
        compiler_params=_cparams(("parallel", "parallel", "arbitrary")),
        name="diff_attn",
    )(of, of, oa, cos, sa, sb, cos, sa, sb, lam_p, norm_g.reshape(1, DF_DV))


def _merge_kernel(ml_ref, df_ref, gl_ref, g0_ref, g1_ref, g2_ref, wb_ref, wo_ref, x_ref, mod_ref, nf_ref,
                  wr_ref, br_ref, xo_ref, h2_ref, rid_ref, rw_ref):
    y = (jax.nn.sigmoid(g0_ref[...]) * _dot(ml_ref[...], wb_ref[0])
         + jax.nn.sigmoid(g1_ref[...]) * _dot(df_ref[...], wb_ref[1])
         + jax.nn.sigmoid(g2_ref[...]) * _dot(gl_ref[...], wb_ref[2]))
    xn = x_ref[...] + mod_ref[2:3, :] * _dot(y.astype(BF16), wo_ref[...])
    xo_ref[...] = xn
    h2 = _rms(xn, nf_ref[...]) * (1.0 + mod_ref[4:5, :]) + mod_ref[3:4, :]
    h2_ref[...] = h2.astype(BF16)
    logits = jnp.dot(h2, wr_ref[...], precision=HIGHEST, preferred_element_type=F32) + br_ref[...]
    lane = lax.broadcasted_iota(jnp.int32, logits.shape, 1)
    lane_f = lane.astype(F32)
    neg = -jnp.inf

    def first_max(vals):
        mx = jnp.max(vals, axis=1, keepdims=True)
        return mx, jnp.min(jnp.where(vals == mx, lane_f, float(LANE)), axis=1, keepdims=True)

    is_grp = lane < N_GROUPS
    gmax, gidx = first_max(jnp.where(is_grp, logits, neg))
    pg_top = 1.0 / jnp.sum(jnp.where(is_grp, jnp.exp(logits - gmax), 0.0), axis=1, keepdims=True)
    lo = N_GROUPS + gidx * EXPERTS_PER_GROUP
    in_grp = (lane_f >= lo) & (lane_f < lo + EXPERTS_PER_GROUP)
    le = jnp.where(in_grp, logits, neg)
    m1, e1 = first_max(le)
    m2, e2 = first_max(jnp.where(lane_f == e1, neg, le))
    r = jnp.exp(m2 - m1)
    w1 = pg_top / (1.0 + r)
    w2 = pg_top * r / (1.0 + r)
    rid = jnp.where(lane == 0, e1 - N_GROUPS, jnp.where(lane == 1, e2 - N_GROUPS, 0.0))
    rid_ref[...] = rid.astype(jnp.int32)
    rw_ref[...] = jnp.where(lane == 0, w1, jnp.where(lane == 1, w2, 0.0))


def _merge(ml, df, gl, of, wb, wo, xs, modtab, nf, wr, br, tm, nct, first_block):
    b, s, d = xs.shape
    nb = s // tm - first_block
    so = nb * tm
    kind = lambda i: jnp.where(i + first_block >= nct, 1, 0)
    row = lambda bi, i: (bi, i + first_block, 0)
    gate = lambda br_: (lambda bi, i: (bi, i + first_block, F_GT // 8 + br_))
    outrow = lambda bi, i: (bi, i, 0)
    return pl.pallas_call(
        _merge_kernel,
        out_shape=(jax.ShapeDtypeStruct((b, so, d), F32), jax.ShapeDtypeStruct((b, so, d), BF16),
                   jax.ShapeDtypeStruct((b, so, LANE), jnp.int32), jax.ShapeDtypeStruct((b, so, LANE), F32)),
        grid=(b, nb),
        in_specs=[pl.BlockSpec((None, tm, BRANCH_W), row), pl.BlockSpec((None, tm, BRANCH_W), row),
                  pl.BlockSpec((None, tm, BRANCH_W), row),
                  pl.BlockSpec((None, tm, d), gate(0)), pl.BlockSpec((None, tm, d), gate(1)),
                  pl.BlockSpec((None, tm, d), gate(2)),
                  _const_spec((N_BRANCH, BRANCH_W, d)), _const_spec((d, d)),
                  pl.BlockSpec((None, tm, d), row),
                  pl.BlockSpec((None, None, 8, d), lambda bi, i: (bi, kind(i), 0, 0)),
                  _const_spec((1, d)), _const_spec((d, LANE)), _const_spec((1, LANE))],
        out_specs=(pl.BlockSpec((None, tm, d), outrow), pl.BlockSpec((None, tm, d), outrow),
                   pl.BlockSpec((None, tm, LANE), outrow), pl.BlockSpec((None, tm, LANE), outrow)),
        compiler_params=_cparams(("parallel", "parallel")),
        name="merge_route",
    )(ml, df, gl, of, of, of, wb, wo, xs, modtab, nf.reshape(1, d), wr, br)


def _gmm_kernel(te_ref, tv_ref, x_ref, ws_ref, wg_ref, wu_ref, wd_ref, y_ref):
    i = pl.program_id(0)

    @pl.when(tv_ref[i] > 0)
    def _():
        x = x_ref[...]
        a = _dot(x, wg_ref[...])
        hid = (a * jax.nn.sigmoid(a)) * _dot(x, wu_ref[...])
        y_ref[...] = _dot(hid.astype(BF16), wd_ref[...]) * ws_ref[...]

    @pl.when(tv_ref[i] == 0)
    def _():
        y_ref[...] = jnp.zeros(y_ref.shape, F32)


def _gmm(tile_expert, tile_valid, xs, ws, wg, wu, wd):
    npad, d = xs.shape
    tm = MOE_TILE
    de = wg.shape[2]
    return pl.pallas_call(
        _gmm_kernel,
        out_shape=jax.ShapeDtypeStruct((npad, d), F32),
        grid_spec=pltpu.PrefetchScalarGridSpec(
            num_scalar_prefetch=2,
            grid=(npad // tm,),
            in_specs=[pl.BlockSpec((tm, d), lambda i, te, tv: (i, 0)),
                      pl.BlockSpec((tm, 1), lambda i, te, tv: (i, 0)),
                      pl.BlockSpec((None, d, de), lambda i, te, tv: (te[i], 0, 0)),
                      pl.BlockSpec((None, d, de), lambda i, te, tv: (te[i], 0, 0)),
                      pl.BlockSpec((None, de, d), lambda i, te, tv: (te[i], 0, 0))],
            out_specs=pl.BlockSpec((tm, d), lambda i, te, tv: (i, 0))),
        compiler_params=_cparams(("arbitrary",)),
        name="moe_gmm",
    )(tile_expert, tile_valid, xs, ws, wg, wu, wd)


def _route_plan(eid, w, valid):
    t = eid.shape[0]
    tm = MOE_TILE
    n_tiles = (TOP_K * t + tm - 1) // tm + N_EXPERTS
    e_flat = jnp.where(valid[:, None], eid, N_EXPERTS).reshape(-1)
    w_flat = w.reshape(-1)
    counts = jnp.sum(e_flat[:, None] == jnp.arange(N_EXPERTS)[None, :], axis=0).astype(jnp.int32)
    tiles_e = (counts + tm - 1) // tm
    tile_end = jnp.cumsum(tiles_e)
    pad_start = (tile_end - tiles_e) * tm
    start = jnp.cumsum(counts) - counts
    order = jnp.argsort(e_flat, stable=True).astype(jnp.int32)
    tile = jnp.arange(n_tiles, dtype=jnp.int32)
    tile_expert = jnp.minimum(jnp.sum(tile[:, None] >= tile_end[None, :], axis=1), N_EXPERTS - 1).astype(jnp.int32)
    tile_valid = (tile < tile_end[-1]).astype(jnp.int32)
    p = jnp.arange(n_tiles * tm, dtype=jnp.int32)
    pe = tile_expert[p // tm]
    r = p - pad_start[pe]
    ok = (r < counts[pe]) & (tile_valid[p // tm] > 0)
    flat = order[jnp.clip(start[pe] + r, 0, TOP_K * t - 1)]
    src = jnp.where(ok, flat // TOP_K, 0)
    ws = jnp.where(ok, w_flat[flat], 0.0)
    pos = jnp.zeros((TOP_K * t + 1,), jnp.int32).at[jnp.where(ok, flat, TOP_K * t)].set(p)[:TOP_K * t]
    return tile_expert, tile_valid, src, ws.reshape(-1, 1), pos.reshape(t, TOP_K)


def _combine_kernel(x_ref, y0_ref, y1_ref, mod_ref, fg_ref, o_ref, *, final):
    xn = x_ref[...] + mod_ref[5:6, :] * (y0_ref[...] + y1_ref[...])
    o_ref[...] = _rms(xn, fg_ref[...]) if final else xn


def _combine(xs, y0, y1, modtab, fg, tm, nct, first_block, final):
    b, s, d = xs.shape
    kind = lambda i: jnp.where(i + first_block >= nct, 1, 0)
    row = lambda bi, i: (bi, i, 0)
    return pl.pallas_call(
        functools.partial(_combine_kernel, final=final),
        out_shape=jax.ShapeDtypeStruct((b, s, d), F32),
        grid=(b, s // tm),
        in_specs=[pl.BlockSpec((None, tm, d), row), pl.BlockSpec((None, tm, d), row),
                  pl.BlockSpec((None, tm, d), row),
                  pl.BlockSpec((None, None, 8, d), lambda bi, i: (bi, kind(i), 0, 0)),
                  _const_spec((1, d))],
        out_specs=pl.BlockSpec((None, tm, d), row),
        compiler_params=_cparams(("parallel", "parallel")),
        name="moe_combine",
    )(xs, y0, y1, modtab, fg.reshape(1, d))


def _pack_w_in(w):
    d = w.shape[0]
    mq, mk, mv, mo, mg, dq, dk, dv, gq, gk, gv, gg, ga, gt = jnp.split(
        w, [int(i) for i in np.cumsum(IN_SIZES)[:-1]], axis=1)

    def pad_heads(t):
        t = t.reshape(d, GL_HEADS, GL_DK)
        return jnp.pad(t, ((0, 0), (0, 0), (0, LANE - GL_DK))).reshape(d, GL_HEADS * LANE)

    def pad_cols(t):
        return jnp.pad(t, ((0, 0), (0, LANE - t.shape[1])))

    wa = jnp.concatenate([mq, mk, mv, pad_heads(gq), pad_heads(gk), gv, dv], axis=1).astype(BF16)
    wf = jnp.concatenate([dq, dk, mo, gg, gt, pad_cols(mg), pad_cols(ga)], axis=1).astype(BF16)
    return wa, wf


def kernel(x, c, ctx, c_ctx, w_mod, b_mod, norm_mix_g, norm_ffn_g, w_in, ml_gate_b, ml_norm_g, df_lambda,
           df_norm_g, gl_w_alpha, gl_b_alpha, gl_norm_g, w_branch, w_out, router_group_w, router_group_b,
           router_expert_w, router_expert_b, moe_w_gate, moe_w_up, moe_w_down, final_norm_g):
    b, n, d = x.shape
    n_ctx = ctx.shape[1]
    s = n_ctx + n
    depth = w_mod.shape[0]
    tm = 256 if n_ctx % 256 == 0 else 128
    assert n_ctx % tm == 0 and n % tm == 0 and n % GRID_W == 0 and n_ctx % CHUNK == 0 and d == D_MODEL
    nct = n_ctx // tm
    nch, nch_ctx = s // CHUNK, n_ctx // CHUNK
    rope = _rope_tables(n)
    xs = jnp.concatenate([ctx, x], axis=1)
    cc = jnp.zeros((32, d), F32).at[:b].set(c).at[b].set(c_ctx)
    for li in range(depth):
        last = li == depth - 1
        lam_init = 0.8 - 0.6 * math.exp(-0.3 * li)
        mod = _modulation(cc, w_mod[li], b_mod[li]).reshape(32, 6, d)
        mod = jnp.pad(mod, ((0, 0), (0, 2), (0, 0)))
        modtab = jnp.stack([jnp.broadcast_to(mod[b], (b, 8, d)), mod[:b]], axis=1)
        wa, wf = _pack_w_in(w_in[li])
        oa, of = _inproj(xs, modtab, norm_mix_g[li], wa, wf, tm, nct)
        grow = of[:, :, F_MG * LANE:F_MG * LANE + 4 * ML_HEADS]
        grow = grow.reshape(b, nch, CHUNK, 4 * ML_HEADS).transpose(0, 1, 3, 2)
        ml = _mlstm(oa, of, grow, ml_gate_b[li], ml_norm_g[li], nch_ctx, nch)
        df = _diff_attn(oa, of, rope, df_lambda[li], df_norm_g[li], tm, nct, n_ctx, lam_init)
        gl = _gla(oa, of, gl_w_alpha[li], gl_b_alpha[li], gl_norm_g[li], nch_ctx, nch)
        wr = jnp.zeros((d, LANE), F32).at[:, :N_GROUPS].set(router_group_w[li])
        wr = wr.at[:, N_GROUPS:N_GROUPS + N_EXPERTS].set(router_expert_w[li])
        br = jnp.zeros((1, LANE), F32).at[0, :N_GROUPS].set(router_group_b[li])
        br = br.at[0, N_GROUPS:N_GROUPS + N_EXPERTS].set(router_expert_b[li])
        first_block = nct if last else 0
        xn, h2, rid, rw = _merge(ml, df, gl, of, w_branch[li].astype(BF16), w_out[li].astype(BF16), xs, modtab,
                                 norm_ffn_g[li], wr, br, tm, nct, first_block)
        so = xn.shape[1]
        t = b * so
        eid = rid.reshape(t, LANE)[:, :TOP_K]
        wts = rw.reshape(t, LANE)[:, :TOP_K]
        te, tv, src, ws, pos = _route_plan(eid, wts, jnp.ones((t,), bool))
        xsorted = jnp.take(h2.reshape(t, d), src, axis=0)
        ys = _gmm(te, tv, xsorted, ws, moe_w_gate[li].astype(BF16), moe_w_up[li].astype(BF16),
                  moe_w_down[li].astype(BF16))
        y0 = jnp.take(ys, pos[:, 0], axis=0).reshape(b, so, d)
        y1 = jnp.take(ys, pos[:, 1], axis=0).reshape(b, so, d)
        xs = _combine(xn, y0, y1, modtab, final_norm_g, tm, nct, first_block, last)
    return xs
```

```python
import functools
import math

import numpy as np
import jax
import jax.numpy as jnp
from jax import lax
from jax.experimental import pallas as pl
from jax.experimental.pallas import tpu as pltpu

F32 = jnp.float32
BF16 = jnp.bfloat16
HIGHEST = lax.Precision.HIGHEST

D_MODEL = 1024
GRID_W = 64
NORM_EPS = 1e-6
CHUNK = 64
ROPE_THETA = 10000.0
ML_HEADS, ML_DK, ML_DV = 4, 128, 128
DF_HEADS, DF_HD, DF_DV = 4, 64, 128
GL_HEADS, GL_DK, GL_DV, GL_RANK, GL_TAU = 4, 64, 128, 16, 16.0
N_BRANCH, BRANCH_W = 3, 512
N_GROUPS, EXPERTS_PER_GROUP, TOP_K, D_EXPERT = 4, 8, 2, 512
N_EXPERTS = N_GROUPS * EXPERTS_PER_GROUP
IN_SIZES = (
    ML_HEADS * ML_DK, ML_HEADS * ML_DK, ML_HEADS * ML_DV, ML_HEADS * ML_DV, 4 * ML_HEADS,
    DF_HEADS * 2 * DF_HD, DF_HEADS * 2 * DF_HD, DF_HEADS * DF_DV,
    GL_HEADS * GL_DK, GL_HEADS * GL_DK, GL_HEADS * GL_DV, GL_HEADS * GL_DV, 2 * GL_RANK,
    N_BRANCH * D_MODEL,
)

LANE = 128
SUBLANE = 8
VMEM_LIMIT = 56 * 1024 * 1024
MOE_TILE = 256

A_MQ, A_MK, A_MV, A_GQ, A_GK, A_GV, A_DV = 0, 4, 8, 12, 14, 16, 20
A_WIDTH = 24 * LANE
F_DQ, F_DK, F_MO, F_GG, F_GT, F_MG, F_GA = 0, 4, 8, 12, 16, 40, 41
F_WIDTH = 42 * LANE
HEAD_W = 4 * LANE


def _cparams(sem):
    return pltpu.CompilerParams(dimension_semantics=sem, vmem_limit_bytes=VMEM_LIMIT)


def _const_spec(shape):
    nd = len(shape)
    return pl.BlockSpec(shape, lambda *_: (0,) * nd, pipeline_mode=pl.Buffered(1))


def _logsig(x):
    return jnp.minimum(x, 0.0) - jnp.log1p(jnp.exp(-jnp.abs(x)))


def _dot(a, b):
    return jnp.dot(a, b, preferred_element_type=F32)


def _dot_nt(a, b):
    return lax.dot_general(a, b, (((1,), (1,)), ((), ())), preferred_element_type=F32)


def _dot_tn(a, b):
    return lax.dot_general(a, b, (((0,), (0,)), ((), ())), preferred_element_type=F32)


def _rms(x, g):
    return x * lax.rsqrt(jnp.mean(x * x, axis=-1, keepdims=True) + NORM_EPS) * g


def _mod_kernel(c_ref, w_ref, b_ref, o_ref):
    c = c_ref[...]
    s = c * jax.nn.sigmoid(c)
    o_ref[...] = jnp.dot(s, w_ref[...], precision=HIGHEST, preferred_element_type=F32) + b_ref[...]


def _modulation(cc, w_mod, b_mod):
    rows, d = cc.shape
    n = w_mod.shape[1]
    tn = 512
    return pl.pallas_call(
        _mod_kernel,
        out_shape=jax.ShapeDtypeStruct((rows, n), F32),
        grid=(n // tn,),
        in_specs=[pl.BlockSpec((rows, d), lambda j: (0, 0)),
                  pl.BlockSpec((d, tn), lambda j: (0, j)),
                  pl.BlockSpec((1, tn), lambda j: (0, j))],
        out_specs=pl.BlockSpec((rows, tn), lambda j: (0, j)),
        compiler_params=_cparams(("parallel",)),
        name="adaln_mod",
    )(cc, w_mod, b_mod.reshape(1, n))


def _inproj_kernel(x_ref, mod_ref, g_ref, wa_ref, wf_ref, oa_ref, of_ref):
    y = _rms(x_ref[...], g_ref[...])
    h = (y * (1.0 + mod_ref[1:2, :]) + mod_ref[0:1, :]).astype(BF16)
    oa_ref[...] = _dot(h, wa_ref[...]).astype(BF16)
    of_ref[...] = _dot(h, wf_ref[...])


def _inproj(xs, modtab, g, wa, wf, tm, nct):
    b, s, d = xs.shape
    kind = lambda i: jnp.where(i >= nct, 1, 0)
    return pl.pallas_call(
        _inproj_kernel,
        out_shape=(jax.ShapeDtypeStruct((b, s, A_WIDTH), BF16), jax.ShapeDtypeStruct((b, s, F_WIDTH), F32)),
        grid=(b, s // tm),
        in_specs=[pl.BlockSpec((None, tm, d), lambda bi, i: (bi, i, 0)),
                  pl.BlockSpec((None, None, SUBLANE, d), lambda bi, i: (bi, kind(i), 0, 0)),
                  _const_spec((1, d)), _const_spec((d, A_WIDTH)), _const_spec((d, F_WIDTH))],
        out_specs=(pl.BlockSpec((None, tm, A_WIDTH), lambda bi, i: (bi, i, 0)),
                   pl.BlockSpec((None, tm, F_WIDTH), lambda bi, i: (bi, i, 0))),
        compiler_params=_cparams(("parallel", "parallel")),
        name="in_proj",
    )(xs, modtab, g.reshape(1, d), wa, wf)


def _chunk_ids(t, nct, nch):
    return t, jnp.where(t < nct, nct - 1 - t, nch - 1 - (t - nct))


def _head_cols(h):
    return slice(h * LANE, (h + 1) * LANE)


def _mlstm_kernel(q_ref, kt_ref, v_ref, o_ref, gcol_ref, grow_ref, bcol_ref, brow_ref, ng_ref, out_ref,
                  hf_ref, hb_ref, ct_ref, *, nct, nch):
    L = CHUNK
    H = ML_HEADS
    jj = lax.broadcasted_iota(jnp.int32, (L, L), 0)
    ii = lax.broadcasted_iota(jnp.int32, (L, L), 1)
    lane = lax.broadcasted_iota(jnp.int32, (L, LANE), 1)
    ones_col = jnp.where(lane == 0, 1.0, 0.0).astype(BF16)
    scale = ML_DK ** -0.5
    vis = ((ii <= jj), (ii >= jj))
    cum = tuple(jnp.where(m_, 1.0, 0.0).astype(BF16) for m_ in vis)
    cum_t = tuple(jnp.where(m_, 1.0, 0.0).astype(BF16) for m_ in ((jj <= ii), (jj >= ii)))
    ct_ref[...] = jnp.zeros(ct_ref.shape, F32)

    def split2(x):
        hi = x.astype(BF16)
        return hi, (x - hi.astype(F32)).astype(BF16)

    def step(t, carry):
        chunks = _chunk_ids(t, nct, nch)
        rows = [pl.ds(pl.multiple_of(c * L, L), L) for c in chunks]
        chains = [(d, h) for d in range(2) for h in range(H)]
        q, kt, vext, ct, sq, qc = {}, {}, {}, {}, {}, {}
        for d, h in chains:
            q[d, h] = q_ref[rows[d], _head_cols(h)]
            kt[d, h] = kt_ref[chunks[d], h]
            vext[d, h] = jnp.concatenate([v_ref[rows[d], _head_cols(h)], ones_col], axis=1)
            ct[d, h] = ct_ref[d, h]
            sq[d, h] = _dot(q[d, h], kt[d, h])
            qc[d, h] = _dot(q[d, h], ct[d, h].astype(BF16))
        gr, bc_all, br_all = [], [], []
        for d in range(2):
            gc = gcol_ref[rows[d], :] + bcol_ref[...]
            gr.append(grow_ref[chunks[d]] + brow_ref[...])
            chi, clo = split2(_logsig(gc))
            rhi, rlo = split2(_logsig(gr[d]))
            bc_all.append(_dot(cum[d], chi) + _dot(cum[d], clo))
            br_all.append(_dot(rhi, cum_t[d]) + _dot(rlo, cum_t[d]))
        w, w_inter, einv, decay, kw, new = {}, {}, {}, {}, {}, []
        for d, h in chains:
            m = carry[d * H + h]
            gi = 2 * d * H + h
            b_row = br_all[d][gi + H:gi + H + 1, :]
            b_col = bc_all[d][:, gi + H:gi + H + 1]
            b_last = b_row[:, L - 1:L] if d == 0 else b_row[:, 0:1]
            u_row = gr[d][gi:gi + 1, :] - b_row
            g_col = jnp.maximum(jnp.max(jnp.where(vis[d], u_row, -jnp.inf), axis=1, keepdims=True), m)
            w[d, h] = jnp.exp(jnp.where(vis[d], u_row - g_col, -jnp.inf)) * scale
            w_inter[d, h] = jnp.exp(m - g_col)
            einv[d, h] = jnp.exp(-(b_col + g_col))
            m_new = b_last + jnp.maximum(m, jnp.max(u_row, axis=1, keepdims=True))
            wk_row = jnp.exp(b_last + u_row - m_new) * scale
            decay[d, h] = jnp.exp(b_last + m - m_new)
            kw[d, h] = (kt[d, h].astype(F32) * wk_row).astype(BF16)
            new.append(m_new)
        for d, h in chains:
            ct_ref[d, h] = decay[d, h] * ct[d, h] + _dot(kw[d, h], vext[d, h])
        sv = {}
        for d, h in chains:
            sv[d, h] = _dot((sq[d, h] * w[d, h]).astype(BF16), vext[d, h])
        for d, h in chains:
            num = sv[d, h][:, :ML_DV] + w_inter[d, h] * qc[d, h][:, :ML_DV]
            den = sv[d, h][:, ML_DV:ML_DV + 1] + w_inter[d, h] * qc[d, h][:, ML_DV:ML_DV + 1]
            hh = num / jnp.maximum(jnp.abs(den), einv[d, h])
            if d == 0:
                hf_ref[rows[d], _head_cols(h)] = hh
            else:
                hb_ref[rows[d], _head_cols(h)] = hh
        return tuple(new)

    zero = jnp.zeros((1, 1), F32)
    lax.fori_loop(0, nch, step, (zero,) * (2 * H))
    for h in range(H):
        y = _rms(hf_ref[:, _head_cols(h)] + hb_ref[:, _head_cols(h)], ng_ref[...])
        out_ref[:, _head_cols(h)] = (y * jax.nn.sigmoid(o_ref[:, _head_cols(h)])).astype(BF16)


def _chunk_transposed(oa, base, nch):
    b = oa.shape[0]
    t = oa[:, :, base * LANE:base * LANE + HEAD_W].reshape(b, nch, CHUNK, 4, LANE)
    return t.transpose(0, 1, 3, 4, 2)


def _mlstm(oa, of, grow, gate_b, norm_g, nct, nch):
    b, s, _ = oa.shape
    bcol = jnp.zeros((1, LANE), F32).at[0, :4 * ML_HEADS].set(gate_b)
    brow = gate_b.reshape(4 * ML_HEADS, 1)
    sect = lambda base: (lambda bi: (bi, 0, base // 4))
    kt = _chunk_transposed(oa, A_MK, nch)
    return pl.pallas_call(
        functools.partial(_mlstm_kernel, nct=nct, nch=nch),
        out_shape=jax.ShapeDtypeStruct((b, s, HEAD_W), BF16),
        grid=(b,),
        in_specs=[pl.BlockSpec((None, s, HEAD_W), sect(A_MQ)),
                  pl.BlockSpec((None, nch, ML_HEADS, ML_DK, CHUNK), lambda bi: (bi, 0, 0, 0, 0)),
                  pl.BlockSpec((None, s, HEAD_W), sect(A_MV)),
                  pl.BlockSpec((None, s, HEAD_W), sect(F_MO)),
                  pl.BlockSpec((None, s, LANE), lambda bi: (bi, 0, F_MG)),
                  pl.BlockSpec((None, nch, 4 * ML_HEADS, CHUNK), lambda bi: (bi, 0, 0, 0)),
                  pl.BlockSpec((1, LANE), lambda bi: (0, 0)),
                  pl.BlockSpec((4 * ML_HEADS, 1), lambda bi: (0, 0)),
                  pl.BlockSpec((1, ML_DV), lambda bi: (0, 0))],
        out_specs=pl.BlockSpec((None, s, HEAD_W), lambda bi: (bi, 0, 0)),
        scratch_shapes=[pltpu.VMEM((s, HEAD_W), F32), pltpu.VMEM((s, HEAD_W), F32),
                        pltpu.VMEM((2, ML_HEADS, ML_DK, 2 * LANE), F32)],
        compiler_params=_cparams(("parallel",)),
        name="mlstm",
    )(oa, kt, oa, of, of, grow, bcol, brow, norm_g.reshape(1, ML_DV))


def _gla_tables():
    L = CHUNK
    G = np.zeros((2, 7 * L, L), np.float32)
    lvl = np.full((2, L, L), 7, np.int32)
    for d in range(2):
        for p in range(L):
            if d == 0:
                G[d, p, :p + 1] = 1
            else:
                G[d, p, p:] = 1
        for li, s in enumerate((32, 16, 8, 4, 2, 1)):
            for p in range(L):
                base = (p // (2 * s)) * 2 * s
                row = (li + 1) * L + p
                if d == 0:
                    mid = base + s
                    if p >= mid:
                        G[d, row, mid + 1:p + 1] = 1
                    else:
                        G[d, row, p + 1:mid + 1] = 1
                else:
                    mid = base + s - 1
                    if p <= mid:
                        G[d, row, p:mid] = 1
                    else:
                        G[d, row, mid:p] = 1
            blk = np.arange(L) // (2 * s)
            upper = (np.arange(L) % (2 * s)) >= s
            same = blk[:, None] == blk[None, :]
            if d == 0:
                sel = same & upper[:, None] & ~upper[None, :]
            else:
                sel = same & ~upper[:, None] & upper[None, :]
            lvl[d][sel] = li
        lvl[d][np.arange(L), np.arange(L)] = 6
    return G, lvl


GL_PAIRS = GL_HEADS * GL_DK // LANE


def _gla_kernel(q_ref, k_ref, v_ref, vt_ref, g_ref, a_ref, wa_ref, ba_ref, gm_ref, lvl_ref, ng_ref, out_ref,
                acc_ref, la_ref, st_ref, *, nct, nch):
    L = CHUNK
    for d in range(2):
        pre = jnp.dot(a_ref[...], wa_ref[d], precision=HIGHEST, preferred_element_type=F32) + ba_ref[d]
        la_ref[d] = _logsig(pre) * (1.0 / GL_TAU)
    acc_ref[...] = jnp.zeros(acc_ref.shape, F32)
    st_ref[...] = jnp.zeros(st_ref.shape, F32)
    first = lax.broadcasted_iota(jnp.int32, (L, LANE), 1) < GL_DK
    first2 = lax.broadcasted_iota(jnp.int32, (GL_DV, LANE), 1) < GL_DK

    def split(x):
        return jnp.concatenate([jnp.where(first, x, 0.0), jnp.where(first, 0.0, x)], axis=0).astype(BF16)

    def step(t, carry):
        chunks = _chunk_ids(t, nct, nch)
        rows = [pl.ds(pl.multiple_of(c * L, L), L) for c in chunks]
        chains = [(d, p) for d in range(2) for p in range(GL_PAIRS)]
        xs = []
        for d in range(2):
            lac = la_ref[d, rows[d], :]
            hi = lac.astype(BF16)
            lo = (lac - hi.astype(F32)).astype(BF16)
            xs.append(_dot(gm_ref[d], hi) + _dot(gm_ref[d], lo))
        q, k, cs, tot, st, inter, upd = {}, {}, {}, {}, {}, {}, {}
        for d, p in chains:
            q[d, p] = q_ref[rows[d], _head_cols(p)].astype(F32) * (GL_DK ** -0.5)
            k[d, p] = k_ref[rows[d], _head_cols(p)].astype(F32)
            cs[d, p] = xs[d][0:L, _head_cols(p)]
            tot[d, p] = cs[d, p][L - 1:L] if d == 0 else cs[d, p][0:1]
            st[d, p] = st_ref[d, p]
        for d, p in chains:
            inter[d, p] = _dot_nt(split(q[d, p] * jnp.exp(cs[d, p])), st[d, p].astype(BF16))
            ke = (k[d, p] * jnp.exp(tot[d, p] - cs[d, p])).astype(BF16)
            u = [_dot(vt_ref[chunks[d], 2 * p + hh], ke) for hh in range(2)]
            st_ref[d, p] = st[d, p] * jnp.exp(tot[d, p]) + jnp.where(first2, u[0], u[1])
        amat = {}
        for d, p in chains:
            amat[d, p] = jnp.where(lvl_ref[d] == 6, _dot_nt(split(q[d, p]), k[d, p].astype(BF16)), 0.0)
        for li in range(6):
            for d, p in chains:
                e = jnp.exp(xs[d][(li + 1) * L:(li + 2) * L, _head_cols(p)])
                lev = _dot_nt(split(q[d, p] * e), (k[d, p] * e).astype(BF16))
                amat[d, p] = jnp.where(lvl_ref[d] == li, lev, amat[d, p])
        for d, p in chains:
            a = amat[d, p].astype(BF16)
            for hh in range(2):
                cols = _head_cols(2 * p + hh)
                acc_ref[rows[d], cols] += (_dot(a[hh * L:(hh + 1) * L], v_ref[rows[d], cols])
                                           + inter[d, p][hh * L:(hh + 1) * L])
        return carry

    lax.fori_loop(0, nch, step, 0)
    for h in range(GL_HEADS):
        cols = _head_cols(h)
        g = g_ref[:, cols]
        out_ref[:, cols] = (_rms(acc_ref[:, cols], ng_ref[...]) * (g * jax.nn.sigmoid(g))).astype(BF16)


def _gla(oa, of, w_alpha, b_alpha, norm_g, nct, nch):
    b, s, _ = oa.shape
    gmat, lvl = _gla_tables()
    lvl = np.concatenate([lvl, lvl], axis=1)
    qk_w = GL_HEADS * GL_DK
    wa = jnp.zeros((2, LANE, qk_w), F32)
    for d in range(2):
        wa = wa.at[d, d * GL_RANK:(d + 1) * GL_RANK, :].set(w_alpha[d])
    ba = b_alpha.reshape(2, 1, qk_w)
    vt = _chunk_transposed(oa, A_GV, nch)
    full = lambda nd: (lambda bi: (0,) * nd)
    return pl.pallas_call(
        functools.partial(_gla_kernel, nct=nct, nch=nch),
        out_shape=jax.ShapeDtypeStruct((b, s, HEAD_W), BF16),
        grid=(b,),
        in_specs=[pl.BlockSpec((None, s, qk_w), lambda bi: (bi, 0, A_GQ * LANE // qk_w)),
                  pl.BlockSpec((None, s, qk_w), lambda bi: (bi, 0, A_GK * LANE // qk_w)),
                  pl.BlockSpec((None, s, HEAD_W), lambda bi: (bi, 0, A_GV // 4)),
                  pl.BlockSpec((None, nch, GL_HEADS, GL_DV, CHUNK), lambda bi: (bi, 0, 0, 0, 0)),
                  pl.BlockSpec((None, s, HEAD_W), lambda bi: (bi, 0, F_GG // 4)),
                  pl.BlockSpec((None, s, LANE), lambda bi: (bi, 0, F_GA)),
                  pl.BlockSpec((2, LANE, qk_w), full(3)),
                  pl.BlockSpec((2, 1, qk_w), full(3)),
                  pl.BlockSpec((2, 7 * CHUNK, CHUNK), full(3)),
                  pl.BlockSpec((2, 2 * CHUNK, CHUNK), full(3)),
                  pl.BlockSpec((1, GL_DV), full(2))],
        out_specs=pl.BlockSpec((None, s, HEAD_W), lambda bi: (bi, 0, 0)),
        scratch_shapes=[pltpu.VMEM((s, HEAD_W), F32), pltpu.VMEM((2, s, qk_w), F32),
                        pltpu.VMEM((2, GL_PAIRS, GL_DV, LANE), F32)],
        compiler_params=_cparams(("parallel",)),
        name="gla",
    )(oa, oa, oa, vt, of, of, wa, ba, jnp.asarray(gmat, BF16), jnp.asarray(lvl), norm_g.reshape(1, GL_DV))


def _rope_tables(n):
    t = np.arange(n)
    n_freq = DF_HD // 4
    inv = jnp.asarray(ROPE_THETA, F32) ** (-jnp.arange(n_freq, dtype=F32) / n_freq)
    ang_r = jnp.asarray(t // GRID_W, F32)[:, None] * inv
    ang_c = jnp.asarray(t % GRID_W, F32)[:, None] * inv
    ang = jnp.concatenate([ang_r, ang_r, ang_c, ang_c] * 2, axis=1)
    first = (np.arange(LANE) % 32) < 16
    cos, sin = jnp.cos(ang), jnp.sin(ang)
    return cos, jnp.where(first, -sin, 0.0), jnp.where(first, 0.0, sin)


DF_ROW_GROUPS = 4


def _rope(x, cos, sa, sb):
    return x * cos + pltpu.roll(x, LANE - 16, 1) * sa + pltpu.roll(x, 16, 1) * sb


def _diff_kernel(q_ref, k_ref, v_ref, cq_ref, saq_ref, sbq_ref, ck_ref, sak_ref, sbk_ref, lam_ref, ng_ref,
                 out_ref, kr_ref, *, nct, n_ctx, lam_init):
    s = k_ref.shape[0]
    tq = q_ref.shape[0]
    i = pl.program_id(2)

    @pl.when(i == 0)
    def _():
        kr_ref[0:n_ctx, :] = k_ref[0:n_ctx, :].astype(BF16)
        kr_ref[n_ctx:s, :] = _rope(k_ref[n_ctx:s, :], ck_ref[...], sak_ref[...], sbk_ref[...]).astype(BF16)

    lp = lam_ref[...]
    lam = (jnp.exp(jnp.sum(lp[0:1] * lp[1:2], axis=1, keepdims=True))
           - jnp.exp(jnp.sum(lp[2:3] * lp[3:4], axis=1, keepdims=True)) + lam_init)
    lane = lax.broadcasted_iota(jnp.int32, q_ref.shape, 1)

    def attend(qb, nk):
        qb = qb * (DF_HD ** -0.5)
        qs = jnp.concatenate([jnp.where(lane < DF_HD, qb, 0.0), jnp.where(lane >= DF_HD, qb, 0.0)],
                             axis=0).astype(BF16)
        rg = 2 * tq // DF_ROW_GROUPS
        scs = [_dot_nt(qs[g * rg:(g + 1) * rg], kr_ref[0:nk, :]) for g in range(DF_ROW_GROUPS)]
        ovs = []
        for sc in scs:
            p = jnp.exp(sc - jnp.max(sc, axis=1, keepdims=True))
            rl = 1.0 / jnp.sum(p, axis=1, keepdims=True)
            ovs.append((p.astype(BF16), rl))
        ov = jnp.concatenate([_dot(p, v_ref[0:nk, :]) * rl for p, rl in ovs], axis=0)
        o = ov[0:tq] - lam * ov[tq:2 * tq]
        out_ref[...] = (_rms(o, ng_ref[...]) * (1.0 - lam_init)).astype(BF16)

    @pl.when(i < nct)
    def _():
        attend(q_ref[...], n_ctx)

    @pl.when(i >= nct)
    def _():
        attend(_rope(q_ref[...], cq_ref[...], saq_ref[...], sbq_ref[...]), s)


def _diff_attn(oa, of, rope, df_lambda, norm_g, tq, nct, n_ctx, lam_init):
    b, s, _ = oa.shape
    n = s - n_ctx
    cos, sa, sb = rope
    lam_p = jnp.zeros((4, LANE), F32).at[:, :DF_HD].set(df_lambda)
    qblk = lambda bi, h, i: (jnp.maximum(i - nct, 0), 0)
    full = lambda bi, h, i: (0, 0)
    return pl.pallas_call(
        functools.partial(_diff_kernel, nct=nct, n_ctx=n_ctx, lam_init=lam_init),
        out_shape=jax.ShapeDtypeStruct((b, s, DF_HEADS * DF_DV), BF16),
        grid=(b, DF_HEADS, s // tq),
        in_specs=[pl.BlockSpec((None, tq, LANE), lambda bi, h, i: (bi, i, F_DQ + h)),
                  pl.BlockSpec((None, s, LANE), lambda bi, h, i: (bi, 0, F_DK + h)),
                  pl.BlockSpec((None, s, LANE), lambda bi, h, i: (bi, 0, A_DV + h)),
                  pl.BlockSpec((tq, LANE), qblk), pl.BlockSpec((tq, LANE), qblk), pl.BlockSpec((tq, LANE), qblk),
                  pl.BlockSpec((n, LANE), full), pl.BlockSpec((n, LANE), full), pl.BlockSpec((n, LANE), full),
                  pl.BlockSpec((4, LANE), full), pl.BlockSpec((1, DF_DV), full)],
        out_specs=pl.BlockSpec((None, tq, LANE), lambda bi, h, i: (bi, i, h)),
        scratch_shapes=[pltpu.VMEM((s, LANE), BF16)],
        compiler_params=_cparams(("parallel", "parallel", "arbitrary")),
        name="diff_attn",
    )(of, of, oa, cos, sa, sb, cos, sa, sb, lam_p, norm_g.reshape(1, DF_DV))


def _merge_kernel(ml_ref, df_ref, gl_ref, g0_ref, g1_ref, g2_ref, wb_ref, wo_ref, x_ref, mod_ref, nf_ref,
                  wr_ref, br_ref, tri_ref, xo_ref, h2_ref, rid_ref, rw_ref, cnt_ref):
    y = (jax.nn.sigmoid(g0_ref[...]) * _dot(ml_ref[...], wb_ref[0])
         + jax.nn.sigmoid(g1_ref[...]) * _dot(df_ref[...], wb_ref[1])
         + jax.nn.sigmoid(g2_ref[...]) * _dot(gl_ref[...], wb_ref[2]))
    xn = x_ref[...] + mod_ref[2:3, :] * _dot(y.astype(BF16), wo_ref[...])
    xo_ref[...] = xn
    h2 = _rms(xn, nf_ref[...]) * (1.0 + mod_ref[4:5, :]) + mod_ref[3:4, :]
    h2_ref[...] = h2.astype(BF16)
    logits = jnp.dot(h2, wr_ref[...], precision=HIGHEST, preferred_element_type=F32) + br_ref[...]
    lane = lax.broadcasted_iota(jnp.int32, logits.shape, 1)
    lane_f = lane.astype(F32)
    neg = -jnp.inf

    def first_max(vals):
        mx = jnp.max(vals, axis=1, keepdims=True)
        return mx, jnp.min(jnp.where(vals == mx, lane_f, float(LANE)), axis=1, keepdims=True)

    is_grp = lane < N_GROUPS
    gmax, gidx = first_max(jnp.where(is_grp, logits, neg))
    pg_top = 1.0 / jnp.sum(jnp.where(is_grp, jnp.exp(logits - gmax), 0.0), axis=1, keepdims=True)
    lo = N_GROUPS + gidx * EXPERTS_PER_GROUP
    in_grp = (lane_f >= lo) & (lane_f < lo + EXPERTS_PER_GROUP)
    le = jnp.where(in_grp, logits, neg)
    m1, e1 = first_max(le)
    m2, e2 = first_max(jnp.where(lane_f == e1, neg, le))
    r = jnp.exp(m2 - m1)
    w1 = pg_top / (1.0 + r)
    w2 = pg_top * r / (1.0 + r)

    @pl.when((pl.program_id(0) == 0) & (pl.program_id(1) == 0))
    def _():
        cnt_ref[...] = jnp.zeros(cnt_ref.shape, F32)

    hot1 = jnp.where(lane_f == e1, 1.0, 0.0)
    hot2 = jnp.where(lane_f == e2, 1.0, 0.0)
    tot1 = jnp.sum(hot1, axis=0, keepdims=True)
    tot2 = jnp.sum(hot2, axis=0, keepdims=True)
    cnt = cnt_ref[...]
    before1 = cnt + _dot(tri_ref[...], hot1.astype(BF16))
    before2 = cnt + tot1 + _dot(tri_ref[...], hot2.astype(BF16))
    rank1 = jnp.sum(hot1 * before1, axis=1, keepdims=True)
    rank2 = jnp.sum(hot2 * before2, axis=1, keepdims=True)
    cnt_ref[...] = cnt + tot1 + tot2
    rid = jnp.where(lane == 0, e1 - N_GROUPS, jnp.where(lane == 1, e2 - N_GROUPS,
                    jnp.where(lane == 2, rank1, jnp.where(lane == 3, rank2, 0.0))))
    rid_ref[...] = rid.astype(jnp.int32)
    rw_ref[...] = jnp.where(lane == 0, w1, jnp.where(lane == 1, w2, 0.0))


def _merge(ml, df, gl, of, wb, wo, xs, modtab, nf, wr, br, tm, nct, first_block):
    b, s, d = xs.shape
    nb = s // tm - first_block
    so = nb * tm
    kind = lambda i: jnp.where(i + first_block >= nct, 1, 0)
    row = lambda bi, i: (bi, i + first_block, 0)
    gate = lambda br_: (lambda bi, i: (bi, i + first_block, F_GT // (d // LANE) + br_))
    outrow = lambda bi, i: (bi, i, 0)
    tri = jnp.asarray(np.tril(np.ones((tm, tm), np.float32), -1), BF16)
    return pl.pallas_call(
        _merge_kernel,
        out_shape=(jax.ShapeDtypeStruct((b, so, d), F32), jax.ShapeDtypeStruct((b, so, d), BF16),
                   jax.ShapeDtypeStruct((b, so, LANE), jnp.int32), jax.ShapeDtypeStruct((b, so, LANE), F32),
                   jax.ShapeDtypeStruct((1, LANE), F32)),
        grid=(b, nb),
        in_specs=[pl.BlockSpec((None, tm, BRANCH_W), row), pl.BlockSpec((None, tm, BRANCH_W), row),
                  pl.BlockSpec((None, tm, BRANCH_W), row),
                  pl.BlockSpec((None, tm, d), gate(0)), pl.BlockSpec((None, tm, d), gate(1)),
                  pl.BlockSpec((None, tm, d), gate(2)),
                  _const_spec((N_BRANCH, BRANCH_W, d)), _const_spec((d, d)),
                  pl.BlockSpec((None, tm, d), row),
                  pl.BlockSpec((None, None, SUBLANE, d), lambda bi, i: (bi, kind(i), 0, 0)),
                  _const_spec((1, d)), _const_spec((d, LANE)), _const_spec((1, LANE)), _const_spec((tm, tm))],
        out_specs=(pl.BlockSpec((None, tm, d), outrow), pl.BlockSpec((None, tm, d), outrow),
                   pl.BlockSpec((None, tm, LANE), outrow), pl.BlockSpec((None, tm, LANE), outrow),
                   pl.BlockSpec((1, LANE), lambda bi, i: (0, 0))),
        compiler_params=_cparams(("arbitrary", "arbitrary")),
        name="merge_route",
    )(ml, df, gl, of, of, of, wb, wo, xs, modtab, nf.reshape(1, d), wr, br, tri)


def _gmm_kernel(te_ref, tv_ref, x_ref, wg_ref, wu_ref, wd_ref, y_ref):
    i = pl.program_id(0)

    @pl.when(tv_ref[i] > 0)
    def _():
        x = x_ref[...]
        a = _dot(x, wg_ref[...])
        hid = (a * jax.nn.sigmoid(a)) * _dot(x, wu_ref[...])
        y_ref[...] = _dot(hid.astype(BF16), wd_ref[...]).astype(y_ref.dtype)

    @pl.when(tv_ref[i] == 0)
    def _():
        y_ref[...] = jnp.zeros(y_ref.shape, y_ref.dtype)


def _gmm(tile_expert, tile_valid, xs, wg, wu, wd):
    npad, d = xs.shape
    tm = MOE_TILE
    de = wg.shape[2]
    return pl.pallas_call(
        _gmm_kernel,
        out_shape=jax.ShapeDtypeStruct((npad, d), BF16),
        grid_spec=pltpu.PrefetchScalarGridSpec(
            num_scalar_prefetch=2,
            grid=(npad // tm,),
            in_specs=[pl.BlockSpec((tm, d), lambda i, te, tv: (i, 0)),
                      pl.BlockSpec((None, d, de), lambda i, te, tv: (te[i], 0, 0)),
                      pl.BlockSpec((None, d, de), lambda i, te, tv: (te[i], 0, 0)),
                      pl.BlockSpec((None, de, d), lambda i, te, tv: (te[i], 0, 0))],
            out_specs=pl.BlockSpec((tm, d), lambda i, te, tv: (i, 0))),
        compiler_params=_cparams(("arbitrary",)),
        name="moe_gmm",
    )(tile_expert, tile_valid, xs, wg, wu, wd)


def _route_plan(rid, cnt):
    t = rid.shape[0]
    tm = MOE_TILE
    n_tiles = (TOP_K * t + tm - 1) // tm + N_EXPERTS
    counts = cnt[0, N_GROUPS:N_GROUPS + N_EXPERTS].astype(jnp.int32)
    tiles_e = (counts + tm - 1) // tm
    tile_end = jnp.cumsum(tiles_e)
    pad_start = (tile_end - tiles_e) * tm
    eid, rank = rid[:, 0:TOP_K], rid[:, TOP_K:2 * TOP_K]
    pos = rank + jnp.sum(jnp.where(eid[:, :, None] == jnp.arange(N_EXPERTS), pad_start, 0), axis=-1)
    tile = jnp.arange(n_tiles, dtype=jnp.int32)
    tile_expert = jnp.minimum(jnp.sum(tile[:, None] >= tile_end[None, :], axis=1), N_EXPERTS - 1).astype(jnp.int32)
    tile_valid = (tile < tile_end[-1]).astype(jnp.int32)
    return tile_expert, tile_valid, pos.astype(jnp.int32), n_tiles * tm


def _combine_kernel(x_ref, y0_ref, y1_ref, rw_ref, mod_ref, fg_ref, o_ref, *, final):
    rw = rw_ref[...]
    y = rw[:, 0:1] * y0_ref[...].astype(F32) + rw[:, 1:2] * y1_ref[...].astype(F32)
    xn = x_ref[...] + mod_ref[5:6, :] * y
    o_ref[...] = _rms(xn, fg_ref[...]) if final else xn


def _combine(xs, y0, y1, rw, modtab, fg, tm, nct, first_block, final):
    b, s, d = xs.shape
    kind = lambda i: jnp.where(i + first_block >= nct, 1, 0)
    row = lambda bi, i: (bi, i, 0)
    return pl.pallas_call(
        functools.partial(_combine_kernel, final=final),
        out_shape=jax.ShapeDtypeStruct((b, s, d), F32),
        grid=(b, s // tm),
        in_specs=[pl.BlockSpec((None, tm, d), row), pl.BlockSpec((None, tm, d), row),
                  pl.BlockSpec((None, tm, d), row), pl.BlockSpec((None, tm, LANE), row),
                  pl.BlockSpec((None, None, SUBLANE, d), lambda bi, i: (bi, kind(i), 0, 0)),
                  _const_spec((1, d))],
        out_specs=pl.BlockSpec((None, tm, d), row),
        compiler_params=_cparams(("parallel", "parallel")),
        name="moe_combine",
    )(xs, y0, y1, rw, modtab, fg.reshape(1, d))


def _pack_w_in(w):
    d = w.shape[0]
    mq, mk, mv, mo, mg, dq, dk, dv, gq, gk, gv, gg, ga, gt = jnp.split(
        w, [int(i) for i in np.cumsum(IN_SIZES)[:-1]], axis=1)

    def pad_cols(t):
        return jnp.pad(t, ((0, 0), (0, LANE - t.shape[1])))

    wa = jnp.concatenate([mq, mk, mv, gq, gk, gv, dv], axis=1).astype(BF16)
    wf = jnp.concatenate([dq, dk, mo, gg, gt, pad_cols(mg), pad_cols(ga)], axis=1).astype(BF16)
    return wa, wf


def _dispatch_rows(h2, pos, npad):
    t = h2.shape[0]
    tok = jnp.arange(t, dtype=jnp.int32)
    src = jnp.zeros((npad,), jnp.int32).at[pos[:, 0]].set(tok).at[pos[:, 1]].set(tok)
    return jnp.take(h2, src, axis=0)


def kernel(x, c, ctx, c_ctx, w_mod, b_mod, norm_mix_g, norm_ffn_g, w_in, ml_gate_b, ml_norm_g, df_lambda,
           df_norm_g, gl_w_alpha, gl_b_alpha, gl_norm_g, w_branch, w_out, router_group_w, router_group_b,
           router_expert_w, router_expert_b, moe_w_gate, moe_w_up, moe_w_down, final_norm_g):
    b, n, d = x.shape
    n_ctx = ctx.shape[1]
    s = n_ctx + n
    depth = w_mod.shape[0]
    tm = 256 if n_ctx % 256 == 0 else 128
    assert n_ctx % tm == 0 and n % tm == 0 and n % GRID_W == 0 and n_ctx % CHUNK == 0 and d == D_MODEL
    nct = n_ctx // tm
    nch, nch_ctx = s // CHUNK, n_ctx // CHUNK
    rope = _rope_tables(n)
    xs = jnp.concatenate([ctx, x], axis=1)
    mod_rows = -(-(b + 1) // SUBLANE) * SUBLANE
    cc = jnp.zeros((mod_rows, d), F32).at[:b].set(c).at[b].set(c_ctx)
    for li in range(depth):
        last = li == depth - 1
        lam_init = 0.8 - 0.6 * math.exp(-0.3 * li)
        mod = _modulation(cc, w_mod[li], b_mod[li]).reshape(mod_rows, 6, d)
        mod = jnp.pad(mod, ((0, 0), (0, SUBLANE - 6), (0, 0)))
        modtab = jnp.stack([jnp.broadcast_to(mod[b], (b, SUBLANE, d)), mod[:b]], axis=1)
        wa, wf = _pack_w_in(w_in[li])
        oa, of = _inproj(xs, modtab, norm_mix_g[li], wa, wf, tm, nct)
        grow = of[:, :, F_MG * LANE:F_MG * LANE + 4 * ML_HEADS]
        grow = grow.reshape(b, nch, CHUNK, 4 * ML_HEADS).transpose(0, 1, 3, 2)
        ml = _mlstm(oa, of, grow, ml_gate_b[li], ml_norm_g[li], nch_ctx, nch)
        df = _diff_attn(oa, of, rope, df_lambda[li], df_norm_g[li], tm, nct, n_ctx, lam_init)
        gl = _gla(oa, of, gl_w_alpha[li], gl_b_alpha[li], gl_norm_g[li], nch_ctx, nch)
        wr = jnp.zeros((d, LANE), F32).at[:, :N_GROUPS].set(router_group_w[li])
        wr = wr.at[:, N_GROUPS:N_GROUPS + N_EXPERTS].set(router_expert_w[li])
        br = jnp.zeros((1, LANE), F32).at[0, :N_GROUPS].set(router_group_b[li])
        br = br.at[0, N_GROUPS:N_GROUPS + N_EXPERTS].set(router_expert_b[li])
        first_block = nct if last else 0
        xn, h2, rid, rw, cnt = _merge(ml, df, gl, of, w_branch[li].astype(BF16), w_out[li].astype(BF16), xs,
                                      modtab, norm_ffn_g[li], wr, br, tm, nct, first_block)
        so = xn.shape[1]
        t = b * so
        te, tv, pos, npad = _route_plan(rid.reshape(t, LANE), cnt)
        xsorted = _dispatch_rows(h2.reshape(t, d), pos, npad)
        ys = _gmm(te, tv, xsorted, moe_w_gate[li].astype(BF16), moe_w_up[li].astype(BF16),
                  moe_w_down[li].astype(BF16))
        y0 = jnp.take(ys, pos[:, 0], axis=0).reshape(b, so, d)
        y1 = jnp.take(ys, pos[:, 1], axis=0).reshape(b, so, d)
        xs = _combine(xn, y0, y1, rw, modtab, final_norm_g, tm, nct, first_block, last)
    return xs
```

```python
import functools
import math

import numpy as np
import jax
import jax.numpy as jnp
from jax import lax
from jax.experimental import pallas as pl
from jax.experimental.pallas import tpu as pltpu

F32 = jnp.float32
BF16 = jnp.bfloat16
HIGHEST = lax.Precision.HIGHEST

D_MODEL = 1024
GRID_W = 64
NORM_EPS = 1e-6
CHUNK = 64
ROPE_THETA = 10000.0
ML_HEADS, ML_DK, ML_DV = 4, 128, 128
DF_HEADS, DF_HD, DF_DV = 4, 64, 128
GL_HEADS, GL_DK, GL_DV, GL_RANK, GL_TAU = 4, 64, 128, 16, 16.0
N_BRANCH, BRANCH_W = 3, 512
N_GROUPS, EXPERTS_PER_GROUP, TOP_K, D_EXPERT = 4, 8, 2, 512
N_EXPERTS = N_GROUPS * EXPERTS_PER_GROUP
IN_SIZES = (
    ML_HEADS * ML_DK, ML_HEADS * ML_DK, ML_HEADS * ML_DV, ML_HEADS * ML_DV, 4 * ML_HEADS,
    DF_HEADS * 2 * DF_HD, DF_HEADS * 2 * DF_HD, DF_HEADS * DF_DV,
    GL_HEADS * GL_DK, GL_HEADS * GL_DK, GL_HEADS * GL_DV, GL_HEADS * GL_DV, 2 * GL_RANK,
    N_BRANCH * D_MODEL,
)

LANE = 128
SUBLANE = 8
VMEM_LIMIT = 56 * 1024 * 1024
MOE_TILE = 512

A_MQ, A_MK, A_MV, A_GQ, A_GK, A_GV, A_DV = 0, 4, 8, 12, 14, 16, 20
A_WIDTH = 24 * LANE
F_DQ, F_DK, F_MO, F_GG, F_GT, F_MG, F_GA = 0, 4, 8, 12, 16, 40, 41
F_WIDTH = 42 * LANE
HEAD_W = 4 * LANE


def _cparams(sem):
    return pltpu.CompilerParams(dimension_semantics=sem, vmem_limit_bytes=VMEM_LIMIT)


def _const_spec(shape):
    nd = len(shape)
    return pl.BlockSpec(shape, lambda *_: (0,) * nd, pipeline_mode=pl.Buffered(1))


def _logsig(x):
    return jnp.minimum(x, 0.0) - jnp.log1p(jnp.exp(-jnp.abs(x)))


def _dot(a, b):
    return jnp.dot(a, b, preferred_element_type=F32)


def _dot_nt(a, b):
    return lax.dot_general(a, b, (((1,), (1,)), ((), ())), preferred_element_type=F32)


def _dot_tn(a, b):
    return lax.dot_general(a, b, (((0,), (0,)), ((), ())), preferred_element_type=F32)


def _rms(x, g):
    return x * lax.rsqrt(jnp.mean(x * x, axis=-1, keepdims=True) + NORM_EPS) * g


def _mod_kernel(c_ref, w_ref, b_ref, o_ref):
    c = c_ref[...]
    s = c * jax.nn.sigmoid(c)
    o_ref[...] = jnp.dot(s, w_ref[...], precision=HIGHEST, preferred_element_type=F32) + b_ref[...]


def _modulation(cc, w_mod, b_mod):
    rows, d = cc.shape
    n = w_mod.shape[1]
    tn = 512
    return pl.pallas_call(
        _mod_kernel,
        out_shape=jax.ShapeDtypeStruct((rows, n), F32),
        grid=(n // tn,),
        in_specs=[pl.BlockSpec((rows, d), lambda j: (0, 0)),
                  pl.BlockSpec((d, tn), lambda j: (0, j)),
                  pl.BlockSpec((1, tn), lambda j: (0, j))],
        out_specs=pl.BlockSpec((rows, tn), lambda j: (0, j)),
        compiler_params=_cparams(("parallel",)),
        name="adaln_mod",
    )(cc, w_mod, b_mod.reshape(1, n))


def _inproj_kernel(x_ref, mod_ref, g_ref, wa_ref, wf_ref, oa_ref, of_ref):
    y = _rms(x_ref[...], g_ref[...])
    h = (y * (1.0 + mod_ref[1:2, :]) + mod_ref[0:1, :]).astype(BF16)
    oa_ref[...] = _dot(h, wa_ref[...]).astype(BF16)
    of_ref[...] = _dot(h, wf_ref[...])


def _inproj(xs, modtab, g, wa, wf, tm, nct):
    b, s, d = xs.shape
    kind = lambda i: jnp.where(i >= nct, 1, 0)
    return pl.pallas_call(
        _inproj_kernel,
        out_shape=(jax.ShapeDtypeStruct((b, s, A_WIDTH), BF16), jax.ShapeDtypeStruct((b, s, F_WIDTH), F32)),
        grid=(b, s // tm),
        in_specs=[pl.BlockSpec((None, tm, d), lambda bi, i: (bi, i, 0)),
                  pl.BlockSpec((None, None, SUBLANE, d), lambda bi, i: (bi, kind(i), 0, 0)),
                  _const_spec((1, d)), _const_spec((d, A_WIDTH)), _const_spec((d, F_WIDTH))],
        out_specs=(pl.BlockSpec((None, tm, A_WIDTH), lambda bi, i: (bi, i, 0)),
                   pl.BlockSpec((None, tm, F_WIDTH), lambda bi, i: (bi, i, 0))),
        compiler_params=_cparams(("parallel", "parallel")),
        name="in_proj",
    )(xs, modtab, g.reshape(1, d), wa, wf)


def _chunk_ids(t, nct, nch):
    return t, jnp.where(t < nct, nct - 1 - t, nch - 1 - (t - nct))


def _head_cols(h):
    return slice(h * LANE, (h + 1) * LANE)


def _mlstm_kernel(q_ref, kt_ref, v_ref, o_ref, gcol_ref, grow_ref, bcol_ref, brow_ref, ng_ref, out_ref,
                  hf_ref, hb_ref, ct_ref, *, nct, nch):
    L = CHUNK
    H = ML_HEADS
    jj = lax.broadcasted_iota(jnp.int32, (L, L), 0)
    ii = lax.broadcasted_iota(jnp.int32, (L, L), 1)
    lane = lax.broadcasted_iota(jnp.int32, (L, LANE), 1)
    ones_col = jnp.where(lane == 0, 1.0, 0.0).astype(BF16)
    scale = ML_DK ** -0.5
    vis = ((ii <= jj), (ii >= jj))
    cum = tuple(jnp.where(m_, 1.0, 0.0).astype(BF16) for m_ in vis)
    cum_t = tuple(jnp.where(m_, 1.0, 0.0).astype(BF16) for m_ in ((jj <= ii), (jj >= ii)))
    ct_ref[...] = jnp.zeros(ct_ref.shape, F32)

    def split2(x):
        hi = x.astype(BF16)
        return hi, (x - hi.astype(F32)).astype(BF16)

    def step(t, carry):
        chunks = _chunk_ids(t, nct, nch)
        rows = [pl.ds(pl.multiple_of(c * L, L), L) for c in chunks]
        chains = [(d, h) for d in range(2) for h in range(H)]
        q, kt, vext, ct, sq, qc = {}, {}, {}, {}, {}, {}
        for d, h in chains:
            q[d, h] = q_ref[rows[d], _head_cols(h)]
            kt[d, h] = kt_ref[chunks[d], h]
            vext[d, h] = jnp.concatenate([v_ref[rows[d], _head_cols(h)], ones_col], axis=1)
            ct[d, h] = ct_ref[d, h]
            sq[d, h] = _dot(q[d, h], kt[d, h])
            qc[d, h] = _dot(q[d, h], ct[d, h].astype(BF16))
        gr, bc_all, br_all = [], [], []
        for d in range(2):
            gc = gcol_ref[rows[d], :] + bcol_ref[...]
            gr.append(grow_ref[chunks[d]] + brow_ref[...])
            chi, clo = split2(_logsig(gc))
            rhi, rlo = split2(_logsig(gr[d]))
            bc_all.append(_dot(cum[d], chi) + _dot(cum[d], clo))
            br_all.append(_dot(rhi, cum_t[d]) + _dot(rlo, cum_t[d]))
        w, w_inter, einv, decay, kw, new = {}, {}, {}, {}, {}, []
        for d, h in chains:
            m = carry[d * H + h]
            gi = 2 * d * H + h
            b_row = br_all[d][gi + H:gi + H + 1, :]
            b_col = bc_all[d][:, gi + H:gi + H + 1]
            b_last = b_row[:, L - 1:L] if d == 0 else b_row[:, 0:1]
            u_row = gr[d][gi:gi + 1, :] - b_row
            g_col = jnp.maximum(jnp.max(jnp.where(vis[d], u_row, -jnp.inf), axis=1, keepdims=True), m)
            w[d, h] = jnp.exp(jnp.where(vis[d], u_row - g_col, -jnp.inf)) * scale
            w_inter[d, h] = jnp.exp(m - g_col)
            einv[d, h] = jnp.exp(-(b_col + g_col))
            m_new = b_last + jnp.maximum(m, jnp.max(u_row, axis=1, keepdims=True))
            wk_row = jnp.exp(b_last + u_row - m_new) * scale
            decay[d, h] = jnp.exp(b_last + m - m_new)
            kw[d, h] = (kt[d, h].astype(F32) * wk_row).astype(BF16)
            new.append(m_new)
        for d, h in chains:
            ct_ref[d, h] = decay[d, h] * ct[d, h] + _dot(kw[d, h], vext[d, h])
        sv = {}
        for d, h in chains:
            sv[d, h] = _dot((sq[d, h] * w[d, h]).astype(BF16), vext[d, h])
        for d, h in chains:
            num = sv[d, h][:, :ML_DV] + w_inter[d, h] * qc[d, h][:, :ML_DV]
            den = sv[d, h][:, ML_DV:ML_DV + 1] + w_inter[d, h] * qc[d, h][:, ML_DV:ML_DV + 1]
            hh = num / jnp.maximum(jnp.abs(den), einv[d, h])
            if d == 0:
                hf_ref[rows[d], _head_cols(h)] = hh
            else:
                hb_ref[rows[d], _head_cols(h)] = hh
        return tuple(new)

    zero = jnp.zeros((1, 1), F32)
    lax.fori_loop(0, nch, step, (zero,) * (2 * H))
    for h in range(H):
        y = _rms(hf_ref[:, _head_cols(h)] + hb_ref[:, _head_cols(h)], ng_ref[...])
        out_ref[:, _head_cols(h)] = (y * jax.nn.sigmoid(o_ref[:, _head_cols(h)])).astype(BF16)


def _chunk_transposed(oa, base, nch):
    b = oa.shape[0]
    t = oa[:, :, base * LANE:base * LANE + HEAD_W].reshape(b, nch, CHUNK, 4, LANE)
    return t.transpose(0, 1, 3, 4, 2)


def _mlstm(oa, of, grow, gate_b, norm_g, nct, nch):
    b, s, _ = oa.shape
    bcol = jnp.zeros((1, LANE), F32).at[0, :4 * ML_HEADS].set(gate_b)
    brow = gate_b.reshape(4 * ML_HEADS, 1)
    sect = lambda base: (lambda bi: (bi, 0, base // 4))
    kt = _chunk_transposed(oa, A_MK, nch)
    return pl.pallas_call(
        functools.partial(_mlstm_kernel, nct=nct, nch=nch),
        out_shape=jax.ShapeDtypeStruct((b, s, HEAD_W), BF16),
        grid=(b,),
        in_specs=[pl.BlockSpec((None, s, HEAD_W), sect(A_MQ)),
                  pl.BlockSpec((None, nch, ML_HEADS, ML_DK, CHUNK), lambda bi: (bi, 0, 0, 0, 0)),
                  pl.BlockSpec((None, s, HEAD_W), sect(A_MV)),
                  pl.BlockSpec((None, s, HEAD_W), sect(F_MO)),
                  pl.BlockSpec((None, s, LANE), lambda bi: (bi, 0, F_MG)),
                  pl.BlockSpec((None, nch, 4 * ML_HEADS, CHUNK), lambda bi: (bi, 0, 0, 0)),
                  pl.BlockSpec((1, LANE), lambda bi: (0, 0)),
                  pl.BlockSpec((4 * ML_HEADS, 1), lambda bi: (0, 0)),
                  pl.BlockSpec((1, ML_DV), lambda bi: (0, 0))],
        out_specs=pl.BlockSpec((None, s, HEAD_W), lambda bi: (bi, 0, 0)),
        scratch_shapes=[pltpu.VMEM((s, HEAD_W), F32), pltpu.VMEM((s, HEAD_W), F32),
                        pltpu.VMEM((2, ML_HEADS, ML_DK, 2 * LANE), F32)],
        compiler_params=_cparams(("parallel",)),
        name="mlstm",
    )(oa, kt, oa, of, of, grow, bcol, brow, norm_g.reshape(1, ML_DV))


def _gla_tables():
    L = CHUNK
    G = np.zeros((2, 7 * L, L), np.float32)
    lvl = np.full((2, L, L), 7, np.int32)
    for d in range(2):
        for p in range(L):
            if d == 0:
                G[d, p, :p + 1] = 1
            else:
                G[d, p, p:] = 1
        for li, s in enumerate((32, 16, 8, 4, 2, 1)):
            for p in range(L):
                base = (p // (2 * s)) * 2 * s
                row = (li + 1) * L + p
                if d == 0:
                    mid = base + s
                    if p >= mid:
                        G[d, row, mid + 1:p + 1] = 1
                    else:
                        G[d, row, p + 1:mid + 1] = 1
                else:
                    mid = base + s - 1
                    if p <= mid:
                        G[d, row, p:mid] = 1
                    else:
                        G[d, row, mid:p] = 1
            blk = np.arange(L) // (2 * s)
            upper = (np.arange(L) % (2 * s)) >= s
            same = blk[:, None] == blk[None, :]
            if d == 0:
                sel = same & upper[:, None] & ~upper[None, :]
            else:
                sel = same & ~upper[:, None] & upper[None, :]
            lvl[d][sel] = li
        lvl[d][np.arange(L), np.arange(L)] = 6
    return G, lvl


GL_PAIRS = GL_HEADS * GL_DK // LANE


def _gla_kernel(q_ref, k_ref, v_ref, vt_ref, g_ref, a_ref, wa_ref, ba_ref, gm_ref, lvl_ref, ng_ref, out_ref,
                acc_ref, la_ref, st_ref, *, nct, nch):
    L = CHUNK
    for d in range(2):
        pre = jnp.dot(a_ref[...], wa_ref[d], precision=HIGHEST, preferred_element_type=F32) + ba_ref[d]
        la_ref[d] = _logsig(pre) * (1.0 / GL_TAU)
    acc_ref[...] = jnp.zeros(acc_ref.shape, F32)
    st_ref[...] = jnp.zeros(st_ref.shape, F32)
    first = lax.broadcasted_iota(jnp.int32, (L, LANE), 1) < GL_DK
    first2 = lax.broadcasted_iota(jnp.int32, (GL_DV, LANE), 1) < GL_DK

    def split(x):
        return jnp.concatenate([jnp.where(first, x, 0.0), jnp.where(first, 0.0, x)], axis=0).astype(BF16)

    def step(t, carry):
        chunks = _chunk_ids(t, nct, nch)
        rows = [pl.ds(pl.multiple_of(c * L, L), L) for c in chunks]
        chains = [(d, p) for d in range(2) for p in range(GL_PAIRS)]
        xs = []
        for d in range(2):
            lac = la_ref[d, rows[d], :]
            hi = lac.astype(BF16)
            lo = (lac - hi.astype(F32)).astype(BF16)
            xs.append(_dot(gm_ref[d], hi) + _dot(gm_ref[d], lo))
        q, k, cs, tot, st, inter, upd = {}, {}, {}, {}, {}, {}, {}
        for d, p in chains:
            q[d, p] = q_ref[rows[d], _head_cols(p)].astype(F32) * (GL_DK ** -0.5)
            k[d, p] = k_ref[rows[d], _head_cols(p)].astype(F32)
            cs[d, p] = xs[d][0:L, _head_cols(p)]
            tot[d, p] = cs[d, p][L - 1:L] if d == 0 else cs[d, p][0:1]
            st[d, p] = st_ref[d, p]
        for d, p in chains:
            inter[d, p] = _dot_nt(split(q[d, p] * jnp.exp(cs[d, p])), st[d, p].astype(BF16))
            ke = (k[d, p] * jnp.exp(tot[d, p] - cs[d, p])).astype(BF16)
            u = [_dot(vt_ref[chunks[d], 2 * p + hh], ke) for hh in range(2)]
            st_ref[d, p] = st[d, p] * jnp.exp(tot[d, p]) + jnp.where(first2, u[0], u[1])
        amat = {}
        for d, p in chains:
            amat[d, p] = jnp.where(lvl_ref[d] == 6, _dot_nt(split(q[d, p]), k[d, p].astype(BF16)), 0.0)
        for li in range(6):
            for d, p in chains:
                e = jnp.exp(xs[d][(li + 1) * L:(li + 2) * L, _head_cols(p)])
                lev = _dot_nt(split(q[d, p] * e), (k[d, p] * e).astype(BF16))
                amat[d, p] = jnp.where(lvl_ref[d] == li, lev, amat[d, p])
        for d, p in chains:
            a = amat[d, p].astype(BF16)
            for hh in range(2):
                cols = _head_cols(2 * p + hh)
                acc_ref[rows[d], cols] += (_dot(a[hh * L:(hh + 1) * L], v_ref[rows[d], cols])
                                           + inter[d, p][hh * L:(hh + 1) * L])
        return carry

    lax.fori_loop(0, nch, step, 0)
    for h in range(GL_HEADS):
        cols = _head_cols(h)
        g = g_ref[:, cols]
        out_ref[:, cols] = (_rms(acc_ref[:, cols], ng_ref[...]) * (g * jax.nn.sigmoid(g))).astype(BF16)


def _gla(oa, of, w_alpha, b_alpha, norm_g, nct, nch):
    b, s, _ = oa.shape
    gmat, lvl = _gla_tables()
    lvl = np.concatenate([lvl, lvl], axis=1)
    qk_w = GL_HEADS * GL_DK
    wa = jnp.zeros((2, LANE, qk_w), F32)
    for d in range(2):
        wa = wa.at[d, d * GL_RANK:(d + 1) * GL_RANK, :].set(w_alpha[d])
    ba = b_alpha.reshape(2, 1, qk_w)
    vt = _chunk_transposed(oa, A_GV, nch)
    full = lambda nd: (lambda bi: (0,) * nd)
    return pl.pallas_call(
        functools.partial(_gla_kernel, nct=nct, nch=nch),
        out_shape=jax.ShapeDtypeStruct((b, s, HEAD_W), BF16),
        grid=(b,),
        in_specs=[pl.BlockSpec((None, s, qk_w), lambda bi: (bi, 0, A_GQ * LANE // qk_w)),
                  pl.BlockSpec((None, s, qk_w), lambda bi: (bi, 0, A_GK * LANE // qk_w)),
                  pl.BlockSpec((None, s, HEAD_W), lambda bi: (bi, 0, A_GV // 4)),
                  pl.BlockSpec((None, nch, GL_HEADS, GL_DV, CHUNK), lambda bi: (bi, 0, 0, 0, 0)),
                  pl.BlockSpec((None, s, HEAD_W), lambda bi: (bi, 0, F_GG // 4)),
                  pl.BlockSpec((None, s, LANE), lambda bi: (bi, 0, F_GA)),
                  pl.BlockSpec((2, LANE, qk_w), full(3)),
                  pl.BlockSpec((2, 1, qk_w), full(3)),
                  pl.BlockSpec((2, 7 * CHUNK, CHUNK), full(3)),
                  pl.BlockSpec((2, 2 * CHUNK, CHUNK), full(3)),
                  pl.BlockSpec((1, GL_DV), full(2))],
        out_specs=pl.BlockSpec((None, s, HEAD_W), lambda bi: (bi, 0, 0)),
        scratch_shapes=[pltpu.VMEM((s, HEAD_W), F32), pltpu.VMEM((2, s, qk_w), F32),
                        pltpu.VMEM((2, GL_PAIRS, GL_DV, LANE), F32)],
        compiler_params=_cparams(("parallel",)),
        name="gla",
    )(oa, oa, oa, vt, of, of, wa, ba, jnp.asarray(gmat, BF16), jnp.asarray(lvl), norm_g.reshape(1, GL_DV))


def _rope_tables(n):
    t = np.arange(n)
    n_freq = DF_HD // 4
    inv = jnp.asarray(ROPE_THETA, F32) ** (-jnp.arange(n_freq, dtype=F32) / n_freq)
    ang_r = jnp.asarray(t // GRID_W, F32)[:, None] * inv
    ang_c = jnp.asarray(t % GRID_W, F32)[:, None] * inv
    ang = jnp.concatenate([ang_r, ang_r, ang_c, ang_c] * 2, axis=1)
    first = (np.arange(LANE) % 32) < 16
    cos, sin = jnp.cos(ang), jnp.sin(ang)
    return cos, jnp.where(first, -sin, 0.0), jnp.where(first, 0.0, sin)


DF_ROW_GROUPS = 4


def _rope(x, cos, sa, sb):
    return x * cos + pltpu.roll(x, LANE - 16, 1) * sa + pltpu.roll(x, 16, 1) * sb


def _diff_kernel(q_ref, k_ref, v_ref, cq_ref, saq_ref, sbq_ref, ck_ref, sak_ref, sbk_ref, lam_ref, ng_ref,
                 out_ref, kr_ref, *, nct, n_ctx, lam_init):
    s = k_ref.shape[0]
    tq = q_ref.shape[0]
    i = pl.program_id(2)

    @pl.when(i == 0)
    def _():
        kr_ref[0:n_ctx, :] = k_ref[0:n_ctx, :].astype(BF16)
        kr_ref[n_ctx:s, :] = _rope(k_ref[n_ctx:s, :], ck_ref[...], sak_ref[...], sbk_ref[...]).astype(BF16)

    lp = lam_ref[...]
    lam = (jnp.exp(jnp.sum(lp[0:1] * lp[1:2], axis=1, keepdims=True))
           - jnp.exp(jnp.sum(lp[2:3] * lp[3:4], axis=1, keepdims=True)) + lam_init)
    lane = lax.broadcasted_iota(jnp.int32, q_ref.shape, 1)

    def attend(qb, nk):
        qb = qb * (DF_HD ** -0.5)
        qs = jnp.concatenate([jnp.where(lane < DF_HD, qb, 0.0), jnp.where(lane >= DF_HD, qb, 0.0)],
                             axis=0).astype(BF16)
        rg = 2 * tq // DF_ROW_GROUPS
        scs = [_dot_nt(qs[g * rg:(g + 1) * rg], kr_ref[0:nk, :]) for g in range(DF_ROW_GROUPS)]
        ovs = []
        for sc in scs:
            p = jnp.exp(sc - jnp.max(sc, axis=1, keepdims=True))
            rl = 1.0 / jnp.sum(p, axis=1, keepdims=True)
            ovs.append((p.astype(BF16), rl))
        ov = jnp.concatenate([_dot(p, v_ref[0:nk, :]) * rl for p, rl in ovs], axis=0)
        o = ov[0:tq] - lam * ov[tq:2 * tq]
        out_ref[...] = (_rms(o, ng_ref[...]) * (1.0 - lam_init)).astype(BF16)

    @pl.when(i < nct)
    def _():
        attend(q_ref[...], n_ctx)

    @pl.when(i >= nct)
    def _():
        attend(_rope(q_ref[...], cq_ref[...], saq_ref[...], sbq_ref[...]), s)


def _diff_attn(oa, of, rope, df_lambda, norm_g, tq, nct, n_ctx, lam_init):
    b, s, _ = oa.shape
    n = s - n_ctx
    cos, sa, sb = rope
    lam_p = jnp.zeros((4, LANE), F32).at[:, :DF_HD].set(df_lambda)
    qblk = lambda bi, h, i: (jnp.maximum(i - nct, 0), 0)
    full = lambda bi, h, i: (0, 0)
    return pl.pallas_call(
        functools.partial(_diff_kernel, nct=nct, n_ctx=n_ctx, lam_init=lam_init),
        out_shape=jax.ShapeDtypeStruct((b, s, DF_HEADS * DF_DV), BF16),
        grid=(b, DF_HEADS, s // tq),
        in_specs=[pl.BlockSpec((None, tq, LANE), lambda bi, h, i: (bi, i, F_DQ + h)),
                  pl.BlockSpec((None, s, LANE), lambda bi, h, i: (bi, 0, F_DK + h)),
                  pl.BlockSpec((None, s, LANE), lambda bi, h, i: (bi, 0, A_DV + h)),
                  pl.BlockSpec((tq, LANE), qblk), pl.BlockSpec((tq, LANE), qblk), pl.BlockSpec((tq, LANE), qblk),
                  pl.BlockSpec((n, LANE), full), pl.BlockSpec((n, LANE), full), pl.BlockSpec((n, LANE), full),
                  pl.BlockSpec((4, LANE), full), pl.BlockSpec((1, DF_DV), full)],
        out_specs=pl.BlockSpec((None, tq, LANE), lambda bi, h, i: (bi, i, h)),
        scratch_shapes=[pltpu.VMEM((s, LANE), BF16)],
        compiler_params=_cparams(("parallel", "parallel", "arbitrary")),
        name="diff_attn",
    )(of, of, oa, cos, sa, sb, cos, sa, sb, lam_p, norm_g.reshape(1, DF_DV))


def _merge_kernel(ml_ref, df_ref, gl_ref, g0_ref, g1_ref, g2_ref, wb_ref, wo_ref, x_ref, mod_ref, nf_ref,
                  wr_ref, br_ref, tri_ref, xo_ref, h2_ref, rid_ref, rw_ref, cnt_ref):
    y = (jax.nn.sigmoid(g0_ref[...]) * _dot(ml_ref[...], wb_ref[0])
         + jax.nn.sigmoid(g1_ref[...]) * _dot(df_ref[...], wb_ref[1])
         + jax.nn.sigmoid(g2_ref[...]) * _dot(gl_ref[...], wb_ref[2]))
    xn = x_ref[...] + mod_ref[2:3, :] * _dot(y.astype(BF16), wo_ref[...])
    xo_ref[...] = xn
    h2 = _rms(xn, nf_ref[...]) * (1.0 + mod_ref[4:5, :]) + mod_ref[3:4, :]
    h2_hi = h2.astype(BF16)
    h2_ref[...] = h2_hi
    h2_lo = (h2 - h2_hi.astype(F32)).astype(BF16)
    logits = (_dot(h2_hi, wr_ref[0]) + (_dot(h2_lo, wr_ref[0]) + _dot(h2_hi, wr_ref[1]))) + br_ref[...]
    lane = lax.broadcasted_iota(jnp.int32, logits.shape, 1)
    lane_f = lane.astype(F32)
    neg = -jnp.inf

    def first_max(vals):
        mx = jnp.max(vals, axis=1, keepdims=True)
        return mx, jnp.min(jnp.where(vals == mx, lane_f, float(LANE)), axis=1, keepdims=True)

    is_grp = lane < N_GROUPS
    gmax, gidx = first_max(jnp.where(is_grp, logits, neg))
    pg_top = 1.0 / jnp.sum(jnp.where(is_grp, jnp.exp(logits - gmax), 0.0), axis=1, keepdims=True)
    lo = N_GROUPS + gidx * EXPERTS_PER_GROUP
    in_grp = (lane_f >= lo) & (lane_f < lo + EXPERTS_PER_GROUP)
    le = jnp.where(in_grp, logits, neg)
    m1, e1 = first_max(le)
    m2, e2 = first_max(jnp.where(lane_f == e1, neg, le))
    r = jnp.exp(m2 - m1)
    w1 = pg_top / (1.0 + r)
    w2 = pg_top * r / (1.0 + r)

    @pl.when((pl.program_id(0) == 0) & (pl.program_id(1) == 0))
    def _():
        cnt_ref[...] = jnp.zeros(cnt_ref.shape, F32)

    hot1 = jnp.where(lane_f == e1, 1.0, 0.0)
    hot2 = jnp.where(lane_f == e2, 1.0, 0.0)
    tot1 = jnp.sum(hot1, axis=0, keepdims=True)
    tot2 = jnp.sum(hot2, axis=0, keepdims=True)
    cnt = cnt_ref[...]
    before1 = cnt + _dot(tri_ref[...], hot1.astype(BF16))
    before2 = cnt + tot1 + _dot(tri_ref[...], hot2.astype(BF16))
    rank1 = jnp.sum(hot1 * before1, axis=1, keepdims=True)
    rank2 = jnp.sum(hot2 * before2, axis=1, keepdims=True)
    cnt_ref[...] = cnt + tot1 + tot2
    rid = jnp.where(lane == 0, e1 - N_GROUPS, jnp.where(lane == 1, e2 - N_GROUPS,
                    jnp.where(lane == 2, rank1, jnp.where(lane == 3, rank2, 0.0))))
    rid_ref[...] = rid.astype(jnp.int32)
    rw_ref[...] = jnp.where(lane == 0, w1, jnp.where(lane == 1, w2, 0.0))


def _merge(ml, df, gl, of, wb, wo, xs, modtab, nf, wr, br, tm, nct, first_block):
    b, s, d = xs.shape
    nb = s // tm - first_block
    so = nb * tm
    kind = lambda i: jnp.where(i + first_block >= nct, 1, 0)
    row = lambda bi, i: (bi, i + first_block, 0)
    gate = lambda br_: (lambda bi, i: (bi, i + first_block, F_GT // (d // LANE) + br_))
    outrow = lambda bi, i: (bi, i, 0)
    tri = jnp.asarray(np.tril(np.ones((tm, tm), np.float32), -1), BF16)
    return pl.pallas_call(
        _merge_kernel,
        out_shape=(jax.ShapeDtypeStruct((b, so, d), F32), jax.ShapeDtypeStruct((b, so, d), BF16),
                   jax.ShapeDtypeStruct((b, so, LANE), jnp.int32), jax.ShapeDtypeStruct((b, so, LANE), F32),
                   jax.ShapeDtypeStruct((1, LANE), F32)),
        grid=(b, nb),
        in_specs=[pl.BlockSpec((None, tm, BRANCH_W), row), pl.BlockSpec((None, tm, BRANCH_W), row),
                  pl.BlockSpec((None, tm, BRANCH_W), row),
                  pl.BlockSpec((None, tm, d), gate(0)), pl.BlockSpec((None, tm, d), gate(1)),
                  pl.BlockSpec((None, tm, d), gate(2)),
                  _const_spec((N_BRANCH, BRANCH_W, d)), _const_spec((d, d)),
                  pl.BlockSpec((None, tm, d), row),
                  pl.BlockSpec((None, None, SUBLANE, d), lambda bi, i: (bi, kind(i), 0, 0)),
                  _const_spec((1, d)), _const_spec((2, d, LANE)), _const_spec((1, LANE)), _const_spec((tm, tm))],
        out_specs=(pl.BlockSpec((None, tm, d), outrow), pl.BlockSpec((None, tm, d), outrow),
                   pl.BlockSpec((None, tm, LANE), outrow), pl.BlockSpec((None, tm, LANE), outrow),
                   pl.BlockSpec((1, LANE), lambda bi, i: (0, 0))),
        compiler_params=_cparams(("arbitrary", "arbitrary")),
        name="merge_route",
    )(ml, df, gl, of, of, of, wb, wo, xs, modtab, nf.reshape(1, d), wr, br, tri)


def _gmm_kernel(te_ref, tv_ref, x_ref, wg_ref, wu_ref, wd_ref, y_ref, wgb_ref, wub_ref, wdb_ref):
    i = pl.program_id(0)

    @pl.when((i == 0) | (te_ref[i] != te_ref[jnp.maximum(i - 1, 0)]))
    def _():
        wgb_ref[...] = wg_ref[...].astype(BF16)
        wub_ref[...] = wu_ref[...].astype(BF16)
        wdb_ref[...] = wd_ref[...].astype(BF16)

    @pl.when(tv_ref[i] > 0)
    def _():
        x = x_ref[...]
        a = _dot(x, wgb_ref[...])
        hid = (a * jax.nn.sigmoid(a)) * _dot(x, wub_ref[...])
        y_ref[...] = _dot(hid.astype(BF16), wdb_ref[...]).astype(y_ref.dtype)

    @pl.when(tv_ref[i] == 0)
    def _():
        y_ref[...] = jnp.zeros(y_ref.shape, y_ref.dtype)


def _gmm(tile_expert, tile_valid, xs, wg, wu, wd):
    npad, d = xs.shape
    tm = MOE_TILE
    de = wg.shape[2]
    return pl.pallas_call(
        _gmm_kernel,
        out_shape=jax.ShapeDtypeStruct((npad, d), BF16),
        grid_spec=pltpu.PrefetchScalarGridSpec(
            num_scalar_prefetch=2,
            grid=(npad // tm,),
            in_specs=[pl.BlockSpec((tm, d), lambda i, te, tv: (i, 0)),
                      pl.BlockSpec((None, d, de), lambda i, te, tv: (te[i], 0, 0)),
                      pl.BlockSpec((None, d, de), lambda i, te, tv: (te[i], 0, 0)),
                      pl.BlockSpec((None, de, d), lambda i, te, tv: (te[i], 0, 0))],
            out_specs=pl.BlockSpec((tm, d), lambda i, te, tv: (i, 0)),
            scratch_shapes=[pltpu.VMEM((d, de), BF16), pltpu.VMEM((d, de), BF16), pltpu.VMEM((de, d), BF16)]),
        compiler_params=_cparams(("arbitrary",)),
        name="moe_gmm",
    )(tile_expert, tile_valid, xs, wg, wu, wd)


def _route_plan(rid, cnt):
    t = rid.shape[0]
    tm = MOE_TILE
    n_tiles = (TOP_K * t + tm - 1) // tm + N_EXPERTS
    counts = cnt[0, N_GROUPS:N_GROUPS + N_EXPERTS].astype(jnp.int32)
    tiles_e = (counts + tm - 1) // tm
    tile_end = jnp.cumsum(tiles_e)
    pad_start = (tile_end - tiles_e) * tm
    eid, rank = rid[:, 0:TOP_K], rid[:, TOP_K:2 * TOP_K]
    pos = rank + jnp.sum(jnp.where(eid[:, :, None] == jnp.arange(N_EXPERTS), pad_start, 0), axis=-1)
    tile = jnp.arange(n_tiles, dtype=jnp.int32)
    tile_expert = jnp.minimum(jnp.sum(tile[:, None] >= tile_end[None, :], axis=1), N_EXPERTS - 1).astype(jnp.int32)
    tile_valid = (tile < tile_end[-1]).astype(jnp.int32)
    return tile_expert, tile_valid, pos.astype(jnp.int32), n_tiles * tm


def _combine_kernel(x_ref, y0_ref, y1_ref, rw_ref, mod_ref, fg_ref, o_ref, *, final):
    rw = rw_ref[...]
    y = rw[:, 0:1] * y0_ref[...].astype(F32) + rw[:, 1:2] * y1_ref[...].astype(F32)
    xn = x_ref[...] + mod_ref[5:6, :] * y
    o_ref[...] = _rms(xn, fg_ref[...]) if final else xn


def _combine(xs, y0, y1, rw, modtab, fg, tm, nct, first_block, final):
    b, s, d = xs.shape
    kind = lambda i: jnp.where(i + first_block >= nct, 1, 0)
    row = lambda bi, i: (bi, i, 0)
    return pl.pallas_call(
        functools.partial(_combine_kernel, final=final),
        out_shape=jax.ShapeDtypeStruct((b, s, d), F32),
        grid=(b, s // tm),
        in_specs=[pl.BlockSpec((None, tm, d), row), pl.BlockSpec((None, tm, d), row),
                  pl.BlockSpec((None, tm, d), row), pl.BlockSpec((None, tm, LANE), row),
                  pl.BlockSpec((None, None, SUBLANE, d), lambda bi, i: (bi, kind(i), 0, 0)),
                  _const_spec((1, d))],
        out_specs=pl.BlockSpec((None, tm, d), row),
        compiler_params=_cparams(("parallel", "parallel")),
        name="moe_combine",
    )(xs, y0, y1, rw, modtab, fg.reshape(1, d))


def _pack_w_in(w):
    d = w.shape[0]
    mq, mk, mv, mo, mg, dq, dk, dv, gq, gk, gv, gg, ga, gt = jnp.split(
        w, [int(i) for i in np.cumsum(IN_SIZES)[:-1]], axis=1)

    def pad_cols(t):
        return jnp.pad(t, ((0, 0), (0, LANE - t.shape[1])))

    wa = jnp.concatenate([mq, mk, mv, gq, gk, gv, dv], axis=1).astype(BF16)
    wf = jnp.concatenate([dq, dk, mo, gg, gt, pad_cols(mg), pad_cols(ga)], axis=1).astype(BF16)
    return wa, wf


def _dispatch_rows(h2, pos, npad):
    t = h2.shape[0]
    tok = jnp.arange(t, dtype=jnp.int32)
    src = jnp.zeros((npad,), jnp.int32).at[pos[:, 0]].set(tok).at[pos[:, 1]].set(tok)
    return jnp.take(h2, src, axis=0)


def kernel(x, c, ctx, c_ctx, w_mod, b_mod, norm_mix_g, norm_ffn_g, w_in, ml_gate_b, ml_norm_g, df_lambda,
           df_norm_g, gl_w_alpha, gl_b_alpha, gl_norm_g, w_branch, w_out, router_group_w, router_group_b,
           router_expert_w, router_expert_b, moe_w_gate, moe_w_up, moe_w_down, final_norm_g):
    b, n, d = x.shape
    n_ctx = ctx.shape[1]
    s = n_ctx + n
    depth = w_mod.shape[0]
    tm = 256 if n_ctx % 256 == 0 else 128
    assert n_ctx % tm == 0 and n % tm == 0 and n % GRID_W == 0 and n_ctx % CHUNK == 0 and d == D_MODEL
    nct = n_ctx // tm
    nch, nch_ctx = s // CHUNK, n_ctx // CHUNK
    rope = _rope_tables(n)
    xs = jnp.concatenate([ctx, x], axis=1)
    mod_rows = -(-(b + 1) // SUBLANE) * SUBLANE
    cc = jnp.zeros((mod_rows, d), F32).at[:b].set(c).at[b].set(c_ctx)
    for li in range(depth):
        last = li == depth - 1
        lam_init = 0.8 - 0.6 * math.exp(-0.3 * li)
        mod = _modulation(cc, w_mod[li], b_mod[li]).reshape(mod_rows, 6, d)
        mod = jnp.pad(mod, ((0, 0), (0, SUBLANE - 6), (0, 0)))
        modtab = jnp.stack([jnp.broadcast_to(mod[b], (b, SUBLANE, d)), mod[:b]], axis=1)
        wa, wf = _pack_w_in(w_in[li])
        oa, of = _inproj(xs, modtab, norm_mix_g[li], wa, wf, tm, nct)
        grow = of[:, :, F_MG * LANE:F_MG * LANE + 4 * ML_HEADS]
        grow = grow.reshape(b, nch, CHUNK, 4 * ML_HEADS).transpose(0, 1, 3, 2)
        ml = _mlstm(oa, of, grow, ml_gate_b[li], ml_norm_g[li], nch_ctx, nch)
        df = _diff_attn(oa, of, rope, df_lambda[li], df_norm_g[li], tm, nct, n_ctx, lam_init)
        gl = _gla(oa, of, gl_w_alpha[li], gl_b_alpha[li], gl_norm_g[li], nch_ctx, nch)
        wr = jnp.zeros((d, LANE), F32).at[:, :N_GROUPS].set(router_group_w[li])
        wr = wr.at[:, N_GROUPS:N_GROUPS + N_EXPERTS].set(router_expert_w[li])
        wr_hi = wr.astype(BF16)
        wr = jnp.stack([wr_hi, (wr - wr_hi.astype(F32)).astype(BF16)])
        br = jnp.zeros((1, LANE), F32).at[0, :N_GROUPS].set(router_group_b[li])
        br = br.at[0, N_GROUPS:N_GROUPS + N_EXPERTS].set(router_expert_b[li])
        first_block = nct if last else 0
        xn, h2, rid, rw, cnt = _merge(ml, df, gl, of, w_branch[li].astype(BF16), w_out[li].astype(BF16), xs,
                                      modtab, norm_ffn_g[li], wr, br, tm, nct, first_block)
        so = xn.shape[1]
        t = b * so
        te, tv, pos, npad = _route_plan(rid.reshape(t, LANE), cnt)
        xsorted = _dispatch_rows(h2.reshape(t, d), pos, npad)
        ys = _gmm(te, tv, xsorted, moe_w_gate[li], moe_w_up[li], moe_w_down[li])
        y0 = jnp.take(ys, pos[:, 0], axis=0).reshape(b, so, d)
        y1 = jnp.take(ys, pos[:, 1], axis=0).reshape(b, so, d)
        xs = _combine(xn, y0, y1, rw, modtab, final_norm_g, tm, nct, first_block, last)
    return xs
```

```python
import functools
import math

import numpy as np
import jax
import jax.numpy as jnp
from jax import lax
from jax.experimental import pallas as pl
from jax.experimental.pallas import tpu as pltpu

F32 = jnp.float32
BF16 = jnp.bfloat16
HIGHEST = lax.Precision.HIGHEST

D_MODEL = 1024
GRID_W = 64
NORM_EPS = 1e-6
CHUNK = 64
ROPE_THETA = 10000.0
ML_HEADS, ML_DK, ML_DV = 4, 128, 128
DF_HEADS, DF_HD, DF_DV = 4, 64, 128
GL_HEADS, GL_DK, GL_DV, GL_RANK, GL_TAU = 4, 64, 128, 16, 16.0
N_BRANCH, BRANCH_W = 3, 512
N_GROUPS, EXPERTS_PER_GROUP, TOP_K, D_EXPERT = 4, 8, 2, 512
N_EXPERTS = N_GROUPS * EXPERTS_PER_GROUP
IN_SIZES = (
    ML_HEADS * ML_DK, ML_HEADS * ML_DK, ML_HEADS * ML_DV, ML_HEADS * ML_DV, 4 * ML_HEADS,
    DF_HEADS * 2 * DF_HD, DF_HEADS * 2 * DF_HD, DF_HEADS * DF_DV,
    GL_HEADS * GL_DK, GL_HEADS * GL_DK, GL_HEADS * GL_DV, GL_HEADS * GL_DV, 2 * GL_RANK,
    N_BRANCH * D_MODEL,
)

LANE = 128
SUBLANE = 8
VMEM_LIMIT = 56 * 1024 * 1024
MOE_TILE = 512

A_MQ, A_MK, A_MV, A_GQ, A_GK, A_GV, A_DV = 0, 4, 8, 12, 14, 16, 20
A_WIDTH = 24 * LANE
F_DQ, F_DK, F_MO, F_GG, F_GT, F_MG, F_GA = 0, 4, 8, 12, 16, 40, 41
F_WIDTH = 42 * LANE
HEAD_W = 4 * LANE


def _cparams(sem):
    return pltpu.CompilerParams(dimension_semantics=sem, vmem_limit_bytes=VMEM_LIMIT)


def _const_spec(shape):
    nd = len(shape)
    return pl.BlockSpec(shape, lambda *_: (0,) * nd, pipeline_mode=pl.Buffered(1))


def _logsig(x):
    return jnp.minimum(x, 0.0) - jnp.log1p(jnp.exp(-jnp.abs(x)))


def _dot(a, b):
    return jnp.dot(a, b, preferred_element_type=F32)


def _dot_nt(a, b):
    return lax.dot_general(a, b, (((1,), (1,)), ((), ())), preferred_element_type=F32)


def _dot_tn(a, b):
    return lax.dot_general(a, b, (((0,), (0,)), ((), ())), preferred_element_type=F32)


def _rms(x, g):
    return x * lax.rsqrt(jnp.mean(x * x, axis=-1, keepdims=True) + NORM_EPS) * g


def _mod_kernel(c_ref, w_ref, b_ref, o_ref):
    c = c_ref[...]
    s = c * jax.nn.sigmoid(c)
    o_ref[...] = jnp.dot(s, w_ref[...], precision=HIGHEST, preferred_element_type=F32) + b_ref[...]


def _modulation(cc, w_mod, b_mod):
    rows, d = cc.shape
    n = w_mod.shape[1]
    tn = 512
    return pl.pallas_call(
        _mod_kernel,
        out_shape=jax.ShapeDtypeStruct((rows, n), F32),
        grid=(n // tn,),
        in_specs=[pl.BlockSpec((rows, d), lambda j: (0, 0)),
                  pl.BlockSpec((d, tn), lambda j: (0, j)),
                  pl.BlockSpec((1, tn), lambda j: (0, j))],
        out_specs=pl.BlockSpec((rows, tn), lambda j: (0, j)),
        compiler_params=_cparams(("parallel",)),
        name="adaln_mod",
    )(cc, w_mod, b_mod.reshape(1, n))


def _inproj_kernel(x_ref, mod_ref, g_ref, wa_ref, wf_ref, oa_ref, of_ref):
    y = _rms(x_ref[...], g_ref[...])
    h = (y * (1.0 + mod_ref[1:2, :]) + mod_ref[0:1, :]).astype(BF16)
    oa_ref[...] = _dot(h, wa_ref[...]).astype(BF16)
    of_ref[...] = _dot(h, wf_ref[...])


def _inproj(xs, modtab, g, wa, wf, tm, nct):
    b, s, d = xs.shape
    kind = lambda i: jnp.where(i >= nct, 1, 0)
    return pl.pallas_call(
        _inproj_kernel,
        out_shape=(jax.ShapeDtypeStruct((b, s, A_WIDTH), BF16), jax.ShapeDtypeStruct((b, s, F_WIDTH), F32)),
        grid=(b, s // tm),
        in_specs=[pl.BlockSpec((None, tm, d), lambda bi, i: (bi, i, 0)),
                  pl.BlockSpec((None, None, SUBLANE, d), lambda bi, i: (bi, kind(i), 0, 0)),
                  _const_spec((1, d)), _const_spec((d, A_WIDTH)), _const_spec((d, F_WIDTH))],
        out_specs=(pl.BlockSpec((None, tm, A_WIDTH), lambda bi, i: (bi, i, 0)),
                   pl.BlockSpec((None, tm, F_WIDTH), lambda bi, i: (bi, i, 0))),
        compiler_params=_cparams(("parallel", "parallel")),
        name="in_proj",
    )(xs, modtab, g.reshape(1, d), wa, wf)


def _chunk_ids(t, nct, nch):
    return t, jnp.where(t < nct, nct - 1 - t, nch - 1 - (t - nct))


def _head_cols(h):
    return slice(h * LANE, (h + 1) * LANE)


def _transpose_chunks(src_ref, dst_ref, nch):
    eye = (lax.broadcasted_iota(jnp.int32, (LANE, LANE), 0)
           == lax.broadcasted_iota(jnp.int32, (LANE, LANE), 1)).astype(BF16)

    def body(c, carry):
        blk = src_ref[pl.ds(pl.multiple_of(c * CHUNK, CHUNK), CHUNK), :]
        for h in range(HEAD_W // LANE):
            dst_ref[c, h] = _dot_nt(eye, blk[:, _head_cols(h)]).astype(BF16)
        return carry

    lax.fori_loop(0, nch, body, 0, unroll=2)


def _mlstm_kernel(q_ref, k_ref, v_ref, o_ref, gcol_ref, grow_ref, bcol_ref, brow_ref, ng_ref, out_ref,
                  hf_ref, hb_ref, ct_ref, kt_ref, *, nct, nch):
    L = CHUNK
    H = ML_HEADS
    _transpose_chunks(k_ref, kt_ref, nch)
    jj = lax.broadcasted_iota(jnp.int32, (L, L), 0)
    ii = lax.broadcasted_iota(jnp.int32, (L, L), 1)
    lane = lax.broadcasted_iota(jnp.int32, (L, LANE), 1)
    ones_col = jnp.where(lane == 0, 1.0, 0.0).astype(BF16)
    scale = ML_DK ** -0.5
    vis = ((ii <= jj), (ii >= jj))
    cum = tuple(jnp.where(m_, 1.0, 0.0).astype(BF16) for m_ in vis)
    cum_t = tuple(jnp.where(m_, 1.0, 0.0).astype(BF16) for m_ in ((jj <= ii), (jj >= ii)))
    ct_ref[...] = jnp.zeros(ct_ref.shape, F32)

    def split2(x):
        hi = x.astype(BF16)
        return hi, (x - hi.astype(F32)).astype(BF16)

    def step(t, carry):
        chunks = _chunk_ids(t, nct, nch)
        rows = [pl.ds(pl.multiple_of(c * L, L), L) for c in chunks]
        chains = [(d, h) for d in range(2) for h in range(H)]
        q, kt, vext, ct, sq, qc = {}, {}, {}, {}, {}, {}
        for d, h in chains:
            q[d, h] = q_ref[rows[d], _head_cols(h)]
            kt[d, h] = kt_ref[chunks[d], h]
            vext[d, h] = jnp.concatenate([v_ref[rows[d], _head_cols(h)], ones_col], axis=1)
            ct[d, h] = ct_ref[d, h]
            sq[d, h] = _dot(q[d, h], kt[d, h])
            qc[d, h] = _dot(q[d, h], ct[d, h].astype(BF16))
        gr, bc_all, br_all = [], [], []
        for d in range(2):
            gc = gcol_ref[rows[d], :] + bcol_ref[...]
            gr.append(grow_ref[chunks[d]] + brow_ref[...])
            chi, clo = split2(_logsig(gc))
            rhi, rlo = split2(_logsig(gr[d]))
            bc_all.append(_dot(cum[d], chi) + _dot(cum[d], clo))
            br_all.append(_dot(rhi, cum_t[d]) + _dot(rlo, cum_t[d]))
        w, w_inter, einv, decay, kw, new = {}, {}, {}, {}, {}, []
        for d, h in chains:
            m = carry[d * H + h]
            gi = 2 * d * H + h
            b_row = br_all[d][gi + H:gi + H + 1, :]
            b_col = bc_all[d][:, gi + H:gi + H + 1]
            b_last = b_row[:, L - 1:L] if d == 0 else b_row[:, 0:1]
            u_row = gr[d][gi:gi + 1, :] - b_row
            g_col = jnp.maximum(jnp.max(jnp.where(vis[d], u_row, -jnp.inf), axis=1, keepdims=True), m)
            w[d, h] = jnp.exp(jnp.where(vis[d], u_row - g_col, -jnp.inf)) * scale
            w_inter[d, h] = jnp.exp(m - g_col)
            einv[d, h] = jnp.exp(-(b_col + g_col))
            m_new = b_last + jnp.maximum(m, jnp.max(u_row, axis=1, keepdims=True))
            wk_row = jnp.exp(b_last + u_row - m_new) * scale
            decay[d, h] = jnp.exp(b_last + m - m_new)
            kw[d, h] = (kt[d, h].astype(F32) * wk_row).astype(BF16)
            new.append(m_new)
        for d, h in chains:
            ct_ref[d, h] = decay[d, h] * ct[d, h] + _dot(kw[d, h], vext[d, h])
        sv = {}
        for d, h in chains:
            sv[d, h] = _dot((sq[d, h] * w[d, h]).astype(BF16), vext[d, h])
        for d, h in chains:
            num = sv[d, h][:, :ML_DV] + w_inter[d, h] * qc[d, h][:, :ML_DV]
            den = sv[d, h][:, ML_DV:ML_DV + 1] + w_inter[d, h] * qc[d, h][:, ML_DV:ML_DV + 1]
            hh = num / jnp.maximum(jnp.abs(den), einv[d, h])
            if d == 0:
                hf_ref[rows[d], _head_cols(h)] = hh
            else:
                hb_ref[rows[d], _head_cols(h)] = hh
        return tuple(new)

    zero = jnp.zeros((1, 1), F32)
    lax.fori_loop(0, nch, step, (zero,) * (2 * H))
    for h in range(H):
        y = _rms(hf_ref[:, _head_cols(h)] + hb_ref[:, _head_cols(h)], ng_ref[...])
        out_ref[:, _head_cols(h)] = (y * jax.nn.sigmoid(o_ref[:, _head_cols(h)])).astype(BF16)


def _mlstm(oa, of, grow, gate_b, norm_g, nct, nch):
    b, s, _ = oa.shape
    bcol = jnp.zeros((1, LANE), F32).at[0, :4 * ML_HEADS].set(gate_b)
    brow = gate_b.reshape(4 * ML_HEADS, 1)
    sect = lambda base: (lambda bi: (bi, 0, base // 4))
    return pl.pallas_call(
        functools.partial(_mlstm_kernel, nct=nct, nch=nch),
        out_shape=jax.ShapeDtypeStruct((b, s, HEAD_W), BF16),
        grid=(b,),
        in_specs=[pl.BlockSpec((None, s, HEAD_W), sect(A_MQ)),
                  pl.BlockSpec((None, s, HEAD_W), sect(A_MK)),
                  pl.BlockSpec((None, s, HEAD_W), sect(A_MV)),
                  pl.BlockSpec((None, s, HEAD_W), sect(F_MO)),
                  pl.BlockSpec((None, s, LANE), lambda bi: (bi, 0, F_MG)),
                  pl.BlockSpec((None, nch, 4 * ML_HEADS, CHUNK), lambda bi: (bi, 0, 0, 0)),
                  pl.BlockSpec((1, LANE), lambda bi: (0, 0)),
                  pl.BlockSpec((4 * ML_HEADS, 1), lambda bi: (0, 0)),
                  pl.BlockSpec((1, ML_DV), lambda bi: (0, 0))],
        out_specs=pl.BlockSpec((None, s, HEAD_W), lambda bi: (bi, 0, 0)),
        scratch_shapes=[pltpu.VMEM((s, HEAD_W), F32), pltpu.VMEM((s, HEAD_W), F32),
                        pltpu.VMEM((2, ML_HEADS, ML_DK, 2 * LANE), F32),
                        pltpu.VMEM((nch, ML_HEADS, ML_DK, CHUNK), BF16)],
        compiler_params=_cparams(("parallel",)),
        name="mlstm",
    )(oa, oa, oa, of, of, grow, bcol, brow, norm_g.reshape(1, ML_DV))


def _gla_tables():
    L = CHUNK
    G = np.zeros((2, 7 * L, L), np.float32)
    lvl = np.full((2, L, L), 7, np.int32)
    for d in range(2):
        for p in range(L):
            if d == 0:
                G[d, p, :p + 1] = 1
            else:
                G[d, p, p:] = 1
        for li, s in enumerate((32, 16, 8, 4, 2, 1)):
            for p in range(L):
                base = (p // (2 * s)) * 2 * s
                row = (li + 1) * L + p
                if d == 0:
                    mid = base + s
                    if p >= mid:
                        G[d, row, mid + 1:p + 1] = 1
                    else:
                        G[d, row, p + 1:mid + 1] = 1
                else:
                    mid = base + s - 1
                    if p <= mid:
                        G[d, row, p:mid] = 1
                    else:
                        G[d, row, mid:p] = 1
            blk = np.arange(L) // (2 * s)
            upper = (np.arange(L) % (2 * s)) >= s
            same = blk[:, None] == blk[None, :]
            if d == 0:
                sel = same & upper[:, None] & ~upper[None, :]
            else:
                sel = same & ~upper[:, None] & upper[None, :]
            lvl[d][sel] = li
        lvl[d][np.arange(L), np.arange(L)] = 6
    return G, lvl


GL_PAIRS = GL_HEADS * GL_DK // LANE


def _gla_kernel(q_ref, k_ref, v_ref, g_ref, a_ref, wa_ref, ba_ref, gm_ref, lvl_ref, ng_ref, out_ref,
                acc_ref, la_ref, st_ref, vt_ref, *, nct, nch):
    L = CHUNK
    _transpose_chunks(v_ref, vt_ref, nch)
    for d in range(2):
        pre = jnp.dot(a_ref[...], wa_ref[d], precision=HIGHEST, preferred_element_type=F32) + ba_ref[d]
        la_ref[d] = _logsig(pre) * (1.0 / GL_TAU)
    acc_ref[...] = jnp.zeros(acc_ref.shape, F32)
    st_ref[...] = jnp.zeros(st_ref.shape, F32)
    first = lax.broadcasted_iota(jnp.int32, (L, LANE), 1) < GL_DK
    first2 = lax.broadcasted_iota(jnp.int32, (GL_DV, LANE), 1) < GL_DK

    def split(x):
        return jnp.concatenate([jnp.where(first, x, 0.0), jnp.where(first, 0.0, x)], axis=0).astype(BF16)

    def step(t, carry):
        chunks = _chunk_ids(t, nct, nch)
        rows = [pl.ds(pl.multiple_of(c * L, L), L) for c in chunks]
        chains = [(d, p) for d in range(2) for p in range(GL_PAIRS)]
        xs = []
        for d in range(2):
            lac = la_ref[d, rows[d], :]
            hi = lac.astype(BF16)
            lo = (lac - hi.astype(F32)).astype(BF16)
            xs.append(_dot(gm_ref[d], hi) + _dot(gm_ref[d], lo))
        q, k, cs, tot, st, inter, upd = {}, {}, {}, {}, {}, {}, {}
        for d, p in chains:
            q[d, p] = q_ref[rows[d], _head_cols(p)].astype(F32) * (GL_DK ** -0.5)
            k[d, p] = k_ref[rows[d], _head_cols(p)].astype(F32)
            cs[d, p] = xs[d][0:L, _head_cols(p)]
            tot[d, p] = cs[d, p][L - 1:L] if d == 0 else cs[d, p][0:1]
            st[d, p] = st_ref[d, p]
        for d, p in chains:
            inter[d, p] = _dot_nt(split(q[d, p] * jnp.exp(cs[d, p])), st[d, p].astype(BF16))
            ke = (k[d, p] * jnp.exp(tot[d, p] - cs[d, p])).astype(BF16)
            u = [_dot(vt_ref[chunks[d], 2 * p + hh], ke) for hh in range(2)]
            st_ref[d, p] = st[d, p] * jnp.exp(tot[d, p]) + jnp.where(first2, u[0], u[1])
        amat = {}
        for d, p in chains:
            amat[d, p] = jnp.where(lvl_ref[d] == 6, _dot_nt(split(q[d, p]), k[d, p].astype(BF16)), 0.0)
        for li in range(6):
            for d, p in chains:
                e = jnp.exp(xs[d][(li + 1) * L:(li + 2) * L, _head_cols(p)])
                lev = _dot_nt(split(q[d, p] * e), (k[d, p] * e).astype(BF16))
                amat[d, p] = jnp.where(lvl_ref[d] == li, lev, amat[d, p])
        for d, p in chains:
            a = amat[d, p].astype(BF16)
            for hh in range(2):
                cols = _head_cols(2 * p + hh)
                acc_ref[rows[d], cols] += (_dot(a[hh * L:(hh + 1) * L], v_ref[rows[d], cols])
                                           + inter[d, p][hh * L:(hh + 1) * L])
        return carry

    lax.fori_loop(0, nch, step, 0)
    for h in range(GL_HEADS):
        cols = _head_cols(h)
        g = g_ref[:, cols]
        out_ref[:, cols] = (_rms(acc_ref[:, cols], ng_ref[...]) * (g * jax.nn.sigmoid(g))).astype(BF16)


def _gla(oa, of, w_alpha, b_alpha, norm_g, nct, nch):
    b, s, _ = oa.shape
    gmat, lvl = _gla_tables()
    lvl = np.concatenate([lvl, lvl], axis=1)
    qk_w = GL_HEADS * GL_DK
    wa = jnp.zeros((2, LANE, qk_w), F32)
    for d in range(2):
        wa = wa.at[d, d * GL_RANK:(d + 1) * GL_RANK, :].set(w_alpha[d])
    ba = b_alpha.reshape(2, 1, qk_w)
    full = lambda nd: (lambda bi: (0,) * nd)
    return pl.pallas_call(
        functools.partial(_gla_kernel, nct=nct, nch=nch),
        out_shape=jax.ShapeDtypeStruct((b, s, HEAD_W), BF16),
        grid=(b,),
        in_specs=[pl.BlockSpec((None, s, qk_w), lambda bi: (bi, 0, A_GQ * LANE // qk_w)),
                  pl.BlockSpec((None, s, qk_w), lambda bi: (bi, 0, A_GK * LANE // qk_w)),
                  pl.BlockSpec((None, s, HEAD_W), lambda bi: (bi, 0, A_GV // 4)),
                  pl.BlockSpec((None, s, HEAD_W), lambda bi: (bi, 0, F_GG // 4)),
                  pl.BlockSpec((None, s, LANE), lambda bi: (bi, 0, F_GA)),
                  pl.BlockSpec((2, LANE, qk_w), full(3)),
                  pl.BlockSpec((2, 1, qk_w), full(3)),
                  pl.BlockSpec((2, 7 * CHUNK, CHUNK), full(3)),
                  pl.BlockSpec((2, 2 * CHUNK, CHUNK), full(3)),
                  pl.BlockSpec((1, GL_DV), full(2))],
        out_specs=pl.BlockSpec((None, s, HEAD_W), lambda bi: (bi, 0, 0)),
        scratch_shapes=[pltpu.VMEM((s, HEAD_W), F32), pltpu.VMEM((2, s, qk_w), F32),
                        pltpu.VMEM((2, GL_PAIRS, GL_DV, LANE), F32),
                        pltpu.VMEM((nch, GL_HEADS, GL_DV, CHUNK), BF16)],
        compiler_params=_cparams(("parallel",)),
        name="gla",
    )(oa, oa, oa, of, of, wa, ba, jnp.asarray(gmat, BF16), jnp.asarray(lvl), norm_g.reshape(1, GL_DV))


def _rope_tables(n):
    t = np.arange(n)
    n_freq = DF_HD // 4
    inv = jnp.asarray(ROPE_THETA, F32) ** (-jnp.arange(n_freq, dtype=F32) / n_freq)
    ang_r = jnp.asarray(t // GRID_W, F32)[:, None] * inv
    ang_c = jnp.asarray(t % GRID_W, F32)[:, None] * inv
    ang = jnp.concatenate([ang_r, ang_r, ang_c, ang_c] * 2, axis=1)
    first = (np.arange(LANE) % 32) < 16
    cos, sin = jnp.cos(ang), jnp.sin(ang)
    return cos, jnp.where(first, -sin, 0.0), jnp.where(first, 0.0, sin)


DF_ROW_GROUPS = 4


def _rope(x, cos, sa, sb):
    return x * cos + pltpu.roll(x, LANE - 16, 1) * sa + pltpu.roll(x, 16, 1) * sb


def _diff_kernel(q_ref, k_ref, v_ref, cq_ref, saq_ref, sbq_ref, ck_ref, sak_ref, sbk_ref, lam_ref, ng_ref,
                 out_ref, kr_ref, *, nct, n_ctx, lam_init):
    s = k_ref.shape[0]
    tq = q_ref.shape[0]
    i = pl.program_id(2)

    @pl.when(i == 0)
    def _():
        kr_ref[0:n_ctx, :] = k_ref[0:n_ctx, :].astype(BF16)
        kr_ref[n_ctx:s, :] = _rope(k_ref[n_ctx:s, :], ck_ref[...], sak_ref[...], sbk_ref[...]).astype(BF16)

    lp = lam_ref[...]
    lam = (jnp.exp(jnp.sum(lp[0:1] * lp[1:2], axis=1, keepdims=True))
           - jnp.exp(jnp.sum(lp[2:3] * lp[3:4], axis=1, keepdims=True)) + lam_init)
    lane = lax.broadcasted_iota(jnp.int32, q_ref.shape, 1)

    def attend(qb, nk):
        qb = qb * (DF_HD ** -0.5)
        qs = jnp.concatenate([jnp.where(lane < DF_HD, qb, 0.0), jnp.where(lane >= DF_HD, qb, 0.0)],
                             axis=0).astype(BF16)
        rg = 2 * tq // DF_ROW_GROUPS
        scs = [_dot_nt(qs[g * rg:(g + 1) * rg], kr_ref[0:nk, :]) for g in range(DF_ROW_GROUPS)]
        ovs = []
        for sc in scs:
            p = jnp.exp(sc - jnp.max(sc, axis=1, keepdims=True))
            rl = 1.0 / jnp.sum(p, axis=1, keepdims=True)
            ovs.append((p.astype(BF16), rl))
        ov = jnp.concatenate([_dot(p, v_ref[0:nk, :]) * rl for p, rl in ovs], axis=0)
        o = ov[0:tq] - lam * ov[tq:2 * tq]
        out_ref[...] = (_rms(o, ng_ref[...]) * (1.0 - lam_init)).astype(BF16)

    @pl.when(i < nct)
    def _():
        attend(q_ref[...], n_ctx)

    @pl.when(i >= nct)
    def _():
        attend(_rope(q_ref[...], cq_ref[...], saq_ref[...], sbq_ref[...]), s)


def _diff_attn(oa, of, rope, df_lambda, norm_g, tq, nct, n_ctx, lam_init):
    b, s, _ = oa.shape
    n = s - n_ctx
    cos, sa, sb = rope
    lam_p = jnp.zeros((4, LANE), F32).at[:, :DF_HD].set(df_lambda)
    qblk = lambda bi, h, i: (jnp.maximum(i - nct, 0), 0)
    full = lambda bi, h, i: (0, 0)
    return pl.pallas_call(
        functools.partial(_diff_kernel, nct=nct, n_ctx=n_ctx, lam_init=lam_init),
        out_shape=jax.ShapeDtypeStruct((b, s, DF_HEADS * DF_DV), BF16),
        grid=(b, DF_HEADS, s // tq),
        in_specs=[pl.BlockSpec((None, tq, LANE), lambda bi, h, i: (bi, i, F_DQ + h)),
                  pl.BlockSpec((None, s, LANE), lambda bi, h, i: (bi, 0, F_DK + h)),
                  pl.BlockSpec((None, s, LANE), lambda bi, h, i: (bi, 0, A_DV + h)),
                  pl.BlockSpec((tq, LANE), qblk), pl.BlockSpec((tq, LANE), qblk), pl.BlockSpec((tq, LANE), qblk),
                  pl.BlockSpec((n, LANE), full), pl.BlockSpec((n, LANE), full), pl.BlockSpec((n, LANE), full),
                  pl.BlockSpec((4, LANE), full), pl.BlockSpec((1, DF_DV), full)],
        out_specs=pl.BlockSpec((None, tq, LANE), lambda bi, h, i: (bi, i, h)),
        scratch_shapes=[pltpu.VMEM((s, LANE), BF16)],
        compiler_params=_cparams(("parallel", "parallel", "arbitrary")),
        name="diff_attn",
    )(of, of, oa, cos, sa, sb, cos, sa, sb, lam_p, norm_g.reshape(1, DF_DV))


def _merge_kernel(ml_ref, df_ref, gl_ref, g0_ref, g1_ref, g2_ref, wb_ref, wo_ref, x_ref, mod_ref, nf_ref,
                  wr_ref, br_ref, tri_ref, xo_ref, h2_ref, rid_ref, rw_ref, cnt_ref):
    y = (jax.nn.sigmoid(g0_ref[...]) * _dot(ml_ref[...], wb_ref[0])
         + jax.nn.sigmoid(g1_ref[...]) * _dot(df_ref[...], wb_ref[1])
         + jax.nn.sigmoid(g2_ref[...]) * _dot(gl_ref[...], wb_ref[2]))
    xn = x_ref[...] + mod_ref[2:3, :] * _dot(y.astype(BF16), wo_ref[...])
    xo_ref[...] = xn
    h2 = _rms(xn, nf_ref[...]) * (1.0 + mod_ref[4:5, :]) + mod_ref[3:4, :]
    h2_hi = h2.astype(BF16)
    h2_ref[...] = h2_hi
    h2_lo = (h2 - h2_hi.astype(F32)).astype(BF16)
    logits = (_dot(h2_hi, wr_ref[0]) + (_dot(h2_lo, wr_ref[0]) + _dot(h2_hi, wr_ref[1]))) + br_ref[...]
    lane = lax.broadcasted_iota(jnp.int32, logits.shape, 1)
    lane_f = lane.astype(F32)
    neg = -jnp.inf

    def first_max(vals):
        mx = jnp.max(vals, axis=1, keepdims=True)
        return mx, jnp.min(jnp.where(vals == mx, lane_f, float(LANE)), axis=1, keepdims=True)

    is_grp = lane < N_GROUPS
    gmax, gidx = first_max(jnp.where(is_grp, logits, neg))
    pg_top = 1.0 / jnp.sum(jnp.where(is_grp, jnp.exp(logits - gmax), 0.0), axis=1, keepdims=True)
    lo = N_GROUPS + gidx * EXPERTS_PER_GROUP
    in_grp = (lane_f >= lo) & (lane_f < lo + EXPERTS_PER_GROUP)
    le = jnp.where(in_grp, logits, neg)
    m1, e1 = first_max(le)
    m2, e2 = first_max(jnp.where(lane_f == e1, neg, le))
    r = jnp.exp(m2 - m1)
    w1 = pg_top / (1.0 + r)
    w2 = pg_top * r / (1.0 + r)

    @pl.when((pl.program_id(0) == 0) & (pl.program_id(1) == 0))
    def _():
        cnt_ref[...] = jnp.zeros(cnt_ref.shape, F32)

    hot1 = jnp.where(lane_f == e1, 1.0, 0.0)
    hot2 = jnp.where(lane_f == e2, 1.0, 0.0)
    tot1 = jnp.sum(hot1, axis=0, keepdims=True)
    tot2 = jnp.sum(hot2, axis=0, keepdims=True)
    cnt = cnt_ref[...]
    before1 = cnt + _dot(tri_ref[...], hot1.astype(BF16))
    before2 = cnt + tot1 + _dot(tri_ref[...], hot2.astype(BF16))
    rank1 = jnp.sum(hot1 * before1, axis=1, keepdims=True)
    rank2 = jnp.sum(hot2 * before2, axis=1, keepdims=True)
    cnt_ref[...] = cnt + tot1 + tot2
    rid = jnp.where(lane == 0, e1 - N_GROUPS, jnp.where(lane == 1, e2 - N_GROUPS,
                    jnp.where(lane == 2, rank1, jnp.where(lane == 3, rank2, 0.0))))
    rid_ref[...] = rid.astype(jnp.int32)
    rw_ref[...] = jnp.where(lane == 0, w1, jnp.where(lane == 1, w2, 0.0))


def _merge(ml, df, gl, of, wb, wo, xs, modtab, nf, wr, br, tm, nct, first_block):
    b, s, d = xs.shape
    nb = s // tm - first_block
    so = nb * tm
    kind = lambda i: jnp.where(i + first_block >= nct, 1, 0)
    row = lambda bi, i: (bi, i + first_block, 0)
    gate = lambda br_: (lambda bi, i: (bi, i + first_block, F_GT // (d // LANE) + br_))
    outrow = lambda bi, i: (bi, i, 0)
    tri = jnp.asarray(np.tril(np.ones((tm, tm), np.float32), -1), BF16)
    return pl.pallas_call(
        _merge_kernel,
        out_shape=(jax.ShapeDtypeStruct((b, so, d), F32), jax.ShapeDtypeStruct((b, so, d), BF16),
                   jax.ShapeDtypeStruct((b, so, LANE), jnp.int32), jax.ShapeDtypeStruct((b, so, LANE), F32),
                   jax.ShapeDtypeStruct((1, LANE), F32)),
        grid=(b, nb),
        in_specs=[pl.BlockSpec((None, tm, BRANCH_W), row), pl.BlockSpec((None, tm, BRANCH_W), row),
                  pl.BlockSpec((None, tm, BRANCH_W), row),
                  pl.BlockSpec((None, tm, d), gate(0)), pl.BlockSpec((None, tm, d), gate(1)),
                  pl.BlockSpec((None, tm, d), gate(2)),
                  _const_spec((N_BRANCH, BRANCH_W, d)), _const_spec((d, d)),
                  pl.BlockSpec((None, tm, d), row),
                  pl.BlockSpec((None, None, SUBLANE, d), lambda bi, i: (bi, kind(i), 0, 0)),
                  _const_spec((1, d)), _const_spec((2, d, LANE)), _const_spec((1, LANE)), _const_spec((tm, tm))],
        out_specs=(pl.BlockSpec((None, tm, d), outrow), pl.BlockSpec((None, tm, d), outrow),
                   pl.BlockSpec((None, tm, LANE), outrow), pl.BlockSpec((None, tm, LANE), outrow),
                   pl.BlockSpec((1, LANE), lambda bi, i: (0, 0))),
        compiler_params=_cparams(("arbitrary", "arbitrary")),
        name="merge_route",
    )(ml, df, gl, of, of, of, wb, wo, xs, modtab, nf.reshape(1, d), wr, br, tri)


def _gmm_kernel(te_ref, tv_ref, x_ref, wg_ref, wu_ref, wd_ref, y_ref, wgb_ref, wub_ref, wdb_ref):
    i = pl.program_id(0)

    @pl.when((i == 0) | (te_ref[i] != te_ref[jnp.maximum(i - 1, 0)]))
    def _():
        wgb_ref[...] = wg_ref[...].astype(BF16)
        wub_ref[...] = wu_ref[...].astype(BF16)
        wdb_ref[...] = wd_ref[...].astype(BF16)

    @pl.when(tv_ref[i] > 0)
    def _():
        x = x_ref[...]
        a = _dot(x, wgb_ref[...])
        hid = (a * jax.nn.sigmoid(a)) * _dot(x, wub_ref[...])
        y_ref[...] = _dot(hid.astype(BF16), wdb_ref[...]).astype(y_ref.dtype)

    @pl.when(tv_ref[i] == 0)
    def _():
        y_ref[...] = jnp.zeros(y_ref.shape, y_ref.dtype)


def _gmm(tile_expert, tile_valid, xs, wg, wu, wd):
    npad, d = xs.shape
    tm = MOE_TILE
    de = wg.shape[2]
    return pl.pallas_call(
        _gmm_kernel,
        out_shape=jax.ShapeDtypeStruct((npad, d), BF16),
        grid_spec=pltpu.PrefetchScalarGridSpec(
            num_scalar_prefetch=2,
            grid=(npad // tm,),
            in_specs=[pl.BlockSpec((tm, d), lambda i, te, tv: (i, 0)),
                      pl.BlockSpec((None, d, de), lambda i, te, tv: (te[i], 0, 0)),
                      pl.BlockSpec((None, d, de), lambda i, te, tv: (te[i], 0, 0)),
                      pl.BlockSpec((None, de, d), lambda i, te, tv: (te[i], 0, 0))],
            out_specs=pl.BlockSpec((tm, d), lambda i, te, tv: (i, 0)),
            scratch_shapes=[pltpu.VMEM((d, de), BF16), pltpu.VMEM((d, de), BF16), pltpu.VMEM((de, d), BF16)]),
        compiler_params=_cparams(("arbitrary",)),
        name="moe_gmm",
    )(tile_expert, tile_valid, xs, wg, wu, wd)


def _route_plan(rid, cnt):
    t = rid.shape[0]
    tm = MOE_TILE
    n_tiles = (TOP_K * t + tm - 1) // tm + N_EXPERTS
    counts = cnt[0, N_GROUPS:N_GROUPS + N_EXPERTS].astype(jnp.int32)
    tiles_e = (counts + tm - 1) // tm
    tile_end = jnp.cumsum(tiles_e)
    pad_start = (tile_end - tiles_e) * tm
    eid, rank = rid[:, 0:TOP_K], rid[:, TOP_K:2 * TOP_K]
    pos = rank + jnp.sum(jnp.where(eid[:, :, None] == jnp.arange(N_EXPERTS), pad_start, 0), axis=-1)
    tile = jnp.arange(n_tiles, dtype=jnp.int32)
    tile_expert = jnp.minimum(jnp.sum(tile[:, None] >= tile_end[None, :], axis=1), N_EXPERTS - 1).astype(jnp.int32)
    tile_valid = (tile < tile_end[-1]).astype(jnp.int32)
    return tile_expert, tile_valid, pos.astype(jnp.int32), n_tiles * tm


def _combine_kernel(x_ref, y0_ref, y1_ref, rw_ref, mod_ref, fg_ref, o_ref, *, final):
    rw = rw_ref[...]
    y = rw[:, 0:1] * y0_ref[...].astype(F32) + rw[:, 1:2] * y1_ref[...].astype(F32)
    xn = x_ref[...] + mod_ref[5:6, :] * y
    o_ref[...] = _rms(xn, fg_ref[...]) if final else xn


def _combine(xs, y0, y1, rw, modtab, fg, tm, nct, first_block, final):
    b, s, d = xs.shape
    kind = lambda i: jnp.where(i + first_block >= nct, 1, 0)
    row = lambda bi, i: (bi, i, 0)
    return pl.pallas_call(
        functools.partial(_combine_kernel, final=final),
        out_shape=jax.ShapeDtypeStruct((b, s, d), F32),
        grid=(b, s // tm),
        in_specs=[pl.BlockSpec((None, tm, d), row), pl.BlockSpec((None, tm, d), row),
                  pl.BlockSpec((None, tm, d), row), pl.BlockSpec((None, tm, LANE), row),
                  pl.BlockSpec((None, None, SUBLANE, d), lambda bi, i: (bi, kind(i), 0, 0)),
                  _const_spec((1, d))],
        out_specs=pl.BlockSpec((None, tm, d), row),
        compiler_params=_cparams(("parallel", "parallel")),
        name="moe_combine",
    )(xs, y0, y1, rw, modtab, fg.reshape(1, d))


def _pack_w_in(w):
    d = w.shape[0]
    mq, mk, mv, mo, mg, dq, dk, dv, gq, gk, gv, gg, ga, gt = jnp.split(
        w, [int(i) for i in np.cumsum(IN_SIZES)[:-1]], axis=1)

    def pad_cols(t):
        return jnp.pad(t, ((0, 0), (0, LANE - t.shape[1])))

    wa = jnp.concatenate([mq, mk, mv, gq, gk, gv, dv], axis=1).astype(BF16)
    wf = jnp.concatenate([dq, dk, mo, gg, gt, pad_cols(mg), pad_cols(ga)], axis=1).astype(BF16)
    return wa, wf


def _dispatch_rows(h2, pos, npad):
    t = h2.shape[0]
    tok = jnp.arange(t, dtype=jnp.int32)
    src = jnp.zeros((npad,), jnp.int32).at[pos[:, 0]].set(tok).at[pos[:, 1]].set(tok)
    return jnp.take(h2, src, axis=0)


def kernel(x, c, ctx, c_ctx, w_mod, b_mod, norm_mix_g, norm_ffn_g, w_in, ml_gate_b, ml_norm_g, df_lambda,
           df_norm_g, gl_w_alpha, gl_b_alpha, gl_norm_g, w_branch, w_out, router_group_w, router_group_b,
           router_expert_w, router_expert_b, moe_w_gate, moe_w_up, moe_w_down, final_norm_g):
    b, n, d = x.shape
    n_ctx = ctx.shape[1]
    s = n_ctx + n
    depth = w_mod.shape[0]
    tm = 256 if n_ctx % 256 == 0 else 128
    assert n_ctx % tm == 0 and n % tm == 0 and n % GRID_W == 0 and n_ctx % CHUNK == 0 and d == D_MODEL
    nct = n_ctx // tm
    nch, nch_ctx = s // CHUNK, n_ctx // CHUNK
    rope = _rope_tables(n)
    xs = jnp.concatenate([ctx, x], axis=1)
    mod_rows = -(-(b + 1) // SUBLANE) * SUBLANE
    cc = jnp.zeros((mod_rows, d), F32).at[:b].set(c).at[b].set(c_ctx)
    for li in range(depth):
        last = li == depth - 1
        lam_init = 0.8 - 0.6 * math.exp(-0.3 * li)
        mod = _modulation(cc, w_mod[li], b_mod[li]).reshape(mod_rows, 6, d)
        mod = jnp.pad(mod, ((0, 0), (0, SUBLANE - 6), (0, 0)))
        modtab = jnp.stack([jnp.broadcast_to(mod[b], (b, SUBLANE, d)), mod[:b]], axis=1)
        wa, wf = _pack_w_in(w_in[li])
        oa, of = _inproj(xs, modtab, norm_mix_g[li], wa, wf, tm, nct)
        grow = of[:, :, F_MG * LANE:F_MG * LANE + 4 * ML_HEADS]
        grow = grow.reshape(b, nch, CHUNK, 4 * ML_HEADS).transpose(0, 1, 3, 2)
        ml = _mlstm(oa, of, grow, ml_gate_b[li], ml_norm_g[li], nch_ctx, nch)
        df = _diff_attn(oa, of, rope, df_lambda[li], df_norm_g[li], tm, nct, n_ctx, lam_init)
        gl = _gla(oa, of, gl_w_alpha[li], gl_b_alpha[li], gl_norm_g[li], nch_ctx, nch)
        wr = jnp.zeros((d, LANE), F32).at[:, :N_GROUPS].set(router_group_w[li])
        wr = wr.at[:, N_GROUPS:N_GROUPS + N_EXPERTS].set(router_expert_w[li])
        wr_hi = wr.astype(BF16)
        wr = jnp.stack([wr_hi, (wr - wr_hi.astype(F32)).astype(BF16)])
        br = jnp.zeros((1, LANE), F32).at[0, :N_GROUPS].set(router_group_b[li])
        br = br.at[0, N_GROUPS:N_GROUPS + N_EXPERTS].set(router_expert_b[li])
        first_block = nct if last else 0
        xn, h2, rid, rw, cnt = _merge(ml, df, gl, of, w_branch[li].astype(BF16), w_out[li].astype(BF16), xs,
                                      modtab, norm_ffn_g[li], wr, br, tm, nct, first_block)
        so = xn.shape[1]
        t = b * so
        te, tv, pos, npad = _route_plan(rid.reshape(t, LANE), cnt)
        xsorted = _dispatch_rows(h2.reshape(t, d), pos, npad)
        ys = _gmm(te, tv, xsorted, moe_w_gate[li], moe_w_up[li], moe_w_down[li])
        y0 = jnp.take(ys, pos[:, 0], axis=0).reshape(b, so, d)
        y1 = jnp.take(ys, pos[:, 1], axis=0).reshape(b, so, d)
        xs = _combine(xn, y0, y1, rw, modtab, final_norm_g, tm, nct, first_block, last)
    return xs
```

```python
import functools
import math

import numpy as np
import jax
import jax.numpy as jnp
from jax import lax
from jax.experimental import pallas as pl
from jax.experimental.pallas import tpu as pltpu

F32 = jnp.float32
BF16 = jnp.bfloat16
HIGHEST = lax.Precision.HIGHEST

D_MODEL = 1024
GRID_W = 64
NORM_EPS = 1e-6
CHUNK = 64
ROPE_THETA = 10000.0
ML_HEADS, ML_DK, ML_DV = 4, 128, 128
DF_HEADS, DF_HD, DF_DV = 4, 64, 128
GL_HEADS, GL_DK, GL_DV, GL_RANK, GL_TAU = 4, 64, 128, 16, 16.0
N_BRANCH, BRANCH_W = 3, 512
N_GROUPS, EXPERTS_PER_GROUP, TOP_K, D_EXPERT = 4, 8, 2, 512
N_EXPERTS = N_GROUPS * EXPERTS_PER_GROUP
IN_SIZES = (
    ML_HEADS * ML_DK, ML_HEADS * ML_DK, ML_HEADS * ML_DV, ML_HEADS * ML_DV, 4 * ML_HEADS,
    DF_HEADS * 2 * DF_HD, DF_HEADS * 2 * DF_HD, DF_HEADS * DF_DV,
    GL_HEADS * GL_DK, GL_HEADS * GL_DK, GL_HEADS * GL_DV, GL_HEADS * GL_DV, 2 * GL_RANK,
    N_BRANCH * D_MODEL,
)

LANE = 128
SUBLANE = 8
VMEM_LIMIT = 56 * 1024 * 1024
MOE_TILE = 512

A_MQ, A_MK, A_MV, A_GQ, A_GK, A_GV, A_DV = 0, 4, 8, 12, 14, 16, 20
A_WIDTH = 24 * LANE
F_DQ, F_DK, F_MO, F_GG, F_GT, F_MG, F_GA = 0, 4, 8, 12, 16, 40, 41
F_WIDTH = 42 * LANE
HEAD_W = 4 * LANE


def _cparams(sem):
    return pltpu.CompilerParams(dimension_semantics=sem, vmem_limit_bytes=VMEM_LIMIT)


def _const_spec(shape):
    nd = len(shape)
    return pl.BlockSpec(shape, lambda *_: (0,) * nd, pipeline_mode=pl.Buffered(1))


def _logsig(x):
    return jnp.minimum(x, 0.0) - jnp.log1p(jnp.exp(-jnp.abs(x)))


def _dot(a, b):
    return jnp.dot(a, b, preferred_element_type=F32)


def _dot_nt(a, b):
    return lax.dot_general(a, b, (((1,), (1,)), ((), ())), preferred_element_type=F32)


def _dot_tn(a, b):
    return lax.dot_general(a, b, (((0,), (0,)), ((), ())), preferred_element_type=F32)


def _rms(x, g):
    return x * lax.rsqrt(jnp.mean(x * x, axis=-1, keepdims=True) + NORM_EPS) * g


def _mod_kernel(c_ref, w_ref, b_ref, o_ref):
    c = c_ref[...]
    s = c * jax.nn.sigmoid(c)
    o_ref[...] = jnp.dot(s, w_ref[...], precision=HIGHEST, preferred_element_type=F32) + b_ref[...]


def _modulation(cc, w_mod, b_mod, li):
    rows, d = cc.shape
    n = w_mod.shape[2]
    tn = 512
    return pl.pallas_call(
        _mod_kernel,
        out_shape=jax.ShapeDtypeStruct((rows, n), F32),
        grid=(n // tn,),
        in_specs=[pl.BlockSpec((rows, d), lambda j: (0, 0)),
                  pl.BlockSpec((None, d, tn), lambda j: (li, 0, j)),
                  pl.BlockSpec((1, tn), lambda j: (0, j))],
        out_specs=pl.BlockSpec((rows, tn), lambda j: (0, j)),
        compiler_params=_cparams(("parallel",)),
        name="adaln_mod",
    )(cc, w_mod, b_mod.reshape(1, n))


def _inproj_kernel(x_ref, mod_ref, g_ref, wa_ref, wf_ref, oa_ref, of_ref):
    y = _rms(x_ref[...], g_ref[...])
    h = (y * (1.0 + mod_ref[1:2, :]) + mod_ref[0:1, :]).astype(BF16)
    oa_ref[...] = _dot(h, wa_ref[...]).astype(BF16)
    of_ref[...] = _dot(h, wf_ref[...])


def _inproj(xs, modtab, g, wa, wf, tm, nct):
    b, s, d = xs.shape
    kind = lambda i: jnp.where(i >= nct, 1, 0)
    return pl.pallas_call(
        _inproj_kernel,
        out_shape=(jax.ShapeDtypeStruct((b, s, A_WIDTH), BF16), jax.ShapeDtypeStruct((b, s, F_WIDTH), F32)),
        grid=(b, s // tm),
        in_specs=[pl.BlockSpec((None, tm, d), lambda bi, i: (bi, i, 0)),
                  pl.BlockSpec((None, None, SUBLANE, d), lambda bi, i: (bi, kind(i), 0, 0)),
                  _const_spec((1, d)), _const_spec((d, A_WIDTH)), _const_spec((d, F_WIDTH))],
        out_specs=(pl.BlockSpec((None, tm, A_WIDTH), lambda bi, i: (bi, i, 0)),
                   pl.BlockSpec((None, tm, F_WIDTH), lambda bi, i: (bi, i, 0))),
        compiler_params=_cparams(("parallel", "parallel")),
        name="in_proj",
    )(xs, modtab, g.reshape(1, d), wa, wf)


def _chunk_ids(t, nct, nch):
    return t, jnp.where(t < nct, nct - 1 - t, nch - 1 - (t - nct))


def _head_cols(h):
    return slice(h * LANE, (h + 1) * LANE)


def _transpose_chunks(src_ref, dst_ref, nch):
    eye = (lax.broadcasted_iota(jnp.int32, (LANE, LANE), 0)
           == lax.broadcasted_iota(jnp.int32, (LANE, LANE), 1)).astype(BF16)

    def body(c, carry):
        blk = src_ref[pl.ds(pl.multiple_of(c * CHUNK, CHUNK), CHUNK), :]
        for h in range(HEAD_W // LANE):
            dst_ref[c, h] = _dot_nt(eye, blk[:, _head_cols(h)]).astype(BF16)
        return carry

    lax.fori_loop(0, nch, body, 0, unroll=2)


def _mlstm_kernel(q_ref, k_ref, v_ref, o_ref, gcol_ref, grow_ref, bcol_ref, brow_ref, ng_ref, out_ref,
                  hf_ref, hb_ref, ct_ref, kt_ref, *, nct, nch):
    L = CHUNK
    H = ML_HEADS
    _transpose_chunks(k_ref, kt_ref, nch)
    jj = lax.broadcasted_iota(jnp.int32, (L, L), 0)
    ii = lax.broadcasted_iota(jnp.int32, (L, L), 1)
    lane = lax.broadcasted_iota(jnp.int32, (L, LANE), 1)
    ones_col = jnp.where(lane == 0, 1.0, 0.0).astype(BF16)
    scale = ML_DK ** -0.5
    vis = ((ii <= jj), (ii >= jj))
    cum = tuple(jnp.where(m_, 1.0, 0.0).astype(BF16) for m_ in vis)
    cum_t = tuple(jnp.where(m_, 1.0, 0.0).astype(BF16) for m_ in ((jj <= ii), (jj >= ii)))
    ct_ref[...] = jnp.zeros(ct_ref.shape, F32)

    def split2(x):
        hi = x.astype(BF16)
        return hi, (x - hi.astype(F32)).astype(BF16)

    def step(t, carry):
        chunks = _chunk_ids(t, nct, nch)
        rows = [pl.ds(pl.multiple_of(c * L, L), L) for c in chunks]
        chains = [(d, h) for d in range(2) for h in range(H)]
        q, kt, vext, ct, sq, qc = {}, {}, {}, {}, {}, {}
        for d, h in chains:
            q[d, h] = q_ref[rows[d], _head_cols(h)]
            kt[d, h] = kt_ref[chunks[d], h]
            vext[d, h] = jnp.concatenate([v_ref[rows[d], _head_cols(h)], ones_col], axis=1)
            ct[d, h] = ct_ref[d, h]
            sq[d, h] = _dot(q[d, h], kt[d, h])
            qc[d, h] = _dot(q[d, h], ct[d, h].astype(BF16))
        gr, bc_all, br_all = [], [], []
        for d in range(2):
            gc = gcol_ref[rows[d], :] + bcol_ref[...]
            gr.append(grow_ref[chunks[d]] + brow_ref[...])
            chi, clo = split2(_logsig(gc))
            rhi, rlo = split2(_logsig(gr[d]))
            bc_all.append(_dot(cum[d], chi) + _dot(cum[d], clo))
            br_all.append(_dot(rhi, cum_t[d]) + _dot(rlo, cum_t[d]))
        w, w_inter, einv, decay, kw, new = {}, {}, {}, {}, {}, []
        for d, h in chains:
            m = carry[d * H + h]
            gi = 2 * d * H + h
            b_row = br_all[d][gi + H:gi + H + 1, :]
            b_col = bc_all[d][:, gi + H:gi + H + 1]
            b_last = b_row[:, L - 1:L] if d == 0 else b_row[:, 0:1]
            u_row = gr[d][gi:gi + 1, :] - b_row
            g_col = jnp.maximum(jnp.max(jnp.where(vis[d], u_row, -jnp.inf), axis=1, keepdims=True), m)
            w[d, h] = jnp.exp(jnp.where(vis[d], u_row - g_col, -jnp.inf)) * scale
            w_inter[d, h] = jnp.exp(m - g_col)
            einv[d, h] = jnp.exp(-(b_col + g_col))
            m_new = b_last + jnp.maximum(m, jnp.max(u_row, axis=1, keepdims=True))
            wk_row = jnp.exp(b_last + u_row - m_new) * scale
            decay[d, h] = jnp.exp(b_last + m - m_new)
            kw[d, h] = (kt[d, h].astype(F32) * wk_row).astype(BF16)
            new.append(m_new)
        for d, h in chains:
            ct_ref[d, h] = decay[d, h] * ct[d, h] + _dot(kw[d, h], vext[d, h])
        sv = {}
        for d, h in chains:
            sv[d, h] = _dot((sq[d, h] * w[d, h]).astype(BF16), vext[d, h])
        for d, h in chains:
            num = sv[d, h][:, :ML_DV] + w_inter[d, h] * qc[d, h][:, :ML_DV]
            den = sv[d, h][:, ML_DV:ML_DV + 1] + w_inter[d, h] * qc[d, h][:, ML_DV:ML_DV + 1]
            hh = num / jnp.maximum(jnp.abs(den), einv[d, h])
            if d == 0:
                hf_ref[rows[d], _head_cols(h)] = hh
            else:
                hb_ref[rows[d], _head_cols(h)] = hh
        return tuple(new)

    zero = jnp.zeros((1, 1), F32)
    lax.fori_loop(0, nch, step, (zero,) * (2 * H))
    for h in range(H):
        y = _rms(hf_ref[:, _head_cols(h)] + hb_ref[:, _head_cols(h)], ng_ref[...])
        out_ref[:, _head_cols(h)] = (y * jax.nn.sigmoid(o_ref[:, _head_cols(h)])).astype(BF16)


def _mlstm(oa, of, grow, gate_b, norm_g, nct, nch):
    b, s, _ = oa.shape
    bcol = jnp.zeros((1, LANE), F32).at[0, :4 * ML_HEADS].set(gate_b)
    brow = gate_b.reshape(4 * ML_HEADS, 1)
    sect = lambda base: (lambda bi: (bi, 0, base // 4))
    return pl.pallas_call(
        functools.partial(_mlstm_kernel, nct=nct, nch=nch),
        out_shape=jax.ShapeDtypeStruct((b, s, HEAD_W), BF16),
        grid=(b,),
        in_specs=[pl.BlockSpec((None, s, HEAD_W), sect(A_MQ)),
                  pl.BlockSpec((None, s, HEAD_W), sect(A_MK)),
                  pl.BlockSpec((None, s, HEAD_W), sect(A_MV)),
                  pl.BlockSpec((None, s, HEAD_W), sect(F_MO)),
                  pl.BlockSpec((None, s, LANE), lambda bi: (bi, 0, F_MG)),
                  pl.BlockSpec((None, nch, 4 * ML_HEADS, CHUNK), lambda bi: (bi, 0, 0, 0)),
                  pl.BlockSpec((1, LANE), lambda bi: (0, 0)),
                  pl.BlockSpec((4 * ML_HEADS, 1), lambda bi: (0, 0)),
                  pl.BlockSpec((1, ML_DV), lambda bi: (0, 0))],
        out_specs=pl.BlockSpec((None, s, HEAD_W), lambda bi: (bi, 0, 0)),
        scratch_shapes=[pltpu.VMEM((s, HEAD_W), F32), pltpu.VMEM((s, HEAD_W), F32),
                        pltpu.VMEM((2, ML_HEADS, ML_DK, 2 * LANE), F32),
                        pltpu.VMEM((nch, ML_HEADS, ML_DK, CHUNK), BF16)],
        compiler_params=_cparams(("parallel",)),
        name="mlstm",
    )(oa, oa, oa, of, of, grow, bcol, brow, norm_g.reshape(1, ML_DV))


def _gla_tables():
    L = CHUNK
    G = np.zeros((2, 7 * L, L), np.float32)
    lvl = np.full((2, L, L), 7, np.int32)
    for d in range(2):
        for p in range(L):
            if d == 0:
                G[d, p, :p + 1] = 1
            else:
                G[d, p, p:] = 1
        for li, s in enumerate((32, 16, 8, 4, 2, 1)):
            for p in range(L):
                base = (p // (2 * s)) * 2 * s
                row = (li + 1) * L + p
                if d == 0:
                    mid = base + s
                    if p >= mid:
                        G[d, row, mid + 1:p + 1] = 1
                    else:
                        G[d, row, p + 1:mid + 1] = 1
                else:
                    mid = base + s - 1
                    if p <= mid:
                        G[d, row, p:mid] = 1
                    else:
                        G[d, row, mid:p] = 1
            blk = np.arange(L) // (2 * s)
            upper = (np.arange(L) % (2 * s)) >= s
            same = blk[:, None] == blk[None, :]
            if d == 0:
                sel = same & upper[:, None] & ~upper[None, :]
            else:
                sel = same & ~upper[:, None] & upper[None, :]
            lvl[d][sel] = li
        lvl[d][np.arange(L), np.arange(L)] = 6
    return G, lvl


GL_PAIRS = GL_HEADS * GL_DK // LANE


def _gla_kernel(q_ref, k_ref, v_ref, g_ref, a_ref, wa_ref, ba_ref, gm_ref, lvl_ref, ng_ref, out_ref,
                acc_ref, la_ref, st_ref, vt_ref, *, nct, nch):
    L = CHUNK
    _transpose_chunks(v_ref, vt_ref, nch)
    for d in range(2):
        pre = jnp.dot(a_ref[...], wa_ref[d], precision=HIGHEST, preferred_element_type=F32) + ba_ref[d]
        la_ref[d] = _logsig(pre) * (1.0 / GL_TAU)
    acc_ref[...] = jnp.zeros(acc_ref.shape, F32)
    st_ref[...] = jnp.zeros(st_ref.shape, F32)
    first = lax.broadcasted_iota(jnp.int32, (L, LANE), 1) < GL_DK
    first2 = lax.broadcasted_iota(jnp.int32, (GL_DV, LANE), 1) < GL_DK

    def split(x):
        return jnp.concatenate([jnp.where(first, x, 0.0), jnp.where(first, 0.0, x)], axis=0).astype(BF16)

    def step(t, carry):
        chunks = _chunk_ids(t, nct, nch)
        rows = [pl.ds(pl.multiple_of(c * L, L), L) for c in chunks]
        chains = [(d, p) for d in range(2) for p in range(GL_PAIRS)]
        xs = []
        for d in range(2):
            lac = la_ref[d, rows[d], :]
            hi = lac.astype(BF16)
            lo = (lac - hi.astype(F32)).astype(BF16)
            xs.append(_dot(gm_ref[d], hi) + _dot(gm_ref[d], lo))
        q, k, cs, tot, st, inter, upd = {}, {}, {}, {}, {}, {}, {}
        for d, p in chains:
            q[d, p] = q_ref[rows[d], _head_cols(p)].astype(F32) * (GL_DK ** -0.5)
            k[d, p] = k_ref[rows[d], _head_cols(p)].astype(F32)
            cs[d, p] = xs[d][0:L, _head_cols(p)]
            tot[d, p] = cs[d, p][L - 1:L] if d == 0 else cs[d, p][0:1]
            st[d, p] = st_ref[d, p]
        for d, p in chains:
            inter[d, p] = _dot_nt(split(q[d, p] * jnp.exp(cs[d, p])), st[d, p].astype(BF16))
            ke = (k[d, p] * jnp.exp(tot[d, p] - cs[d, p])).astype(BF16)
            u = [_dot(vt_ref[chunks[d], 2 * p + hh], ke) for hh in range(2)]
            st_ref[d, p] = st[d, p] * jnp.exp(tot[d, p]) + jnp.where(first2, u[0], u[1])
        amat = {}
        for d, p in chains:
            amat[d, p] = jnp.where(lvl_ref[d] == 6, _dot_nt(split(q[d, p]), k[d, p].astype(BF16)), 0.0)
        for li in range(6):
            for d, p in chains:
                e = jnp.exp(xs[d][(li + 1) * L:(li + 2) * L, _head_cols(p)])
                lev = _dot_nt(split(q[d, p] * e), (k[d, p] * e).astype(BF16))
                amat[d, p] = jnp.where(lvl_ref[d] == li, lev, amat[d, p])
        for d, p in chains:
            a = amat[d, p].astype(BF16)
            for hh in range(2):
                cols = _head_cols(2 * p + hh)
                acc_ref[rows[d], cols] += (_dot(a[hh * L:(hh + 1) * L], v_ref[rows[d], cols])
                                           + inter[d, p][hh * L:(hh + 1) * L])
        return carry

    lax.fori_loop(0, nch, step, 0)
    for h in range(GL_HEADS):
        cols = _head_cols(h)
        g = g_ref[:, cols]
        out_ref[:, cols] = (_rms(acc_ref[:, cols], ng_ref[...]) * (g * jax.nn.sigmoid(g))).astype(BF16)


def _gla(oa, of, w_alpha, b_alpha, norm_g, nct, nch):
    b, s, _ = oa.shape
    gmat, lvl = _gla_tables()
    lvl = np.concatenate([lvl, lvl], axis=1)
    qk_w = GL_HEADS * GL_DK
    wa = jnp.zeros((2, LANE, qk_w), F32)
    for d in range(2):
        wa = wa.at[d, d * GL_RANK:(d + 1) * GL_RANK, :].set(w_alpha[d])
    ba = b_alpha.reshape(2, 1, qk_w)
    full = lambda nd: (lambda bi: (0,) * nd)
    return pl.pallas_call(
        functools.partial(_gla_kernel, nct=nct, nch=nch),
        out_shape=jax.ShapeDtypeStruct((b, s, HEAD_W), BF16),
        grid=(b,),
        in_specs=[pl.BlockSpec((None, s, qk_w), lambda bi: (bi, 0, A_GQ * LANE // qk_w)),
                  pl.BlockSpec((None, s, qk_w), lambda bi: (bi, 0, A_GK * LANE // qk_w)),
                  pl.BlockSpec((None, s, HEAD_W), lambda bi: (bi, 0, A_GV // 4)),
                  pl.BlockSpec((None, s, HEAD_W), lambda bi: (bi, 0, F_GG // 4)),
                  pl.BlockSpec((None, s, LANE), lambda bi: (bi, 0, F_GA)),
                  pl.BlockSpec((2, LANE, qk_w), full(3)),
                  pl.BlockSpec((2, 1, qk_w), full(3)),
                  pl.BlockSpec((2, 7 * CHUNK, CHUNK), full(3)),
                  pl.BlockSpec((2, 2 * CHUNK, CHUNK), full(3)),
                  pl.BlockSpec((1, GL_DV), full(2))],
        out_specs=pl.BlockSpec((None, s, HEAD_W), lambda bi: (bi, 0, 0)),
        scratch_shapes=[pltpu.VMEM((s, HEAD_W), F32), pltpu.VMEM((2, s, qk_w), F32),
                        pltpu.VMEM((2, GL_PAIRS, GL_DV, LANE), F32),
                        pltpu.VMEM((nch, GL_HEADS, GL_DV, CHUNK), BF16)],
        compiler_params=_cparams(("parallel",)),
        name="gla",
    )(oa, oa, oa, of, of, wa, ba, jnp.asarray(gmat, BF16), jnp.asarray(lvl), norm_g.reshape(1, GL_DV))


def _rope_tables(n):
    t = np.arange(n)
    n_freq = DF_HD // 4
    inv = jnp.asarray(ROPE_THETA, F32) ** (-jnp.arange(n_freq, dtype=F32) / n_freq)
    ang_r = jnp.asarray(t // GRID_W, F32)[:, None] * inv
    ang_c = jnp.asarray(t % GRID_W, F32)[:, None] * inv
    ang = jnp.concatenate([ang_r, ang_r, ang_c, ang_c] * 2, axis=1)
    first = (np.arange(LANE) % 32) < 16
    cos, sin = jnp.cos(ang), jnp.sin(ang)
    return cos, jnp.where(first, -sin, 0.0), jnp.where(first, 0.0, sin)


DF_ROW_GROUPS = 4


def _rope(x, cos, sa, sb):
    return x * cos + pltpu.roll(x, LANE - 16, 1) * sa + pltpu.roll(x, 16, 1) * sb


def _diff_kernel(q_ref, k_ref, v_ref, cq_ref, saq_ref, sbq_ref, ck_ref, sak_ref, sbk_ref, lam_ref, ng_ref,
                 out_ref, kr_ref, *, nct, n_ctx, lam_init):
    s = k_ref.shape[0]
    tq = q_ref.shape[0]
    i = pl.program_id(2)

    @pl.when(i == 0)
    def _():
        kr_ref[0:n_ctx, :] = k_ref[0:n_ctx, :].astype(BF16)
        kr_ref[n_ctx:s, :] = _rope(k_ref[n_ctx:s, :], ck_ref[...], sak_ref[...], sbk_ref[...]).astype(BF16)

    lp = lam_ref[...]
    lam = (jnp.exp(jnp.sum(lp[0:1] * lp[1:2], axis=1, keepdims=True))
           - jnp.exp(jnp.sum(lp[2:3] * lp[3:4], axis=1, keepdims=True)) + lam_init)
    lane = lax.broadcasted_iota(jnp.int32, q_ref.shape, 1)

    def attend(qb, nk):
        qb = qb * (DF_HD ** -0.5)
        qs = jnp.concatenate([jnp.where(lane < DF_HD, qb, 0.0), jnp.where(lane >= DF_HD, qb, 0.0)],
                             axis=0).astype(BF16)
        rg = 2 * tq // DF_ROW_GROUPS
        scs = [_dot_nt(qs[g * rg:(g + 1) * rg], kr_ref[0:nk, :]) for g in range(DF_ROW_GROUPS)]
        ovs = []
        for sc in scs:
            p = jnp.exp(sc - jnp.max(sc, axis=1, keepdims=True))
            rl = 1.0 / jnp.sum(p, axis=1, keepdims=True)
            ovs.append((p.astype(BF16), rl))
        ov = jnp.concatenate([_dot(p, v_ref[0:nk, :]) * rl for p, rl in ovs], axis=0)
        o = ov[0:tq] - lam * ov[tq:2 * tq]
        out_ref[...] = (_rms(o, ng_ref[...]) * (1.0 - lam_init)).astype(BF16)

    @pl.when(i < nct)
    def _():
        attend(q_ref[...], n_ctx)

    @pl.when(i >= nct)
    def _():
        attend(_rope(q_ref[...], cq_ref[...], saq_ref[...], sbq_ref[...]), s)


def _diff_attn(oa, of, rope, df_lambda, norm_g, tq, nct, n_ctx, lam_init):
    b, s, _ = oa.shape
    n = s - n_ctx
    cos, sa, sb = rope
    lam_p = jnp.zeros((4, LANE), F32).at[:, :DF_HD].set(df_lambda)
    qblk = lambda bi, h, i: (jnp.maximum(i - nct, 0), 0)
    full = lambda bi, h, i: (0, 0)
    return pl.pallas_call(
        functools.partial(_diff_kernel, nct=nct, n_ctx=n_ctx, lam_init=lam_init),
        out_shape=jax.ShapeDtypeStruct((b, s, DF_HEADS * DF_DV), BF16),
        grid=(b, DF_HEADS, s // tq),
        in_specs=[pl.BlockSpec((None, tq, LANE), lambda bi, h, i: (bi, i, F_DQ + h)),
                  pl.BlockSpec((None, s, LANE), lambda bi, h, i: (bi, 0, F_DK + h)),
                  pl.BlockSpec((None, s, LANE), lambda bi, h, i: (bi, 0, A_DV + h)),
                  pl.BlockSpec((tq, LANE), qblk), pl.BlockSpec((tq, LANE), qblk), pl.BlockSpec((tq, LANE), qblk),
                  pl.BlockSpec((n, LANE), full), pl.BlockSpec((n, LANE), full), pl.BlockSpec((n, LANE), full),
                  pl.BlockSpec((4, LANE), full), pl.BlockSpec((1, DF_DV), full)],
        out_specs=pl.BlockSpec((None, tq, LANE), lambda bi, h, i: (bi, i, h)),
        scratch_shapes=[pltpu.VMEM((s, LANE), BF16)],
        compiler_params=_cparams(("parallel", "parallel", "arbitrary")),
        name="diff_attn",
    )(of, of, oa, cos, sa, sb, cos, sa, sb, lam_p, norm_g.reshape(1, DF_DV))


def _merge_kernel(ml_ref, df_ref, gl_ref, g0_ref, g1_ref, g2_ref, wb_ref, wo_ref, x_ref, mod_ref, nf_ref,
                  wr_ref, br_ref, tri_ref, xo_ref, h2_ref, rid_ref, rw_ref, cnt_ref):
    y = (jax.nn.sigmoid(g0_ref[...]) * _dot(ml_ref[...], wb_ref[0])
         + jax.nn.sigmoid(g1_ref[...]) * _dot(df_ref[...], wb_ref[1])
         + jax.nn.sigmoid(g2_ref[...]) * _dot(gl_ref[...], wb_ref[2]))
    xn = x_ref[...] + mod_ref[2:3, :] * _dot(y.astype(BF16), wo_ref[...])
    xo_ref[...] = xn
    h2 = _rms(xn, nf_ref[...]) * (1.0 + mod_ref[4:5, :]) + mod_ref[3:4, :]
    h2_hi = h2.astype(BF16)
    h2_ref[...] = h2_hi
    h2_lo = (h2 - h2_hi.astype(F32)).astype(BF16)
    logits = (_dot(h2_hi, wr_ref[0]) + (_dot(h2_lo, wr_ref[0]) + _dot(h2_hi, wr_ref[1]))) + br_ref[...]
    lane = lax.broadcasted_iota(jnp.int32, logits.shape, 1)
    lane_f = lane.astype(F32)
    neg = -jnp.inf

    def first_max(vals):
        mx = jnp.max(vals, axis=1, keepdims=True)
        return mx, jnp.min(jnp.where(vals == mx, lane_f, float(LANE)), axis=1, keepdims=True)

    is_grp = lane < N_GROUPS
    gmax, gidx = first_max(jnp.where(is_grp, logits, neg))
    pg_top = 1.0 / jnp.sum(jnp.where(is_grp, jnp.exp(logits - gmax), 0.0), axis=1, keepdims=True)
    lo = N_GROUPS + gidx * EXPERTS_PER_GROUP
    in_grp = (lane_f >= lo) & (lane_f < lo + EXPERTS_PER_GROUP)
    le = jnp.where(in_grp, logits, neg)
    m1, e1 = first_max(le)
    m2, e2 = first_max(jnp.where(lane_f == e1, neg, le))
    r = jnp.exp(m2 - m1)
    w1 = pg_top / (1.0 + r)
    w2 = pg_top * r / (1.0 + r)

    @pl.when((pl.program_id(0) == 0) & (pl.program_id(1) == 0))
    def _():
        cnt_ref[...] = jnp.zeros(cnt_ref.shape, F32)

    hot1 = jnp.where(lane_f == e1, 1.0, 0.0)
    hot2 = jnp.where(lane_f == e2, 1.0, 0.0)
    tot1 = jnp.sum(hot1, axis=0, keepdims=True)
    tot2 = jnp.sum(hot2, axis=0, keepdims=True)
    cnt = cnt_ref[...]
    before1 = cnt + _dot(tri_ref[...], hot1.astype(BF16))
    before2 = cnt + tot1 + _dot(tri_ref[...], hot2.astype(BF16))
    rank1 = jnp.sum(hot1 * before1, axis=1, keepdims=True)
    rank2 = jnp.sum(hot2 * before2, axis=1, keepdims=True)
    cnt_ref[...] = cnt + tot1 + tot2
    rid = jnp.where(lane == 0, e1 - N_GROUPS, jnp.where(lane == 1, e2 - N_GROUPS,
                    jnp.where(lane == 2, rank1, jnp.where(lane == 3, rank2, 0.0))))
    rid_ref[...] = rid.astype(jnp.int32)
    rw_ref[...] = jnp.where(lane == 0, w1, jnp.where(lane == 1, w2, 0.0))


def _merge(ml, df, gl, of, wb, wo, xs, modtab, nf, wr, br, tm, nct, first_block):
    b, s, d = xs.shape
    nb = s // tm - first_block
    so = nb * tm
    kind = lambda i: jnp.where(i + first_block >= nct, 1, 0)
    row = lambda bi, i: (bi, i + first_block, 0)
    gate = lambda br_: (lambda bi, i: (bi, i + first_block, F_GT // (d // LANE) + br_))
    outrow = lambda bi, i: (bi, i, 0)
    tri = jnp.asarray(np.tril(np.ones((tm, tm), np.float32), -1), BF16)
    return pl.pallas_call(
        _merge_kernel,
        out_shape=(jax.ShapeDtypeStruct((b, so, d), F32), jax.ShapeDtypeStruct((b, so, d), BF16),
                   jax.ShapeDtypeStruct((b, so, LANE), jnp.int32), jax.ShapeDtypeStruct((b, so, LANE), F32),
                   jax.ShapeDtypeStruct((1, LANE), F32)),
        grid=(b, nb),
        in_specs=[pl.BlockSpec((None, tm, BRANCH_W), row), pl.BlockSpec((None, tm, BRANCH_W), row),
                  pl.BlockSpec((None, tm, BRANCH_W), row),
                  pl.BlockSpec((None, tm, d), gate(0)), pl.BlockSpec((None, tm, d), gate(1)),
                  pl.BlockSpec((None, tm, d), gate(2)),
                  _const_spec((N_BRANCH, BRANCH_W, d)), _const_spec((d, d)),
                  pl.BlockSpec((None, tm, d), row),
                  pl.BlockSpec((None, None, SUBLANE, d), lambda bi, i: (bi, kind(i), 0, 0)),
                  _const_spec((1, d)), _const_spec((2, d, LANE)), _const_spec((1, LANE)), _const_spec((tm, tm))],
        out_specs=(pl.BlockSpec((None, tm, d), outrow), pl.BlockSpec((None, tm, d), outrow),
                   pl.BlockSpec((None, tm, LANE), outrow), pl.BlockSpec((None, tm, LANE), outrow),
                   pl.BlockSpec((1, LANE), lambda bi, i: (0, 0))),
        compiler_params=_cparams(("arbitrary", "arbitrary")),
        name="merge_route",
    )(ml, df, gl, of, of, of, wb, wo, xs, modtab, nf.reshape(1, d), wr, br, tri)


def _gmm_kernel(te_ref, tv_ref, x_ref, wg_ref, wu_ref, wd_ref, y_ref, wgb_ref, wub_ref, wdb_ref):
    i = pl.program_id(0)

    @pl.when((i == 0) | (te_ref[i] != te_ref[jnp.maximum(i - 1, 0)]))
    def _():
        wgb_ref[...] = wg_ref[...].astype(BF16)
        wub_ref[...] = wu_ref[...].astype(BF16)
        wdb_ref[...] = wd_ref[...].astype(BF16)

    @pl.when(tv_ref[i] > 0)
    def _():
        x = x_ref[...]
        a = _dot(x, wgb_ref[...])
        hid = (a * jax.nn.sigmoid(a)) * _dot(x, wub_ref[...])
        y_ref[...] = _dot(hid.astype(BF16), wdb_ref[...]).astype(y_ref.dtype)

    @pl.when(tv_ref[i] == 0)
    def _():
        y_ref[...] = jnp.zeros(y_ref.shape, y_ref.dtype)


def _gmm(tile_expert, tile_valid, xs, wg, wu, wd, li):
    npad, d = xs.shape
    tm = MOE_TILE
    de = wg.shape[3]
    return pl.pallas_call(
        _gmm_kernel,
        out_shape=jax.ShapeDtypeStruct((npad, d), BF16),
        grid_spec=pltpu.PrefetchScalarGridSpec(
            num_scalar_prefetch=2,
            grid=(npad // tm,),
            in_specs=[pl.BlockSpec((tm, d), lambda i, te, tv: (i, 0)),
                      pl.BlockSpec((None, None, d, de), lambda i, te, tv: (li, te[i], 0, 0)),
                      pl.BlockSpec((None, None, d, de), lambda i, te, tv: (li, te[i], 0, 0)),
                      pl.BlockSpec((None, None, de, d), lambda i, te, tv: (li, te[i], 0, 0))],
            out_specs=pl.BlockSpec((tm, d), lambda i, te, tv: (i, 0)),
            scratch_shapes=[pltpu.VMEM((d, de), BF16), pltpu.VMEM((d, de), BF16), pltpu.VMEM((de, d), BF16)]),
        compiler_params=_cparams(("arbitrary",)),
        name="moe_gmm",
    )(tile_expert, tile_valid, xs, wg, wu, wd)


def _route_plan(rid, cnt):
    t = rid.shape[0]
    tm = MOE_TILE
    n_tiles = (TOP_K * t + tm - 1) // tm + N_EXPERTS
    counts = cnt[0, N_GROUPS:N_GROUPS + N_EXPERTS].astype(jnp.int32)
    tiles_e = (counts + tm - 1) // tm
    tile_end = jnp.cumsum(tiles_e)
    pad_start = (tile_end - tiles_e) * tm
    eid, rank = rid[:, 0:TOP_K], rid[:, TOP_K:2 * TOP_K]
    pos = rank + jnp.sum(jnp.where(eid[:, :, None] == jnp.arange(N_EXPERTS), pad_start, 0), axis=-1)
    tile = jnp.arange(n_tiles, dtype=jnp.int32)
    tile_expert = jnp.minimum(jnp.sum(tile[:, None] >= tile_end[None, :], axis=1), N_EXPERTS - 1).astype(jnp.int32)
    tile_valid = (tile < tile_end[-1]).astype(jnp.int32)
    return tile_expert, tile_valid, pos.astype(jnp.int32), n_tiles * tm


def _combine_kernel(x_ref, y0_ref, y1_ref, rw_ref, mod_ref, fg_ref, o_ref, *, final):
    rw = rw_ref[...]
    y = rw[:, 0:1] * y0_ref[...].astype(F32) + rw[:, 1:2] * y1_ref[...].astype(F32)
    xn = x_ref[...] + mod_ref[5:6, :] * y
    o_ref[...] = _rms(xn, fg_ref[...]) if final else xn


def _combine(xs, y0, y1, rw, modtab, fg, tm, nct, first_block, final):
    b, s, d = xs.shape
    kind = lambda i: jnp.where(i + first_block >= nct, 1, 0)
    row = lambda bi, i: (bi, i, 0)
    return pl.pallas_call(
        functools.partial(_combine_kernel, final=final),
        out_shape=jax.ShapeDtypeStruct((b, s, d), F32),
        grid=(b, s // tm),
        in_specs=[pl.BlockSpec((None, tm, d), row), pl.BlockSpec((None, tm, d), row),
                  pl.BlockSpec((None, tm, d), row), pl.BlockSpec((None, tm, LANE), row),
                  pl.BlockSpec((None, None, SUBLANE, d), lambda bi, i: (bi, kind(i), 0, 0)),
                  _const_spec((1, d))],
        out_specs=pl.BlockSpec((None, tm, d), row),
        compiler_params=_cparams(("parallel", "parallel")),
        name="moe_combine",
    )(xs, y0, y1, rw, modtab, fg.reshape(1, d))


def _pack_w_in(w):
    d = w.shape[0]
    mq, mk, mv, mo, mg, dq, dk, dv, gq, gk, gv, gg, ga, gt = jnp.split(
        w, [int(i) for i in np.cumsum(IN_SIZES)[:-1]], axis=1)

    def pad_cols(t):
        return jnp.pad(t, ((0, 0), (0, LANE - t.shape[1])))

    wa = jnp.concatenate([mq, mk, mv, gq, gk, gv, dv], axis=1).astype(BF16)
    wf = jnp.concatenate([dq, dk, mo, gg, gt, pad_cols(mg), pad_cols(ga)], axis=1).astype(BF16)
    return wa, wf


def _dispatch_rows(h2, pos, npad):
    t = h2.shape[0]
    tok = jnp.arange(t, dtype=jnp.int32)
    src = jnp.zeros((npad,), jnp.int32).at[pos[:, 0]].set(tok).at[pos[:, 1]].set(tok)
    return jnp.take(h2, src, axis=0)


def kernel(x, c, ctx, c_ctx, w_mod, b_mod, norm_mix_g, norm_ffn_g, w_in, ml_gate_b, ml_norm_g, df_lambda,
           df_norm_g, gl_w_alpha, gl_b_alpha, gl_norm_g, w_branch, w_out, router_group_w, router_group_b,
           router_expert_w, router_expert_b, moe_w_gate, moe_w_up, moe_w_down, final_norm_g):
    b, n, d = x.shape
    n_ctx = ctx.shape[1]
    s = n_ctx + n
    depth = w_mod.shape[0]
    tm = 256 if n_ctx % 256 == 0 else 128
    assert n_ctx % tm == 0 and n % tm == 0 and n % GRID_W == 0 and n_ctx % CHUNK == 0 and d == D_MODEL
    nct = n_ctx // tm
    nch, nch_ctx = s // CHUNK, n_ctx // CHUNK
    rope = _rope_tables(n)
    xs = jnp.concatenate([ctx, x], axis=1)
    mod_rows = -(-(b + 1) // SUBLANE) * SUBLANE
    cc = jnp.zeros((mod_rows, d), F32).at[:b].set(c).at[b].set(c_ctx)
    for li in range(depth):
        last = li == depth - 1
        lam_init = 0.8 - 0.6 * math.exp(-0.3 * li)
        mod = _modulation(cc, w_mod, b_mod[li], li).reshape(mod_rows, 6, d)
        mod = jnp.pad(mod, ((0, 0), (0, SUBLANE - 6), (0, 0)))
        modtab = jnp.stack([jnp.broadcast_to(mod[b], (b, SUBLANE, d)), mod[:b]], axis=1)
        wa, wf = _pack_w_in(w_in[li])
        oa, of = _inproj(xs, modtab, norm_mix_g[li], wa, wf, tm, nct)
        grow = of[:, :, F_MG * LANE:F_MG * LANE + 4 * ML_HEADS]
        grow = grow.reshape(b, nch, CHUNK, 4 * ML_HEADS).transpose(0, 1, 3, 2)
        ml = _mlstm(oa, of, grow, ml_gate_b[li], ml_norm_g[li], nch_ctx, nch)
        df = _diff_attn(oa, of, rope, df_lambda[li], df_norm_g[li], tm, nct, n_ctx, lam_init)
        gl = _gla(oa, of, gl_w_alpha[li], gl_b_alpha[li], gl_norm_g[li], nch_ctx, nch)
        wr = jnp.zeros((d, LANE), F32).at[:, :N_GROUPS].set(router_group_w[li])
        wr = wr.at[:, N_GROUPS:N_GROUPS + N_EXPERTS].set(router_expert_w[li])
        wr_hi = wr.astype(BF16)
        wr = jnp.stack([wr_hi, (wr - wr_hi.astype(F32)).astype(BF16)])
        br = jnp.zeros((1, LANE), F32).at[0, :N_GROUPS].set(router_group_b[li])
        br = br.at[0, N_GROUPS:N_GROUPS + N_EXPERTS].set(router_expert_b[li])
        first_block = nct if last else 0
        xn, h2, rid, rw, cnt = _merge(ml, df, gl, of, w_branch[li].astype(BF16), w_out[li].astype(BF16), xs,
                                      modtab, norm_ffn_g[li], wr, br, tm, nct, first_block)
        so = xn.shape[1]
        t = b * so
        te, tv, pos, npad = _route_plan(rid.reshape(t, LANE), cnt)
        xsorted = _dispatch_rows(h2.reshape(t, d), pos, npad)
        ys = _gmm(te, tv, xsorted, moe_w_gate, moe_w_up, moe_w_down, li)
        y0 = jnp.take(ys, pos[:, 0], axis=0).reshape(b, so, d)
        y1 = jnp.take(ys, pos[:, 1], axis=0).reshape(b, so, d)
        xs = _combine(xn, y0, y1, rw, modtab, final_norm_g, tm, nct, first_block, last)
    return xs
```

```python
import functools
import math

import numpy as np
import jax
import jax.numpy as jnp
from jax import lax
from jax.experimental import pallas as pl
from jax.experimental.pallas import tpu as pltpu

F32 = jnp.float32
BF16 = jnp.bfloat16
HIGHEST = lax.Precision.HIGHEST

D_MODEL = 1024
GRID_W = 64
NORM_EPS = 1e-6
CHUNK = 64
ROPE_THETA = 10000.0
ML_HEADS, ML_DK, ML_DV = 4, 128, 128
DF_HEADS, DF_HD, DF_DV = 4, 64, 128
GL_HEADS, GL_DK, GL_DV, GL_RANK, GL_TAU = 4, 64, 128, 16, 16.0
N_BRANCH, BRANCH_W = 3, 512
N_GROUPS, EXPERTS_PER_GROUP, TOP_K, D_EXPERT = 4, 8, 2, 512
N_EXPERTS = N_GROUPS * EXPERTS_PER_GROUP
IN_SIZES = (
    ML_HEADS * ML_DK, ML_HEADS * ML_DK, ML_HEADS * ML_DV, ML_HEADS * ML_DV, 4 * ML_HEADS,
    DF_HEADS * 2 * DF_HD, DF_HEADS * 2 * DF_HD, DF_HEADS * DF_DV,
    GL_HEADS * GL_DK, GL_HEADS * GL_DK, GL_HEADS * GL_DV, GL_HEADS * GL_DV, 2 * GL_RANK,
    N_BRANCH * D_MODEL,
)

LANE = 128
SUBLANE = 8
VMEM_LIMIT = 56 * 1024 * 1024
MOE_GROUPS = 2
MOE_TILE = 512

A_MQ, A_MK, A_MV, A_GQ, A_GK, A_GV, A_DV = 0, 4, 8, 12, 14, 16, 20
A_WIDTH = 24 * LANE
F_DQ, F_DK, F_MO, F_GG, F_GT, F_MG, F_GA = 0, 4, 8, 12, 16, 40, 41
F_WIDTH = 42 * LANE
HEAD_W = 4 * LANE


def _cparams(sem):
    return pltpu.CompilerParams(dimension_semantics=sem, vmem_limit_bytes=VMEM_LIMIT)


def _const_spec(shape):
    nd = len(shape)
    return pl.BlockSpec(shape, lambda *_: (0,) * nd, pipeline_mode=pl.Buffered(1))


def _logsig(x):
    return jnp.minimum(x, 0.0) - jnp.log1p(jnp.exp(-jnp.abs(x)))


def _dot(a, b):
    return jnp.dot(a, b, preferred_element_type=F32)


def _dot_nt(a, b):
    return lax.dot_general(a, b, (((1,), (1,)), ((), ())), preferred_element_type=F32)


def _dot_tn(a, b):
    return lax.dot_general(a, b, (((0,), (0,)), ((), ())), preferred_element_type=F32)


def _rms(x, g):
    return x * lax.rsqrt(jnp.mean(x * x, axis=-1, keepdims=True) + NORM_EPS) * g


def _mod_kernel(c_ref, w_ref, b_ref, o_ref):
    c = c_ref[...]
    s = c * jax.nn.sigmoid(c)
    o_ref[...] = jnp.dot(s, w_ref[...], precision=HIGHEST, preferred_element_type=F32) + b_ref[...]


def _modulation(cc, w_mod, b_mod, li):
    rows, d = cc.shape
    n = w_mod.shape[2]
    tn = 512
    return pl.pallas_call(
        _mod_kernel,
        out_shape=jax.ShapeDtypeStruct((rows, n), F32),
        grid=(n // tn,),
        in_specs=[pl.BlockSpec((rows, d), lambda j: (0, 0)),
                  pl.BlockSpec((None, d, tn), lambda j: (li, 0, j)),
                  pl.BlockSpec((1, tn), lambda j: (0, j))],
        out_specs=pl.BlockSpec((rows, tn), lambda j: (0, j)),
        compiler_params=_cparams(("parallel",)),
        name="adaln_mod",
    )(cc, w_mod, b_mod.reshape(1, n))


def _inproj_kernel(x_ref, mod_ref, g_ref, wa_ref, wf_ref, oa_ref, of_ref):
    y = _rms(x_ref[...], g_ref[...])
    h = (y * (1.0 + mod_ref[1:2, :]) + mod_ref[0:1, :]).astype(BF16)
    oa_ref[...] = _dot(h, wa_ref[...]).astype(BF16)
    of_ref[...] = _dot(h, wf_ref[...])


def _inproj(xs, modtab, g, wa, wf, tm, nct):
    b, s, d = xs.shape
    kind = lambda i: jnp.where(i >= nct, 1, 0)
    return pl.pallas_call(
        _inproj_kernel,
        out_shape=(jax.ShapeDtypeStruct((b, s, A_WIDTH), BF16), jax.ShapeDtypeStruct((b, s, F_WIDTH), F32)),
        grid=(b, s // tm),
        in_specs=[pl.BlockSpec((None, tm, d), lambda bi, i: (bi, i, 0)),
                  pl.BlockSpec((None, None, SUBLANE, d), lambda bi, i: (bi, kind(i), 0, 0)),
                  _const_spec((1, d)), _const_spec((d, A_WIDTH)), _const_spec((d, F_WIDTH))],
        out_specs=(pl.BlockSpec((None, tm, A_WIDTH), lambda bi, i: (bi, i, 0)),
                   pl.BlockSpec((None, tm, F_WIDTH), lambda bi, i: (bi, i, 0))),
        compiler_params=_cparams(("parallel", "parallel")),
        name="in_proj",
    )(xs, modtab, g.reshape(1, d), wa, wf)


def _chunk_ids(t, nct, nch):
    return t, jnp.where(t < nct, nct - 1 - t, nch - 1 - (t - nct))


def _head_cols(h):
    return slice(h * LANE, (h + 1) * LANE)


def _transpose_chunks(src_ref, dst_ref, nch):
    eye = (lax.broadcasted_iota(jnp.int32, (LANE, LANE), 0)
           == lax.broadcasted_iota(jnp.int32, (LANE, LANE), 1)).astype(BF16)

    def body(c, carry):
        blk = src_ref[pl.ds(pl.multiple_of(c * CHUNK, CHUNK), CHUNK), :]
        for h in range(HEAD_W // LANE):
            dst_ref[c, h] = _dot_nt(eye, blk[:, _head_cols(h)]).astype(BF16)
        return carry

    lax.fori_loop(0, nch, body, 0, unroll=2)


def _mlstm_kernel(q_ref, k_ref, v_ref, o_ref, gcol_ref, grow_ref, bcol_ref, brow_ref, ng_ref, out_ref,
                  hf_ref, hb_ref, ct_ref, kt_ref, *, nct, nch):
    L = CHUNK
    H = ML_HEADS
    _transpose_chunks(k_ref, kt_ref, nch)
    jj = lax.broadcasted_iota(jnp.int32, (L, L), 0)
    ii = lax.broadcasted_iota(jnp.int32, (L, L), 1)
    lane = lax.broadcasted_iota(jnp.int32, (L, LANE), 1)
    ones_col = jnp.where(lane == 0, 1.0, 0.0).astype(BF16)
    scale = ML_DK ** -0.5
    vis = ((ii <= jj), (ii >= jj))
    cum = tuple(jnp.where(m_, 1.0, 0.0).astype(BF16) for m_ in vis)
    cum_t = tuple(jnp.where(m_, 1.0, 0.0).astype(BF16) for m_ in ((jj <= ii), (jj >= ii)))
    ct_ref[...] = jnp.zeros(ct_ref.shape, F32)

    def split2(x):
        hi = x.astype(BF16)
        return hi, (x - hi.astype(F32)).astype(BF16)

    def step(t, carry):
        chunks = _chunk_ids(t, nct, nch)
        rows = [pl.ds(pl.multiple_of(c * L, L), L) for c in chunks]
        chains = [(d, h) for d in range(2) for h in range(H)]
        q, kt, vext, ct, sq, qc = {}, {}, {}, {}, {}, {}
        for d, h in chains:
            q[d, h] = q_ref[rows[d], _head_cols(h)]
            kt[d, h] = kt_ref[chunks[d], h]
            vext[d, h] = jnp.concatenate([v_ref[rows[d], _head_cols(h)], ones_col], axis=1)
            ct[d, h] = ct_ref[d, h]
            sq[d, h] = _dot(q[d, h], kt[d, h])
            qc[d, h] = _dot(q[d, h], ct[d, h].astype(BF16))
        gr, bc_all, br_all = [], [], []
        for d in range(2):
            gc = gcol_ref[rows[d], :] + bcol_ref[...]
            gr.append(grow_ref[chunks[d]] + brow_ref[...])
            chi, clo = split2(_logsig(gc))
            rhi, rlo = split2(_logsig(gr[d]))
            bc_all.append(_dot(cum[d], chi) + _dot(cum[d], clo))
            br_all.append(_dot(rhi, cum_t[d]) + _dot(rlo, cum_t[d]))
        w, w_inter, einv, decay, kw, new = {}, {}, {}, {}, {}, []
        for d, h in chains:
            m = carry[d * H + h]
            gi = 2 * d * H + h
            b_row = br_all[d][gi + H:gi + H + 1, :]
            b_col = bc_all[d][:, gi + H:gi + H + 1]
            b_last = b_row[:, L - 1:L] if d == 0 else b_row[:, 0:1]
            u_row = gr[d][gi:gi + 1, :] - b_row
            g_col = jnp.maximum(jnp.max(jnp.where(vis[d], u_row, -jnp.inf), axis=1, keepdims=True), m)
            w[d, h] = jnp.exp(jnp.where(vis[d], u_row - g_col, -jnp.inf)) * scale
            w_inter[d, h] = jnp.exp(m - g_col)
            einv[d, h] = jnp.exp(-(b_col + g_col))
            m_new = b_last + jnp.maximum(m, jnp.max(u_row, axis=1, keepdims=True))
            wk_row = jnp.exp(b_last + u_row - m_new) * scale
            decay[d, h] = jnp.exp(b_last + m - m_new)
            kw[d, h] = (kt[d, h].astype(F32) * wk_row).astype(BF16)
            new.append(m_new)
        for d, h in chains:
            ct_ref[d, h] = decay[d, h] * ct[d, h] + _dot(kw[d, h], vext[d, h])
        sv = {}
        for d, h in chains:
            sv[d, h] = _dot((sq[d, h] * w[d, h]).astype(BF16), vext[d, h])
        for d, h in chains:
            num = sv[d, h][:, :ML_DV] + w_inter[d, h] * qc[d, h][:, :ML_DV]
            den = sv[d, h][:, ML_DV:ML_DV + 1] + w_inter[d, h] * qc[d, h][:, ML_DV:ML_DV + 1]
            hh = num / jnp.maximum(jnp.abs(den), einv[d, h])
            if d == 0:
                hf_ref[rows[d], _head_cols(h)] = hh
            else:
                hb_ref[rows[d], _head_cols(h)] = hh
        return tuple(new)

    zero = jnp.zeros((1, 1), F32)
    lax.fori_loop(0, nch, step, (zero,) * (2 * H))
    for h in range(H):
        y = _rms(hf_ref[:, _head_cols(h)] + hb_ref[:, _head_cols(h)], ng_ref[...])
        out_ref[:, _head_cols(h)] = (y * jax.nn.sigmoid(o_ref[:, _head_cols(h)])).astype(BF16)


def _mlstm(oa, of, grow, gate_b, norm_g, nct, nch):
    b, s, _ = oa.shape
    bcol = jnp.zeros((1, LANE), F32).at[0, :4 * ML_HEADS].set(gate_b)
    brow = gate_b.reshape(4 * ML_HEADS, 1)
    sect = lambda base: (lambda bi: (bi, 0, base // 4))
    return pl.pallas_call(
        functools.partial(_mlstm_kernel, nct=nct, nch=nch),
        out_shape=jax.ShapeDtypeStruct((b, s, HEAD_W), BF16),
        grid=(b,),
        in_specs=[pl.BlockSpec((None, s, HEAD_W), sect(A_MQ)),
                  pl.BlockSpec((None, s, HEAD_W), sect(A_MK)),
                  pl.BlockSpec((None, s, HEAD_W), sect(A_MV)),
                  pl.BlockSpec((None, s, HEAD_W), sect(F_MO)),
                  pl.BlockSpec((None, s, LANE), lambda bi: (bi, 0, F_MG)),
                  pl.BlockSpec((None, nch, 4 * ML_HEADS, CHUNK), lambda bi: (bi, 0, 0, 0)),
                  pl.BlockSpec((1, LANE), lambda bi: (0, 0)),
                  pl.BlockSpec((4 * ML_HEADS, 1), lambda bi: (0, 0)),
                  pl.BlockSpec((1, ML_DV), lambda bi: (0, 0))],
        out_specs=pl.BlockSpec((None, s, HEAD_W), lambda bi: (bi, 0, 0)),
        scratch_shapes=[pltpu.VMEM((s, HEAD_W), F32), pltpu.VMEM((s, HEAD_W), F32),
                        pltpu.VMEM((2, ML_HEADS, ML_DK, 2 * LANE), F32),
                        pltpu.VMEM((nch, ML_HEADS, ML_DK, CHUNK), BF16)],
        compiler_params=_cparams(("parallel",)),
        name="mlstm",
    )(oa, oa, oa, of, of, grow, bcol, brow, norm_g.reshape(1, ML_DV))


def _gla_tables():
    L = CHUNK
    G = np.zeros((2, 7 * L, L), np.float32)
    lvl = np.full((2, L, L), 7, np.int32)
    for d in range(2):
        for p in range(L):
            if d == 0:
                G[d, p, :p + 1] = 1
            else:
                G[d, p, p:] = 1
        for li, s in enumerate((32, 16, 8, 4, 2, 1)):
            for p in range(L):
                base = (p // (2 * s)) * 2 * s
                row = (li + 1) * L + p
                if d == 0:
                    mid = base + s
                    if p >= mid:
                        G[d, row, mid + 1:p + 1] = 1
                    else:
                        G[d, row, p + 1:mid + 1] = 1
                else:
                    mid = base + s - 1
                    if p <= mid:
                        G[d, row, p:mid] = 1
                    else:
                        G[d, row, mid:p] = 1
            blk = np.arange(L) // (2 * s)
            upper = (np.arange(L) % (2 * s)) >= s
            same = blk[:, None] == blk[None, :]
            if d == 0:
                sel = same & upper[:, None] & ~upper[None, :]
            else:
                sel = same & ~upper[:, None] & upper[None, :]
            lvl[d][sel] = li
        lvl[d][np.arange(L), np.arange(L)] = 6
    return G, lvl


GL_PAIRS = GL_HEADS * GL_DK // LANE


def _gla_kernel(q_ref, k_ref, v_ref, g_ref, a_ref, wa_ref, ba_ref, gm_ref, lvl_ref, ng_ref, out_ref,
                acc_ref, la_ref, st_ref, vt_ref, *, nct, nch):
    L = CHUNK
    _transpose_chunks(v_ref, vt_ref, nch)
    for d in range(2):
        pre = jnp.dot(a_ref[...], wa_ref[d], precision=HIGHEST, preferred_element_type=F32) + ba_ref[d]
        la_ref[d] = _logsig(pre) * (1.0 / GL_TAU)
    acc_ref[...] = jnp.zeros(acc_ref.shape, F32)
    st_ref[...] = jnp.zeros(st_ref.shape, F32)
    first = lax.broadcasted_iota(jnp.int32, (L, LANE), 1) < GL_DK
    first2 = lax.broadcasted_iota(jnp.int32, (GL_DV, LANE), 1) < GL_DK

    def split(x):
        return jnp.concatenate([jnp.where(first, x, 0.0), jnp.where(first, 0.0, x)], axis=0).astype(BF16)

    def step(t, carry):
        chunks = _chunk_ids(t, nct, nch)
        rows = [pl.ds(pl.multiple_of(c * L, L), L) for c in chunks]
        chains = [(d, p) for d in range(2) for p in range(GL_PAIRS)]
        xs = []
        for d in range(2):
            lac = la_ref[d, rows[d], :]
            hi = lac.astype(BF16)
            lo = (lac - hi.astype(F32)).astype(BF16)
            xs.append(_dot(gm_ref[d], hi) + _dot(gm_ref[d], lo))
        q, k, cs, tot, st, inter, upd = {}, {}, {}, {}, {}, {}, {}
        for d, p in chains:
            q[d, p] = q_ref[rows[d], _head_cols(p)].astype(F32) * (GL_DK ** -0.5)
            k[d, p] = k_ref[rows[d], _head_cols(p)].astype(F32)
            cs[d, p] = xs[d][0:L, _head_cols(p)]
            tot[d, p] = cs[d, p][L - 1:L] if d == 0 else cs[d, p][0:1]
            st[d, p] = st_ref[d, p]
        for d, p in chains:
            inter[d, p] = _dot_nt(split(q[d, p] * jnp.exp(cs[d, p])), st[d, p].astype(BF16))
            ke = (k[d, p] * jnp.exp(tot[d, p] - cs[d, p])).astype(BF16)
            u = [_dot(vt_ref[chunks[d], 2 * p + hh], ke) for hh in range(2)]
            st_ref[d, p] = st[d, p] * jnp.exp(tot[d, p]) + jnp.where(first2, u[0], u[1])
        amat = {}
        for d, p in chains:
            amat[d, p] = jnp.where(lvl_ref[d] == 6, _dot_nt(split(q[d, p]), k[d, p].astype(BF16)), 0.0)
        for li in range(6):
            for d, p in chains:
                e = jnp.exp(xs[d][(li + 1) * L:(li + 2) * L, _head_cols(p)])
                lev = _dot_nt(split(q[d, p] * e), (k[d, p] * e).astype(BF16))
                amat[d, p] = jnp.where(lvl_ref[d] == li, lev, amat[d, p])
        for d, p in chains:
            a = amat[d, p].astype(BF16)
            for hh in range(2):
                cols = _head_cols(2 * p + hh)
                acc_ref[rows[d], cols] += (_dot(a[hh * L:(hh + 1) * L], v_ref[rows[d], cols])
                                           + inter[d, p][hh * L:(hh + 1) * L])
        return carry

    lax.fori_loop(0, nch, step, 0)
    for h in range(GL_HEADS):
        cols = _head_cols(h)
        g = g_ref[:, cols]
        out_ref[:, cols] = (_rms(acc_ref[:, cols], ng_ref[...]) * (g * jax.nn.sigmoid(g))).astype(BF16)


def _gla(oa, of, w_alpha, b_alpha, norm_g, nct, nch):
    b, s, _ = oa.shape
    gmat, lvl = _gla_tables()
    lvl = np.concatenate([lvl, lvl], axis=1)
    qk_w = GL_HEADS * GL_DK
    wa = jnp.zeros((2, LANE, qk_w), F32)
    for d in range(2):
        wa = wa.at[d, d * GL_RANK:(d + 1) * GL_RANK, :].set(w_alpha[d])
    ba = b_alpha.reshape(2, 1, qk_w)
    full = lambda nd: (lambda bi: (0,) * nd)
    return pl.pallas_call(
        functools.partial(_gla_kernel, nct=nct, nch=nch),
        out_shape=jax.ShapeDtypeStruct((b, s, HEAD_W), BF16),
        grid=(b,),
        in_specs=[pl.BlockSpec((None, s, qk_w), lambda bi: (bi, 0, A_GQ * LANE // qk_w)),
                  pl.BlockSpec((None, s, qk_w), lambda bi: (bi, 0, A_GK * LANE // qk_w)),
                  pl.BlockSpec((None, s, HEAD_W), lambda bi: (bi, 0, A_GV // 4)),
                  pl.BlockSpec((None, s, HEAD_W), lambda bi: (bi, 0, F_GG // 4)),
                  pl.BlockSpec((None, s, LANE), lambda bi: (bi, 0, F_GA)),
                  pl.BlockSpec((2, LANE, qk_w), full(3)),
                  pl.BlockSpec((2, 1, qk_w), full(3)),
                  pl.BlockSpec((2, 7 * CHUNK, CHUNK), full(3)),
                  pl.BlockSpec((2, 2 * CHUNK, CHUNK), full(3)),
                  pl.BlockSpec((1, GL_DV), full(2))],
        out_specs=pl.BlockSpec((None, s, HEAD_W), lambda bi: (bi, 0, 0)),
        scratch_shapes=[pltpu.VMEM((s, HEAD_W), F32), pltpu.VMEM((2, s, qk_w), F32),
                        pltpu.VMEM((2, GL_PAIRS, GL_DV, LANE), F32),
                        pltpu.VMEM((nch, GL_HEADS, GL_DV, CHUNK), BF16)],
        compiler_params=_cparams(("parallel",)),
        name="gla",
    )(oa, oa, oa, of, of, wa, ba, jnp.asarray(gmat, BF16), jnp.asarray(lvl), norm_g.reshape(1, GL_DV))


def _rope_tables(n):
    t = np.arange(n)
    n_freq = DF_HD // 4
    inv = jnp.asarray(ROPE_THETA, F32) ** (-jnp.arange(n_freq, dtype=F32) / n_freq)
    ang_r = jnp.asarray(t // GRID_W, F32)[:, None] * inv
    ang_c = jnp.asarray(t % GRID_W, F32)[:, None] * inv
    ang = jnp.concatenate([ang_r, ang_r, ang_c, ang_c] * 2, axis=1)
    first = (np.arange(LANE) % 32) < 16
    cos, sin = jnp.cos(ang), jnp.sin(ang)
    return cos, jnp.where(first, -sin, 0.0), jnp.where(first, 0.0, sin)


DF_ROW_GROUPS = 4


def _rope(x, cos, sa, sb):
    return x * cos + pltpu.roll(x, LANE - 16, 1) * sa + pltpu.roll(x, 16, 1) * sb


def _diff_kernel(q_ref, k_ref, v_ref, cq_ref, saq_ref, sbq_ref, ck_ref, sak_ref, sbk_ref, lam_ref, ng_ref,
                 out_ref, kr_ref, *, nct, n_ctx, lam_init):
    s = k_ref.shape[0]
    tq = q_ref.shape[0]
    i = pl.program_id(2)

    @pl.when(i == 0)
    def _():
        kr_ref[0:n_ctx, :] = k_ref[0:n_ctx, :].astype(BF16)
        kr_ref[n_ctx:s, :] = _rope(k_ref[n_ctx:s, :], ck_ref[...], sak_ref[...], sbk_ref[...]).astype(BF16)

    lp = lam_ref[...]
    lam = (jnp.exp(jnp.sum(lp[0:1] * lp[1:2], axis=1, keepdims=True))
           - jnp.exp(jnp.sum(lp[2:3] * lp[3:4], axis=1, keepdims=True)) + lam_init)
    lane = lax.broadcasted_iota(jnp.int32, q_ref.shape, 1)

    def attend(qb, nk):
        qb = qb * (DF_HD ** -0.5)
        qs = jnp.concatenate([jnp.where(lane < DF_HD, qb, 0.0), jnp.where(lane >= DF_HD, qb, 0.0)],
                             axis=0).astype(BF16)
        rg = 2 * tq // DF_ROW_GROUPS
        scs = [_dot_nt(qs[g * rg:(g + 1) * rg], kr_ref[0:nk, :]) for g in range(DF_ROW_GROUPS)]
        ovs = []
        for sc in scs:
            p = jnp.exp(sc - jnp.max(sc, axis=1, keepdims=True))
            rl = 1.0 / jnp.sum(p, axis=1, keepdims=True)
            ovs.append((p.astype(BF16), rl))
        ov = jnp.concatenate([_dot(p, v_ref[0:nk, :]) * rl for p, rl in ovs], axis=0)
        o = ov[0:tq] - lam * ov[tq:2 * tq]
        out_ref[...] = (_rms(o, ng_ref[...]) * (1.0 - lam_init)).astype(BF16)

    @pl.when(i < nct)
    def _():
        attend(q_ref[...], n_ctx)

    @pl.when(i >= nct)
    def _():
        attend(_rope(q_ref[...], cq_ref[...], saq_ref[...], sbq_ref[...]), s)


def _diff_attn(oa, of, rope, df_lambda, norm_g, tq, nct, n_ctx, lam_init):
    b, s, _ = oa.shape
    n = s - n_ctx
    cos, sa, sb = rope
    lam_p = jnp.zeros((4, LANE), F32).at[:, :DF_HD].set(df_lambda)
    qblk = lambda bi, h, i: (jnp.maximum(i - nct, 0), 0)
    full = lambda bi, h, i: (0, 0)
    return pl.pallas_call(
        functools.partial(_diff_kernel, nct=nct, n_ctx=n_ctx, lam_init=lam_init),
        out_shape=jax.ShapeDtypeStruct((b, s, DF_HEADS * DF_DV), BF16),
        grid=(b, DF_HEADS, s // tq),
        in_specs=[pl.BlockSpec((None, tq, LANE), lambda bi, h, i: (bi, i, F_DQ + h)),
                  pl.BlockSpec((None, s, LANE), lambda bi, h, i: (bi, 0, F_DK + h)),
                  pl.BlockSpec((None, s, LANE), lambda bi, h, i: (bi, 0, A_DV + h)),
                  pl.BlockSpec((tq, LANE), qblk), pl.BlockSpec((tq, LANE), qblk), pl.BlockSpec((tq, LANE), qblk),
                  pl.BlockSpec((n, LANE), full), pl.BlockSpec((n, LANE), full), pl.BlockSpec((n, LANE), full),
                  pl.BlockSpec((4, LANE), full), pl.BlockSpec((1, DF_DV), full)],
        out_specs=pl.BlockSpec((None, tq, LANE), lambda bi, h, i: (bi, i, h)),
        scratch_shapes=[pltpu.VMEM((s, LANE), BF16)],
        compiler_params=_cparams(("parallel", "parallel", "arbitrary")),
        name="diff_attn",
    )(of, of, oa, cos, sa, sb, cos, sa, sb, lam_p, norm_g.reshape(1, DF_DV))


def _merge_kernel(ml_ref, df_ref, gl_ref, g0_ref, g1_ref, g2_ref, wb_ref, wo_ref, x_ref, mod_ref, nf_ref,
                  wr_ref, br_ref, tri_ref, xo_ref, h2_ref, rid_ref, rw_ref, cnt_ref):
    y = (jax.nn.sigmoid(g0_ref[...]) * _dot(ml_ref[...], wb_ref[0])
         + jax.nn.sigmoid(g1_ref[...]) * _dot(df_ref[...], wb_ref[1])
         + jax.nn.sigmoid(g2_ref[...]) * _dot(gl_ref[...], wb_ref[2]))
    xn = x_ref[...] + mod_ref[2:3, :] * _dot(y.astype(BF16), wo_ref[...])
    xo_ref[...] = xn
    h2 = _rms(xn, nf_ref[...]) * (1.0 + mod_ref[4:5, :]) + mod_ref[3:4, :]
    h2_hi = h2.astype(BF16)
    h2_ref[...] = h2_hi
    h2_lo = (h2 - h2_hi.astype(F32)).astype(BF16)
    logits = (_dot(h2_hi, wr_ref[0]) + (_dot(h2_lo, wr_ref[0]) + _dot(h2_hi, wr_ref[1]))) + br_ref[...]
    lane = lax.broadcasted_iota(jnp.int32, logits.shape, 1)
    lane_f = lane.astype(F32)
    neg = -jnp.inf

    def first_max(vals):
        mx = jnp.max(vals, axis=1, keepdims=True)
        return mx, jnp.min(jnp.where(vals == mx, lane_f, float(LANE)), axis=1, keepdims=True)

    is_grp = lane < N_GROUPS
    gmax, gidx = first_max(jnp.where(is_grp, logits, neg))
    pg_top = 1.0 / jnp.sum(jnp.where(is_grp, jnp.exp(logits - gmax), 0.0), axis=1, keepdims=True)
    lo = N_GROUPS + gidx * EXPERTS_PER_GROUP
    in_grp = (lane_f >= lo) & (lane_f < lo + EXPERTS_PER_GROUP)
    le = jnp.where(in_grp, logits, neg)
    m1, e1 = first_max(le)
    m2, e2 = first_max(jnp.where(lane_f == e1, neg, le))
    r = jnp.exp(m2 - m1)
    w1 = pg_top / (1.0 + r)
    w2 = pg_top * r / (1.0 + r)

    @pl.when((pl.program_id(0) == 0) & (pl.program_id(1) == 0))
    def _():
        cnt_ref[...] = jnp.zeros(cnt_ref.shape, F32)

    hot1 = jnp.where(lane_f == e1, 1.0, 0.0)
    hot2 = jnp.where(lane_f == e2, 1.0, 0.0)
    tot1 = jnp.sum(hot1, axis=0, keepdims=True)
    tot2 = jnp.sum(hot2, axis=0, keepdims=True)
    cnt = cnt_ref[...]
    before1 = cnt + _dot(tri_ref[...], hot1.astype(BF16))
    before2 = cnt + tot1 + _dot(tri_ref[...], hot2.astype(BF16))
    rank1 = jnp.sum(hot1 * before1, axis=1, keepdims=True)
    rank2 = jnp.sum(hot2 * before2, axis=1, keepdims=True)
    cnt_ref[...] = cnt + tot1 + tot2
    rid = jnp.where(lane == 0, e1 - N_GROUPS, jnp.where(lane == 1, e2 - N_GROUPS,
                    jnp.where(lane == 2, rank1, jnp.where(lane == 3, rank2, 0.0))))
    rid_ref[...] = rid.astype(jnp.int32)
    rw_ref[...] = jnp.where(lane == 0, w1, jnp.where(lane == 1, w2, 0.0))


def _merge(ml, df, gl, of, wb, wo, xs, modtab, nf, wr, br, tm, nct, first_block, b0, b):
    _, s, d = xs.shape
    nb = s // tm - first_block
    so = nb * tm
    kind = lambda i: jnp.where(i + first_block >= nct, 1, 0)
    row = lambda bi, i: (bi + b0, i + first_block, 0)
    gate = lambda br_: (lambda bi, i: (bi + b0, i + first_block, F_GT // (d // LANE) + br_))
    outrow = lambda bi, i: (bi, i, 0)
    tri = jnp.asarray(np.tril(np.ones((tm, tm), np.float32), -1), BF16)
    return pl.pallas_call(
        _merge_kernel,
        out_shape=(jax.ShapeDtypeStruct((b, so, d), F32), jax.ShapeDtypeStruct((b, so, d), BF16),
                   jax.ShapeDtypeStruct((b, so, LANE), jnp.int32), jax.ShapeDtypeStruct((b, so, LANE), F32),
                   jax.ShapeDtypeStruct((1, LANE), F32)),
        grid=(b, nb),
        in_specs=[pl.BlockSpec((None, tm, BRANCH_W), row), pl.BlockSpec((None, tm, BRANCH_W), row),
                  pl.BlockSpec((None, tm, BRANCH_W), row),
                  pl.BlockSpec((None, tm, d), gate(0)), pl.BlockSpec((None, tm, d), gate(1)),
                  pl.BlockSpec((None, tm, d), gate(2)),
                  _const_spec((N_BRANCH, BRANCH_W, d)), _const_spec((d, d)),
                  pl.BlockSpec((None, tm, d), row),
                  pl.BlockSpec((None, None, SUBLANE, d), lambda bi, i: (bi + b0, kind(i), 0, 0)),
                  _const_spec((1, d)), _const_spec((2, d, LANE)), _const_spec((1, LANE)), _const_spec((tm, tm))],
        out_specs=(pl.BlockSpec((None, tm, d), outrow), pl.BlockSpec((None, tm, d), outrow),
                   pl.BlockSpec((None, tm, LANE), outrow), pl.BlockSpec((None, tm, LANE), outrow),
                   pl.BlockSpec((1, LANE), lambda bi, i: (0, 0))),
        compiler_params=_cparams(("arbitrary", "arbitrary")),
        name="merge_route",
    )(ml, df, gl, of, of, of, wb, wo, xs, modtab, nf.reshape(1, d), wr, br, tri)


def _gmm_kernel(te_ref, tv_ref, x_ref, wg_ref, wu_ref, wd_ref, y_ref, wgb_ref, wub_ref, wdb_ref):
    i = pl.program_id(0)

    @pl.when((i == 0) | (te_ref[i] != te_ref[jnp.maximum(i - 1, 0)]))
    def _():
        wgb_ref[...] = wg_ref[...].astype(BF16)
        wub_ref[...] = wu_ref[...].astype(BF16)
        wdb_ref[...] = wd_ref[...].astype(BF16)

    @pl.when(tv_ref[i] > 0)
    def _():
        x = x_ref[...]
        a = _dot(x, wgb_ref[...])
        hid = (a * jax.nn.sigmoid(a)) * _dot(x, wub_ref[...])
        y_ref[...] = _dot(hid.astype(BF16), wdb_ref[...]).astype(y_ref.dtype)

    @pl.when(tv_ref[i] == 0)
    def _():
        y_ref[...] = jnp.zeros(y_ref.shape, y_ref.dtype)


def _gmm(tile_expert, tile_valid, xs, wg, wu, wd, li):
    npad, d = xs.shape
    tm = MOE_TILE
    de = wg.shape[3]
    return pl.pallas_call(
        _gmm_kernel,
        out_shape=jax.ShapeDtypeStruct((npad, d), BF16),
        grid_spec=pltpu.PrefetchScalarGridSpec(
            num_scalar_prefetch=2,
            grid=(npad // tm,),
            in_specs=[pl.BlockSpec((tm, d), lambda i, te, tv: (i, 0)),
                      pl.BlockSpec((None, None, d, de), lambda i, te, tv: (li, te[i], 0, 0)),
                      pl.BlockSpec((None, None, d, de), lambda i, te, tv: (li, te[i], 0, 0)),
                      pl.BlockSpec((None, None, de, d), lambda i, te, tv: (li, te[i], 0, 0))],
            out_specs=pl.BlockSpec((tm, d), lambda i, te, tv: (i, 0)),
            scratch_shapes=[pltpu.VMEM((d, de), BF16), pltpu.VMEM((d, de), BF16), pltpu.VMEM((de, d), BF16)]),
        compiler_params=_cparams(("arbitrary",)),
        name="moe_gmm",
    )(tile_expert, tile_valid, xs, wg, wu, wd)


def _route_plan(rid, cnt):
    t = rid.shape[0]
    tm = MOE_TILE
    n_tiles = (TOP_K * t + tm - 1) // tm + N_EXPERTS
    counts = cnt[0, N_GROUPS:N_GROUPS + N_EXPERTS].astype(jnp.int32)
    tiles_e = (counts + tm - 1) // tm
    tile_end = jnp.cumsum(tiles_e)
    pad_start = (tile_end - tiles_e) * tm
    eid, rank = rid[:, 0:TOP_K], rid[:, TOP_K:2 * TOP_K]
    pos = rank + jnp.sum(jnp.where(eid[:, :, None] == jnp.arange(N_EXPERTS), pad_start, 0), axis=-1)
    tile = jnp.arange(n_tiles, dtype=jnp.int32)
    tile_expert = jnp.minimum(jnp.sum(tile[:, None] >= tile_end[None, :], axis=1), N_EXPERTS - 1).astype(jnp.int32)
    tile_valid = (tile < tile_end[-1]).astype(jnp.int32)
    return tile_expert, tile_valid, pos.astype(jnp.int32), n_tiles * tm


def _combine_kernel(x_ref, y0_ref, y1_ref, rw_ref, mod_ref, fg_ref, o_ref, *, final):
    rw = rw_ref[...]
    y = rw[:, 0:1] * y0_ref[...].astype(F32) + rw[:, 1:2] * y1_ref[...].astype(F32)
    xn = x_ref[...] + mod_ref[5:6, :] * y
    o_ref[...] = _rms(xn, fg_ref[...]) if final else xn


def _combine(xs, y0, y1, rw, modtab, fg, tm, nct, first_block, final, b0):
    b, s, d = xs.shape
    kind = lambda i: jnp.where(i + first_block >= nct, 1, 0)
    row = lambda bi, i: (bi, i, 0)
    return pl.pallas_call(
        functools.partial(_combine_kernel, final=final),
        out_shape=jax.ShapeDtypeStruct((b, s, d), F32),
        grid=(b, s // tm),
        in_specs=[pl.BlockSpec((None, tm, d), row), pl.BlockSpec((None, tm, d), row),
                  pl.BlockSpec((None, tm, d), row), pl.BlockSpec((None, tm, LANE), row),
                  pl.BlockSpec((None, None, SUBLANE, d), lambda bi, i: (bi + b0, kind(i), 0, 0)),
                  _const_spec((1, d))],
        out_specs=pl.BlockSpec((None, tm, d), row),
        compiler_params=_cparams(("parallel", "parallel")),
        name="moe_combine",
    )(xs, y0, y1, rw, modtab, fg.reshape(1, d))


def _pack_w_in(w):
    d = w.shape[0]
    mq, mk, mv, mo, mg, dq, dk, dv, gq, gk, gv, gg, ga, gt = jnp.split(
        w, [int(i) for i in np.cumsum(IN_SIZES)[:-1]], axis=1)

    def pad_cols(t):
        return jnp.pad(t, ((0, 0), (0, LANE - t.shape[1])))

    wa = jnp.concatenate([mq, mk, mv, gq, gk, gv, dv], axis=1).astype(BF16)
    wf = jnp.concatenate([dq, dk, mo, gg, gt, pad_cols(mg), pad_cols(ga)], axis=1).astype(BF16)
    return wa, wf


def _dispatch_rows(h2, pos, npad):
    t = h2.shape[0]
    tok = jnp.arange(t, dtype=jnp.int32)
    src = jnp.zeros((npad,), jnp.int32).at[pos[:, 0]].set(tok).at[pos[:, 1]].set(tok)
    return jnp.take(h2, src, axis=0)


def kernel(x, c, ctx, c_ctx, w_mod, b_mod, norm_mix_g, norm_ffn_g, w_in, ml_gate_b, ml_norm_g, df_lambda,
           df_norm_g, gl_w_alpha, gl_b_alpha, gl_norm_g, w_branch, w_out, router_group_w, router_group_b,
           router_expert_w, router_expert_b, moe_w_gate, moe_w_up, moe_w_down, final_norm_g):
    b, n, d = x.shape
    n_ctx = ctx.shape[1]
    s = n_ctx + n
    depth = w_mod.shape[0]
    tm = 256 if n_ctx % 256 == 0 else 128
    assert n_ctx % tm == 0 and n % tm == 0 and n % GRID_W == 0 and n_ctx % CHUNK == 0 and d == D_MODEL
    nct = n_ctx // tm
    nch, nch_ctx = s // CHUNK, n_ctx // CHUNK
    rope = _rope_tables(n)
    xs = jnp.concatenate([ctx, x], axis=1)
    mod_rows = -(-(b + 1) // SUBLANE) * SUBLANE
    cc = jnp.zeros((mod_rows, d), F32).at[:b].set(c).at[b].set(c_ctx)
    for li in range(depth):
        last = li == depth - 1
        lam_init = 0.8 - 0.6 * math.exp(-0.3 * li)
        mod = _modulation(cc, w_mod, b_mod[li], li).reshape(mod_rows, 6, d)
        mod = jnp.pad(mod, ((0, 0), (0, SUBLANE - 6), (0, 0)))
        modtab = jnp.stack([jnp.broadcast_to(mod[b], (b, SUBLANE, d)), mod[:b]], axis=1)
        wa, wf = _pack_w_in(w_in[li])
        oa, of = _inproj(xs, modtab, norm_mix_g[li], wa, wf, tm, nct)
        grow = of[:, :, F_MG * LANE:F_MG * LANE + 4 * ML_HEADS]
        grow = grow.reshape(b, nch, CHUNK, 4 * ML_HEADS).transpose(0, 1, 3, 2)
        ml = _mlstm(oa, of, grow, ml_gate_b[li], ml_norm_g[li], nch_ctx, nch)
        df = _diff_attn(oa, of, rope, df_lambda[li], df_norm_g[li], tm, nct, n_ctx, lam_init)
        gl = _gla(oa, of, gl_w_alpha[li], gl_b_alpha[li], gl_norm_g[li], nch_ctx, nch)
        wr = jnp.zeros((d, LANE), F32).at[:, :N_GROUPS].set(router_group_w[li])
        wr = wr.at[:, N_GROUPS:N_GROUPS + N_EXPERTS].set(router_expert_w[li])
        wr_hi = wr.astype(BF16)
        wr = jnp.stack([wr_hi, (wr - wr_hi.astype(F32)).astype(BF16)])
        br = jnp.zeros((1, LANE), F32).at[0, :N_GROUPS].set(router_group_b[li])
        br = br.at[0, N_GROUPS:N_GROUPS + N_EXPERTS].set(router_expert_b[li])
        first_block = nct if last else 0
        wb, wo = w_branch[li].astype(BF16), w_out[li].astype(BF16)
        bg = b // MOE_GROUPS if b % MOE_GROUPS == 0 else b
        outs = []
        for b0 in range(0, b, bg):
            xn, h2, rid, rw, cnt = _merge(ml, df, gl, of, wb, wo, xs, modtab, norm_ffn_g[li], wr, br, tm, nct,
                                          first_block, b0, bg)
            so = xn.shape[1]
            t = bg * so
            te, tv, pos, npad = _route_plan(rid.reshape(t, LANE), cnt)
            xsorted = _dispatch_rows(h2.reshape(t, d), pos, npad)
            ys = _gmm(te, tv, xsorted, moe_w_gate, moe_w_up, moe_w_down, li)
            y0 = jnp.take(ys, pos[:, 0], axis=0).reshape(bg, so, d)
            y1 = jnp.take(ys, pos[:, 1], axis=0).reshape(bg, so, d)
            outs.append(_combine(xn, y0, y1, rw, modtab, final_norm_g, tm, nct, first_block, last, b0))
        xs = outs[0] if len(outs) == 1 else jnp.concatenate(outs, axis=0)
    return xs
```

```python
import functools
import math

import numpy as np
import jax
import jax.numpy as jnp
from jax import lax
from jax.experimental import pallas as pl
from jax.experimental.pallas import tpu as pltpu

F32 = jnp.float32
BF16 = jnp.bfloat16
HIGHEST = lax.Precision.HIGHEST

D_MODEL = 1024
GRID_W = 64
NORM_EPS = 1e-6
CHUNK = 64
ROPE_THETA = 10000.0
ML_HEADS, ML_DK, ML_DV = 4, 128, 128
DF_HEADS, DF_HD, DF_DV = 4, 64, 128
GL_HEADS, GL_DK, GL_DV, GL_RANK, GL_TAU = 4, 64, 128, 16, 16.0
N_BRANCH, BRANCH_W = 3, 512
N_GROUPS, EXPERTS_PER_GROUP, TOP_K, D_EXPERT = 4, 8, 2, 512
N_EXPERTS = N_GROUPS * EXPERTS_PER_GROUP
IN_SIZES = (
    ML_HEADS * ML_DK, ML_HEADS * ML_DK, ML_HEADS * ML_DV, ML_HEADS * ML_DV, 4 * ML_HEADS,
    DF_HEADS * 2 * DF_HD, DF_HEADS * 2 * DF_HD, DF_HEADS * DF_DV,
    GL_HEADS * GL_DK, GL_HEADS * GL_DK, GL_HEADS * GL_DV, GL_HEADS * GL_DV, 2 * GL_RANK,
    N_BRANCH * D_MODEL,
)

LANE = 128
SUBLANE = 8
VMEM_LIMIT = 56 * 1024 * 1024
MOE_TILE = 512

A_MQ, A_MK, A_MV, A_GQ, A_GK, A_GV, A_DV = 0, 4, 8, 12, 14, 16, 20
A_WIDTH = 24 * LANE
F_DQ, F_DK, F_MO, F_GG, F_GT, F_MG, F_GA = 0, 4, 8, 12, 16, 40, 41
F_WIDTH = 42 * LANE
HEAD_W = 4 * LANE


def _cparams(sem):
    return pltpu.CompilerParams(dimension_semantics=sem, vmem_limit_bytes=VMEM_LIMIT)


def _const_spec(shape):
    nd = len(shape)
    return pl.BlockSpec(shape, lambda *_: (0,) * nd, pipeline_mode=pl.Buffered(1))


def _logsig(x):
    return jnp.minimum(x, 0.0) - jnp.log1p(jnp.exp(-jnp.abs(x)))


def _dot(a, b):
    return jnp.dot(a, b, preferred_element_type=F32)


def _dot_nt(a, b):
    return lax.dot_general(a, b, (((1,), (1,)), ((), ())), preferred_element_type=F32)


def _dot_tn(a, b):
    return lax.dot_general(a, b, (((0,), (0,)), ((), ())), preferred_element_type=F32)


def _rms(x, g):
    return x * lax.rsqrt(jnp.mean(x * x, axis=-1, keepdims=True) + NORM_EPS) * g


def _mod_kernel(c_ref, w_ref, b_ref, o_ref):
    c = c_ref[...]
    s = c * jax.nn.sigmoid(c)
    o_ref[...] = jnp.dot(s, w_ref[...], precision=HIGHEST, preferred_element_type=F32) + b_ref[...]


def _modulation(cc, w_mod, b_mod, li):
    rows, d = cc.shape
    n = w_mod.shape[2]
    tn = 512
    return pl.pallas_call(
        _mod_kernel,
        out_shape=jax.ShapeDtypeStruct((rows, n), F32),
        grid=(n // tn,),
        in_specs=[pl.BlockSpec((rows, d), lambda j: (0, 0)),
                  pl.BlockSpec((None, d, tn), lambda j: (li, 0, j)),
                  pl.BlockSpec((1, tn), lambda j: (0, j))],
        out_specs=pl.BlockSpec((rows, tn), lambda j: (0, j)),
        compiler_params=_cparams(("parallel",)),
        name="adaln_mod",
    )(cc, w_mod, b_mod.reshape(1, n))


def _inproj_kernel(x_ref, mod_ref, g_ref, wa_ref, wf_ref, oa_ref, of_ref):
    y = _rms(x_ref[...], g_ref[...])
    h = (y * (1.0 + mod_ref[1:2, :]) + mod_ref[0:1, :]).astype(BF16)
    oa_ref[...] = _dot(h, wa_ref[...]).astype(BF16)
    of_ref[...] = _dot(h, wf_ref[...])


def _inproj(xs, modtab, g, wa, wf, tm, nct):
    b, s, d = xs.shape
    kind = lambda i: jnp.where(i >= nct, 1, 0)
    return pl.pallas_call(
        _inproj_kernel,
        out_shape=(jax.ShapeDtypeStruct((b, s, A_WIDTH), BF16), jax.ShapeDtypeStruct((b, s, F_WIDTH), F32)),
        grid=(b, s // tm),
        in_specs=[pl.BlockSpec((None, tm, d), lambda bi, i: (bi, i, 0)),
                  pl.BlockSpec((None, None, SUBLANE, d), lambda bi, i: (bi, kind(i), 0, 0)),
                  _const_spec((1, d)), _const_spec((d, A_WIDTH)), _const_spec((d, F_WIDTH))],
        out_specs=(pl.BlockSpec((None, tm, A_WIDTH), lambda bi, i: (bi, i, 0)),
                   pl.BlockSpec((None, tm, F_WIDTH), lambda bi, i: (bi, i, 0))),
        compiler_params=_cparams(("parallel", "parallel")),
        name="in_proj",
    )(xs, modtab, g.reshape(1, d), wa, wf)


def _chunk_ids(t, nct, nch):
    return t, jnp.where(t < nct, nct - 1 - t, nch - 1 - (t - nct))


def _head_cols(h):
    return slice(h * LANE, (h + 1) * LANE)


def _transpose_chunks(src_ref, dst_ref, nch):
    eye = (lax.broadcasted_iota(jnp.int32, (LANE, LANE), 0)
           == lax.broadcasted_iota(jnp.int32, (LANE, LANE), 1)).astype(BF16)

    def body(c, carry):
        blk = src_ref[pl.ds(pl.multiple_of(c * CHUNK, CHUNK), CHUNK), :]
        for h in range(HEAD_W // LANE):
            dst_ref[c, h] = _dot_nt(eye, blk[:, _head_cols(h)]).astype(BF16)
        return carry

    lax.fori_loop(0, nch, body, 0, unroll=2)


def _mlstm_kernel(q_ref, k_ref, v_ref, o_ref, gcol_ref, grow_ref, bcol_ref, brow_ref, ng_ref, out_ref,
                  hf_ref, hb_ref, ct_ref, kt_ref, *, nct, nch):
    L = CHUNK
    H = ML_HEADS
    _transpose_chunks(k_ref, kt_ref, nch)
    jj = lax.broadcasted_iota(jnp.int32, (L, L), 0)
    ii = lax.broadcasted_iota(jnp.int32, (L, L), 1)
    lane = lax.broadcasted_iota(jnp.int32, (L, LANE), 1)
    ones_col = jnp.where(lane == 0, 1.0, 0.0).astype(BF16)
    scale = ML_DK ** -0.5
    vis = ((ii <= jj), (ii >= jj))
    cum = tuple(jnp.where(m_, 1.0, 0.0).astype(BF16) for m_ in vis)
    cum_t = tuple(jnp.where(m_, 1.0, 0.0).astype(BF16) for m_ in ((jj <= ii), (jj >= ii)))
    ct_ref[...] = jnp.zeros(ct_ref.shape, F32)

    def split2(x):
        hi = x.astype(BF16)
        return hi, (x - hi.astype(F32)).astype(BF16)

    def step(t, carry):
        chunks = _chunk_ids(t, nct, nch)
        rows = [pl.ds(pl.multiple_of(c * L, L), L) for c in chunks]
        chains = [(d, h) for d in range(2) for h in range(H)]
        q, kt, vext, ct, sq, qc = {}, {}, {}, {}, {}, {}
        for d, h in chains:
            q[d, h] = q_ref[rows[d], _head_cols(h)]
            kt[d, h] = kt_ref[chunks[d], h]
            vext[d, h] = jnp.concatenate([v_ref[rows[d], _head_cols(h)], ones_col], axis=1)
            ct[d, h] = ct_ref[d, h]
            sq[d, h] = _dot(q[d, h], kt[d, h])
            qc[d, h] = _dot(q[d, h], ct[d, h].astype(BF16))
        gr, bc_all, br_all = [], [], []
        for d in range(2):
            gc = gcol_ref[rows[d], :] + bcol_ref[...]
            gr.append(grow_ref[chunks[d]] + brow_ref[...])
            chi, clo = split2(_logsig(gc))
            rhi, rlo = split2(_logsig(gr[d]))
            bc_all.append(_dot(cum[d], chi) + _dot(cum[d], clo))
            br_all.append(_dot(rhi, cum_t[d]) + _dot(rlo, cum_t[d]))
        w, w_inter, einv, decay, kw, new = {}, {}, {}, {}, {}, []
        for d, h in chains:
            m = carry[d * H + h]
            gi = 2 * d * H + h
            b_row = br_all[d][gi + H:gi + H + 1, :]
            b_col = bc_all[d][:, gi + H:gi + H + 1]
            b_last = b_row[:, L - 1:L] if d == 0 else b_row[:, 0:1]
            u_row = gr[d][gi:gi + 1, :] - b_row
            g_col = jnp.maximum(jnp.max(jnp.where(vis[d], u_row, -jnp.inf), axis=1, keepdims=True), m)
            w[d, h] = jnp.exp(jnp.where(vis[d], u_row - g_col, -jnp.inf)) * scale
            w_inter[d, h] = jnp.exp(m - g_col)
            einv[d, h] = jnp.exp(-(b_col + g_col))
            m_new = b_last + jnp.maximum(m, jnp.max(u_row, axis=1, keepdims=True))
            wk_row = jnp.exp(b_last + u_row - m_new) * scale
            decay[d, h] = jnp.exp(b_last + m - m_new)
            kw[d, h] = (kt[d, h].astype(F32) * wk_row).astype(BF16)
            new.append(m_new)
        for d, h in chains:
            ct_ref[d, h] = decay[d, h] * ct[d, h] + _dot(kw[d, h], vext[d, h])
        sv = {}
        for d, h in chains:
            sv[d, h] = _dot((sq[d, h] * w[d, h]).astype(BF16), vext[d, h])
        for d, h in chains:
            num = sv[d, h][:, :ML_DV] + w_inter[d, h] * qc[d, h][:, :ML_DV]
            den = sv[d, h][:, ML_DV:ML_DV + 1] + w_inter[d, h] * qc[d, h][:, ML_DV:ML_DV + 1]
            hh = num / jnp.maximum(jnp.abs(den), einv[d, h])
            if d == 0:
                hf_ref[rows[d], _head_cols(h)] = hh
            else:
                hb_ref[rows[d], _head_cols(h)] = hh
        return tuple(new)

    zero = jnp.zeros((1, 1), F32)
    lax.fori_loop(0, nch, step, (zero,) * (2 * H), unroll=2)
    for h in range(H):
        y = _rms(hf_ref[:, _head_cols(h)] + hb_ref[:, _head_cols(h)], ng_ref[...])
        out_ref[:, _head_cols(h)] = (y * jax.nn.sigmoid(o_ref[:, _head_cols(h)])).astype(BF16)


def _mlstm(oa, of, grow, gate_b, norm_g, nct, nch):
    b, s, _ = oa.shape
    bcol = jnp.zeros((1, LANE), F32).at[0, :4 * ML_HEADS].set(gate_b)
    brow = gate_b.reshape(4 * ML_HEADS, 1)
    sect = lambda base: (lambda bi: (bi, 0, base // 4))
    return pl.pallas_call(
        functools.partial(_mlstm_kernel, nct=nct, nch=nch),
        out_shape=jax.ShapeDtypeStruct((b, s, HEAD_W), BF16),
        grid=(b,),
        in_specs=[pl.BlockSpec((None, s, HEAD_W), sect(A_MQ)),
                  pl.BlockSpec((None, s, HEAD_W), sect(A_MK)),
                  pl.BlockSpec((None, s, HEAD_W), sect(A_MV)),
                  pl.BlockSpec((None, s, HEAD_W), sect(F_MO)),
                  pl.BlockSpec((None, s, LANE), lambda bi: (bi, 0, F_MG)),
                  pl.BlockSpec((None, nch, 4 * ML_HEADS, CHUNK), lambda bi: (bi, 0, 0, 0)),
                  pl.BlockSpec((1, LANE), lambda bi: (0, 0)),
                  pl.BlockSpec((4 * ML_HEADS, 1), lambda bi: (0, 0)),
                  pl.BlockSpec((1, ML_DV), lambda bi: (0, 0))],
        out_specs=pl.BlockSpec((None, s, HEAD_W), lambda bi: (bi, 0, 0)),
        scratch_shapes=[pltpu.VMEM((s, HEAD_W), F32), pltpu.VMEM((s, HEAD_W), F32),
                        pltpu.VMEM((2, ML_HEADS, ML_DK, 2 * LANE), F32),
                        pltpu.VMEM((nch, ML_HEADS, ML_DK, CHUNK), BF16)],
        compiler_params=_cparams(("parallel",)),
        name="mlstm",
    )(oa, oa, oa, of, of, grow, bcol, brow, norm_g.reshape(1, ML_DV))


def _gla_tables():
    L = CHUNK
    G = np.zeros((2, 7 * L, L), np.float32)
    lvl = np.full((2, L, L), 7, np.int32)
    for d in range(2):
        for p in range(L):
            if d == 0:
                G[d, p, :p + 1] = 1
            else:
                G[d, p, p:] = 1
        for li, s in enumerate((32, 16, 8, 4, 2, 1)):
            for p in range(L):
                base = (p // (2 * s)) * 2 * s
                row = (li + 1) * L + p
                if d == 0:
                    mid = base + s
                    if p >= mid:
                        G[d, row, mid + 1:p + 1] = 1
                    else:
                        G[d, row, p + 1:mid + 1] = 1
                else:
                    mid = base + s - 1
                    if p <= mid:
                        G[d, row, p:mid] = 1
                    else:
                        G[d, row, mid:p] = 1
            blk = np.arange(L) // (2 * s)
            upper = (np.arange(L) % (2 * s)) >= s
            same = blk[:, None] == blk[None, :]
            if d == 0:
                sel = same & upper[:, None] & ~upper[None, :]
            else:
                sel = same & ~upper[:, None] & upper[None, :]
            lvl[d][sel] = li
        lvl[d][np.arange(L), np.arange(L)] = 6
    return G, lvl


GL_PAIRS = GL_HEADS * GL_DK // LANE


def _gla_kernel(q_ref, k_ref, v_ref, g_ref, a_ref, wa_ref, ba_ref, gm_ref, lvl_ref, ng_ref, out_ref,
                acc_ref, la_ref, st_ref, vt_ref, *, nct, nch):
    L = CHUNK
    _transpose_chunks(v_ref, vt_ref, nch)
    for d in range(2):
        pre = jnp.dot(a_ref[...], wa_ref[d], precision=HIGHEST, preferred_element_type=F32) + ba_ref[d]
        la_ref[d] = _logsig(pre) * (1.0 / GL_TAU)
    acc_ref[...] = jnp.zeros(acc_ref.shape, F32)
    st_ref[...] = jnp.zeros(st_ref.shape, F32)
    first = lax.broadcasted_iota(jnp.int32, (L, LANE), 1) < GL_DK
    first2 = lax.broadcasted_iota(jnp.int32, (GL_DV, LANE), 1) < GL_DK

    def split(x):
        return jnp.concatenate([jnp.where(first, x, 0.0), jnp.where(first, 0.0, x)], axis=0).astype(BF16)

    def step(t, carry):
        chunks = _chunk_ids(t, nct, nch)
        rows = [pl.ds(pl.multiple_of(c * L, L), L) for c in chunks]
        chains = [(d, p) for d in range(2) for p in range(GL_PAIRS)]
        xs = []
        for d in range(2):
            lac = la_ref[d, rows[d], :]
            hi = lac.astype(BF16)
            lo = (lac - hi.astype(F32)).astype(BF16)
            xs.append(_dot(gm_ref[d], hi) + _dot(gm_ref[d], lo))
        q, k, cs, tot, st, inter, upd = {}, {}, {}, {}, {}, {}, {}
        for d, p in chains:
            q[d, p] = q_ref[rows[d], _head_cols(p)].astype(F32) * (GL_DK ** -0.5)
            k[d, p] = k_ref[rows[d], _head_cols(p)].astype(F32)
            cs[d, p] = xs[d][0:L, _head_cols(p)]
            tot[d, p] = cs[d, p][L - 1:L] if d == 0 else cs[d, p][0:1]
            st[d, p] = st_ref[d, p]
        for d, p in chains:
            inter[d, p] = _dot_nt(split(q[d, p] * jnp.exp(cs[d, p])), st[d, p].astype(BF16))
            ke = (k[d, p] * jnp.exp(tot[d, p] - cs[d, p])).astype(BF16)
            u = [_dot(vt_ref[chunks[d], 2 * p + hh], ke) for hh in range(2)]
            st_ref[d, p] = st[d, p] * jnp.exp(tot[d, p]) + jnp.where(first2, u[0], u[1])
        amat = {}
        for d, p in chains:
            amat[d, p] = jnp.where(lvl_ref[d] == 6, _dot_nt(split(q[d, p]), k[d, p].astype(BF16)), 0.0)
        for li in range(6):
            for d, p in chains:
                e = jnp.exp(xs[d][(li + 1) * L:(li + 2) * L, _head_cols(p)])
                lev = _dot_nt(split(q[d, p] * e), (k[d, p] * e).astype(BF16))
                amat[d, p] = jnp.where(lvl_ref[d] == li, lev, amat[d, p])
        for d, p in chains:
            a = amat[d, p].astype(BF16)
            for hh in range(2):
                cols = _head_cols(2 * p + hh)
                acc_ref[rows[d], cols] += (_dot(a[hh * L:(hh + 1) * L], v_ref[rows[d], cols])
                                           + inter[d, p][hh * L:(hh + 1) * L])
        return carry

    lax.fori_loop(0, nch, step, 0, unroll=2)
    for h in range(GL_HEADS):
        cols = _head_cols(h)
        g = g_ref[:, cols]
        out_ref[:, cols] = (_rms(acc_ref[:, cols], ng_ref[...]) * (g * jax.nn.sigmoid(g))).astype(BF16)


def _gla(oa, of, w_alpha, b_alpha, norm_g, nct, nch):
    b, s, _ = oa.shape
    gmat, lvl = _gla_tables()
    lvl = np.concatenate([lvl, lvl], axis=1)
    qk_w = GL_HEADS * GL_DK
    wa = jnp.zeros((2, LANE, qk_w), F32)
    for d in range(2):
        wa = wa.at[d, d * GL_RANK:(d + 1) * GL_RANK, :].set(w_alpha[d])
    ba = b_alpha.reshape(2, 1, qk_w)
    full = lambda nd: (lambda bi: (0,) * nd)
    return pl.pallas_call(
        functools.partial(_gla_kernel, nct=nct, nch=nch),
        out_shape=jax.ShapeDtypeStruct((b, s, HEAD_W), BF16),
        grid=(b,),
        in_specs=[pl.BlockSpec((None, s, qk_w), lambda bi: (bi, 0, A_GQ * LANE // qk_w)),
                  pl.BlockSpec((None, s, qk_w), lambda bi: (bi, 0, A_GK * LANE // qk_w)),
                  pl.BlockSpec((None, s, HEAD_W), lambda bi: (bi, 0, A_GV // 4)),
                  pl.BlockSpec((None, s, HEAD_W), lambda bi: (bi, 0, F_GG // 4)),
                  pl.BlockSpec((None, s, LANE), lambda bi: (bi, 0, F_GA)),
                  pl.BlockSpec((2, LANE, qk_w), full(3)),
                  pl.BlockSpec((2, 1, qk_w), full(3)),
                  pl.BlockSpec((2, 7 * CHUNK, CHUNK), full(3)),
                  pl.BlockSpec((2, 2 * CHUNK, CHUNK), full(3)),
                  pl.BlockSpec((1, GL_DV), full(2))],
        out_specs=pl.BlockSpec((None, s, HEAD_W), lambda bi: (bi, 0, 0)),
        scratch_shapes=[pltpu.VMEM((s, HEAD_W), F32), pltpu.VMEM((2, s, qk_w), F32),
                        pltpu.VMEM((2, GL_PAIRS, GL_DV, LANE), F32),
                        pltpu.VMEM((nch, GL_HEADS, GL_DV, CHUNK), BF16)],
        compiler_params=_cparams(("parallel",)),
        name="gla",
    )(oa, oa, oa, of, of, wa, ba, jnp.asarray(gmat, BF16), jnp.asarray(lvl), norm_g.reshape(1, GL_DV))


def _rope_tables(n):
    t = np.arange(n)
    n_freq = DF_HD // 4
    inv = jnp.asarray(ROPE_THETA, F32) ** (-jnp.arange(n_freq, dtype=F32) / n_freq)
    ang_r = jnp.asarray(t // GRID_W, F32)[:, None] * inv
    ang_c = jnp.asarray(t % GRID_W, F32)[:, None] * inv
    ang = jnp.concatenate([ang_r, ang_r, ang_c, ang_c] * 2, axis=1)
    first = (np.arange(LANE) % 32) < 16
    cos, sin = jnp.cos(ang), jnp.sin(ang)
    return cos, jnp.where(first, -sin, 0.0), jnp.where(first, 0.0, sin)


DF_ROW_GROUPS = 4


def _rope(x, cos, sa, sb):
    return x * cos + pltpu.roll(x, LANE - 16, 1) * sa + pltpu.roll(x, 16, 1) * sb


def _diff_kernel(q_ref, k_ref, v_ref, cq_ref, saq_ref, sbq_ref, ck_ref, sak_ref, sbk_ref, lam_ref, ng_ref,
                 out_ref, kr_ref, *, nct, n_ctx, lam_init):
    s = k_ref.shape[0]
    tq = q_ref.shape[0]
    i = pl.program_id(2)

    @pl.when(i == 0)
    def _():
        kr_ref[0:n_ctx, :] = k_ref[0:n_ctx, :].astype(BF16)
        kr_ref[n_ctx:s, :] = _rope(k_ref[n_ctx:s, :], ck_ref[...], sak_ref[...], sbk_ref[...]).astype(BF16)

    lp = lam_ref[...]
    lam = (jnp.exp(jnp.sum(lp[0:1] * lp[1:2], axis=1, keepdims=True))
           - jnp.exp(jnp.sum(lp[2:3] * lp[3:4], axis=1, keepdims=True)) + lam_init)
    lane = lax.broadcasted_iota(jnp.int32, q_ref.shape, 1)

    def attend(qb, nk):
        qb = qb * (DF_HD ** -0.5)
        qs = jnp.concatenate([jnp.where(lane < DF_HD, qb, 0.0), jnp.where(lane >= DF_HD, qb, 0.0)],
                             axis=0).astype(BF16)
        rg = 2 * tq // DF_ROW_GROUPS
        scs = [_dot_nt(qs[g * rg:(g + 1) * rg], kr_ref[0:nk, :]) for g in range(DF_ROW_GROUPS)]
        ovs = []
        for sc in scs:
            p = jnp.exp(sc - jnp.max(sc, axis=1, keepdims=True))
            rl = 1.0 / jnp.sum(p, axis=1, keepdims=True)
            ovs.append((p.astype(BF16), rl))
        ov = jnp.concatenate([_dot(p, v_ref[0:nk, :]) * rl for p, rl in ovs], axis=0)
        o = ov[0:tq] - lam * ov[tq:2 * tq]
        out_ref[...] = (_rms(o, ng_ref[...]) * (1.0 - lam_init)).astype(BF16)

    @pl.when(i < nct)
    def _():
        attend(q_ref[...], n_ctx)

    @pl.when(i >= nct)
    def _():
        attend(_rope(q_ref[...], cq_ref[...], saq_ref[...], sbq_ref[...]), s)


def _diff_attn(oa, of, rope, df_lambda, norm_g, tq, nct, n_ctx, lam_init):
    b, s, _ = oa.shape
    n = s - n_ctx
    cos, sa, sb = rope
    lam_p = jnp.zeros((4, LANE), F32).at[:, :DF_HD].set(df_lambda)
    qblk = lambda bi, h, i: (jnp.maximum(i - nct, 0), 0)
    full = lambda bi, h, i: (0, 0)
    return pl.pallas_call(
        functools.partial(_diff_kernel, nct=nct, n_ctx=n_ctx, lam_init=lam_init),
        out_shape=jax.ShapeDtypeStruct((b, s, DF_HEADS * DF_DV), BF16),
        grid=(b, DF_HEADS, s // tq),
        in_specs=[pl.BlockSpec((None, tq, LANE), lambda bi, h, i: (bi, i, F_DQ + h)),
                  pl.BlockSpec((None, s, LANE), lambda bi, h, i: (bi, 0, F_DK + h)),
                  pl.BlockSpec((None, s, LANE), lambda bi, h, i: (bi, 0, A_DV + h)),
                  pl.BlockSpec((tq, LANE), qblk), pl.BlockSpec((tq, LANE), qblk), pl.BlockSpec((tq, LANE), qblk),
                  pl.BlockSpec((n, LANE), full), pl.BlockSpec((n, LANE), full), pl.BlockSpec((n, LANE), full),
                  pl.BlockSpec((4, LANE), full), pl.BlockSpec((1, DF_DV), full)],
        out_specs=pl.BlockSpec((None, tq, LANE), lambda bi, h, i: (bi, i, h)),
        scratch_shapes=[pltpu.VMEM((s, LANE), BF16)],
        compiler_params=_cparams(("parallel", "parallel", "arbitrary")),
        name="diff_attn",
    )(of, of, oa, cos, sa, sb, cos, sa, sb, lam_p, norm_g.reshape(1, DF_DV))


def _merge_kernel(ml_ref, df_ref, gl_ref, g0_ref, g1_ref, g2_ref, wb_ref, wo_ref, x_ref, mod_ref, nf_ref,
                  wr_ref, br_ref, tri_ref, xo_ref, h2_ref, rid_ref, rw_ref, cnt_ref):
    y = (jax.nn.sigmoid(g0_ref[...]) * _dot(ml_ref[...], wb_ref[0])
         + jax.nn.sigmoid(g1_ref[...]) * _dot(df_ref[...], wb_ref[1])
         + jax.nn.sigmoid(g2_ref[...]) * _dot(gl_ref[...], wb_ref[2]))
    xn = x_ref[...] + mod_ref[2:3, :] * _dot(y.astype(BF16), wo_ref[...])
    xo_ref[...] = xn
    h2 = _rms(xn, nf_ref[...]) * (1.0 + mod_ref[4:5, :]) + mod_ref[3:4, :]
    h2_hi = h2.astype(BF16)
    h2_ref[...] = h2_hi
    h2_lo = (h2 - h2_hi.astype(F32)).astype(BF16)
    logits = (_dot(h2_hi, wr_ref[0]) + (_dot(h2_lo, wr_ref[0]) + _dot(h2_hi, wr_ref[1]))) + br_ref[...]
    lane = lax.broadcasted_iota(jnp.int32, logits.shape, 1)
    lane_f = lane.astype(F32)
    neg = -jnp.inf

    def first_max(vals):
        mx = jnp.max(vals, axis=1, keepdims=True)
        return mx, jnp.min(jnp.where(vals == mx, lane_f, float(LANE)), axis=1, keepdims=True)

    is_grp = lane < N_GROUPS
    gmax, gidx = first_max(jnp.where(is_grp, logits, neg))
    pg_top = 1.0 / jnp.sum(jnp.where(is_grp, jnp.exp(logits - gmax), 0.0), axis=1, keepdims=True)
    lo = N_GROUPS + gidx * EXPERTS_PER_GROUP
    in_grp = (lane_f >= lo) & (lane_f < lo + EXPERTS_PER_GROUP)
    le = jnp.where(in_grp, logits, neg)
    m1, e1 = first_max(le)
    m2, e2 = first_max(jnp.where(lane_f == e1, neg, le))
    r = jnp.exp(m2 - m1)
    w1 = pg_top / (1.0 + r)
    w2 = pg_top * r / (1.0 + r)

    @pl.when((pl.program_id(0) == 0) & (pl.program_id(1) == 0))
    def _():
        cnt_ref[...] = jnp.zeros(cnt_ref.shape, F32)

    hot1 = jnp.where(lane_f == e1, 1.0, 0.0)
    hot2 = jnp.where(lane_f == e2, 1.0, 0.0)
    tot1 = jnp.sum(hot1, axis=0, keepdims=True)
    tot2 = jnp.sum(hot2, axis=0, keepdims=True)
    cnt = cnt_ref[...]
    before1 = cnt + _dot(tri_ref[...], hot1.astype(BF16))
    before2 = cnt + tot1 + _dot(tri_ref[...], hot2.astype(BF16))
    rank1 = jnp.sum(hot1 * before1, axis=1, keepdims=True)
    rank2 = jnp.sum(hot2 * before2, axis=1, keepdims=True)
    cnt_ref[...] = cnt + tot1 + tot2
    rid = jnp.where(lane == 0, e1 - N_GROUPS, jnp.where(lane == 1, e2 - N_GROUPS,
                    jnp.where(lane == 2, rank1, jnp.where(lane == 3, rank2, 0.0))))
    rid_ref[...] = rid.astype(jnp.int32)
    rw_ref[...] = jnp.where(lane == 0, w1, jnp.where(lane == 1, w2, 0.0))


def _merge(ml, df, gl, of, wb, wo, xs, modtab, nf, wr, br, tm, nct, first_block):
    b, s, d = xs.shape
    nb = s // tm - first_block
    so = nb * tm
    kind = lambda i: jnp.where(i + first_block >= nct, 1, 0)
    row = lambda bi, i: (bi, i + first_block, 0)
    gate = lambda br_: (lambda bi, i: (bi, i + first_block, F_GT // (d // LANE) + br_))
    outrow = lambda bi, i: (bi, i, 0)
    tri = jnp.asarray(np.tril(np.ones((tm, tm), np.float32), -1), BF16)
    return pl.pallas_call(
        _merge_kernel,
        out_shape=(jax.ShapeDtypeStruct((b, so, d), F32), jax.ShapeDtypeStruct((b, so, d), BF16),
                   jax.ShapeDtypeStruct((b, so, LANE), jnp.int32), jax.ShapeDtypeStruct((b, so, LANE), F32),
                   jax.ShapeDtypeStruct((1, LANE), F32)),
        grid=(b, nb),
        in_specs=[pl.BlockSpec((None, tm, BRANCH_W), row), pl.BlockSpec((None, tm, BRANCH_W), row),
                  pl.BlockSpec((None, tm, BRANCH_W), row),
                  pl.BlockSpec((None, tm, d), gate(0)), pl.BlockSpec((None, tm, d), gate(1)),
                  pl.BlockSpec((None, tm, d), gate(2)),
                  _const_spec((N_BRANCH, BRANCH_W, d)), _const_spec((d, d)),
                  pl.BlockSpec((None, tm, d), row),
                  pl.BlockSpec((None, None, SUBLANE, d), lambda bi, i: (bi, kind(i), 0, 0)),
                  _const_spec((1, d)), _const_spec((2, d, LANE)), _const_spec((1, LANE)), _const_spec((tm, tm))],
        out_specs=(pl.BlockSpec((None, tm, d), outrow), pl.BlockSpec((None, tm, d), outrow),
                   pl.BlockSpec((None, tm, LANE), outrow), pl.BlockSpec((None, tm, LANE), outrow),
                   pl.BlockSpec((1, LANE), lambda bi, i: (0, 0))),
        compiler_params=_cparams(("arbitrary", "arbitrary")),
        name="merge_route",
    )(ml, df, gl, of, of, of, wb, wo, xs, modtab, nf.reshape(1, d), wr, br, tri)


def _gmm_kernel(te_ref, tv_ref, x_ref, wg_ref, wu_ref, wd_ref, y_ref, wgb_ref, wub_ref, wdb_ref):
    i = pl.program_id(0)

    @pl.when((i == 0) | (te_ref[i] != te_ref[jnp.maximum(i - 1, 0)]))
    def _():
        wgb_ref[...] = wg_ref[...].astype(BF16)
        wub_ref[...] = wu_ref[...].astype(BF16)
        wdb_ref[...] = wd_ref[...].astype(BF16)

    @pl.when(tv_ref[i] > 0)
    def _():
        x = x_ref[...]
        a = _dot(x, wgb_ref[...])
        hid = (a * jax.nn.sigmoid(a)) * _dot(x, wub_ref[...])
        y_ref[...] = _dot(hid.astype(BF16), wdb_ref[...]).astype(y_ref.dtype)

    @pl.when(tv_ref[i] == 0)
    def _():
        y_ref[...] = jnp.zeros(y_ref.shape, y_ref.dtype)


def _gmm(tile_expert, tile_valid, xs, wg, wu, wd, li):
    npad, d = xs.shape
    tm = MOE_TILE
    de = wg.shape[3]
    return pl.pallas_call(
        _gmm_kernel,
        out_shape=jax.ShapeDtypeStruct((npad, d), BF16),
        grid_spec=pltpu.PrefetchScalarGridSpec(
            num_scalar_prefetch=2,
            grid=(npad // tm,),
            in_specs=[pl.BlockSpec((tm, d), lambda i, te, tv: (i, 0)),
                      pl.BlockSpec((None, None, d, de), lambda i, te, tv: (li, te[i], 0, 0)),
                      pl.BlockSpec((None, None, d, de), lambda i, te, tv: (li, te[i], 0, 0)),
                      pl.BlockSpec((None, None, de, d), lambda i, te, tv: (li, te[i], 0, 0))],
            out_specs=pl.BlockSpec((tm, d), lambda i, te, tv: (i, 0)),
            scratch_shapes=[pltpu.VMEM((d, de), BF16), pltpu.VMEM((d, de), BF16), pltpu.VMEM((de, d), BF16)]),
        compiler_params=_cparams(("arbitrary",)),
        name="moe_gmm",
    )(tile_expert, tile_valid, xs, wg, wu, wd)


def _route_plan(rid, cnt):
    t = rid.shape[0]
    tm = MOE_TILE
    n_tiles = (TOP_K * t + tm - 1) // tm + N_EXPERTS
    counts = cnt[0, N_GROUPS:N_GROUPS + N_EXPERTS].astype(jnp.int32)
    tiles_e = (counts + tm - 1) // tm
    tile_end = jnp.cumsum(tiles_e)
    pad_start = (tile_end - tiles_e) * tm
    eid, rank = rid[:, 0:TOP_K], rid[:, TOP_K:2 * TOP_K]
    pos = rank + jnp.sum(jnp.where(eid[:, :, None] == jnp.arange(N_EXPERTS), pad_start, 0), axis=-1)
    tile = jnp.arange(n_tiles, dtype=jnp.int32)
    tile_expert = jnp.minimum(jnp.sum(tile[:, None] >= tile_end[None, :], axis=1), N_EXPERTS - 1).astype(jnp.int32)
    tile_valid = (tile < tile_end[-1]).astype(jnp.int32)
    return tile_expert, tile_valid, pos.astype(jnp.int32), n_tiles * tm


def _combine_kernel(x_ref, y0_ref, y1_ref, rw_ref, mod_ref, fg_ref, o_ref, *, final):
    rw = rw_ref[...]
    y = rw[:, 0:1] * y0_ref[...].astype(F32) + rw[:, 1:2] * y1_ref[...].astype(F32)
    xn = x_ref[...] + mod_ref[5:6, :] * y
    o_ref[...] = _rms(xn, fg_ref[...]) if final else xn


def _combine(xs, y0, y1, rw, modtab, fg, tm, nct, first_block, final):
    b, s, d = xs.shape
    kind = lambda i: jnp.where(i + first_block >= nct, 1, 0)
    row = lambda bi, i: (bi, i, 0)
    return pl.pallas_call(
        functools.partial(_combine_kernel, final=final),
        out_shape=jax.ShapeDtypeStruct((b, s, d), F32),
        grid=(b, s // tm),
        in_specs=[pl.BlockSpec((None, tm, d), row), pl.BlockSpec((None, tm, d), row),
                  pl.BlockSpec((None, tm, d), row), pl.BlockSpec((None, tm, LANE), row),
                  pl.BlockSpec((None, None, SUBLANE, d), lambda bi, i: (bi, kind(i), 0, 0)),
                  _const_spec((1, d))],
        out_specs=pl.BlockSpec((None, tm, d), row),
        compiler_params=_cparams(("parallel", "parallel")),
        name="moe_combine",
    )(xs, y0, y1, rw, modtab, fg.reshape(1, d))


def _pack_w_in(w):
    d = w.shape[0]
    mq, mk, mv, mo, mg, dq, dk, dv, gq, gk, gv, gg, ga, gt = jnp.split(
        w, [int(i) for i in np.cumsum(IN_SIZES)[:-1]], axis=1)

    def pad_cols(t):
        return jnp.pad(t, ((0, 0), (0, LANE - t.shape[1])))

    wa = jnp.concatenate([mq, mk, mv, gq, gk, gv, dv], axis=1).astype(BF16)
    wf = jnp.concatenate([dq, dk, mo, gg, gt, pad_cols(mg), pad_cols(ga)], axis=1).astype(BF16)
    return wa, wf


def _dispatch_rows(h2, pos, npad):
    t = h2.shape[0]
    tok = jnp.arange(t, dtype=jnp.int32)
    src = jnp.zeros((npad,), jnp.int32).at[pos[:, 0]].set(tok).at[pos[:, 1]].set(tok)
    return jnp.take(h2, src, axis=0)


def kernel(x, c, ctx, c_ctx, w_mod, b_mod, norm_mix_g, norm_ffn_g, w_in, ml_gate_b, ml_norm_g, df_lambda,
           df_norm_g, gl_w_alpha, gl_b_alpha, gl_norm_g, w_branch, w_out, router_group_w, router_group_b,
           router_expert_w, router_expert_b, moe_w_gate, moe_w_up, moe_w_down, final_norm_g):
    b, n, d = x.shape
    n_ctx = ctx.shape[1]
    s = n_ctx + n
    depth = w_mod.shape[0]
    tm = 256 if n_ctx % 256 == 0 else 128
    assert n_ctx % tm == 0 and n % tm == 0 and n % GRID_W == 0 and n_ctx % CHUNK == 0 and d == D_MODEL
    nct = n_ctx // tm
    nch, nch_ctx = s // CHUNK, n_ctx // CHUNK
    rope = _rope_tables(n)
    xs = jnp.concatenate([ctx, x], axis=1)
    mod_rows = -(-(b + 1) // SUBLANE) * SUBLANE
    cc = jnp.zeros((mod_rows, d), F32).at[:b].set(c).at[b].set(c_ctx)
    for li in range(depth):
        last = li == depth - 1
        lam_init = 0.8 - 0.6 * math.exp(-0.3 * li)
        mod = _modulation(cc, w_mod, b_mod[li], li).reshape(mod_rows, 6, d)
        mod = jnp.pad(mod, ((0, 0), (0, SUBLANE - 6), (0, 0)))
        modtab = jnp.stack([jnp.broadcast_to(mod[b], (b, SUBLANE, d)), mod[:b]], axis=1)
        wa, wf = _pack_w_in(w_in[li])
        oa, of = _inproj(xs, modtab, norm_mix_g[li], wa, wf, tm, nct)
        grow = of[:, :, F_MG * LANE:F_MG * LANE + 4 * ML_HEADS]
        grow = grow.reshape(b, nch, CHUNK, 4 * ML_HEADS).transpose(0, 1, 3, 2)
        ml = _mlstm(oa, of, grow, ml_gate_b[li], ml_norm_g[li], nch_ctx, nch)
        df = _diff_attn(oa, of, rope, df_lambda[li], df_norm_g[li], tm, nct, n_ctx, lam_init)
        gl = _gla(oa, of, gl_w_alpha[li], gl_b_alpha[li], gl_norm_g[li], nch_ctx, nch)
        wr = jnp.zeros((d, LANE), F32).at[:, :N_GROUPS].set(router_group_w[li])
        wr = wr.at[:, N_GROUPS:N_GROUPS + N_EXPERTS].set(router_expert_w[li])
        wr_hi = wr.astype(BF16)
        wr = jnp.stack([wr_hi, (wr - wr_hi.astype(F32)).astype(BF16)])
        br = jnp.zeros((1, LANE), F32).at[0, :N_GROUPS].set(router_group_b[li])
        br = br.at[0, N_GROUPS:N_GROUPS + N_EXPERTS].set(router_expert_b[li])
        first_block = nct if last else 0
        xn, h2, rid, rw, cnt = _merge(ml, df, gl, of, w_branch[li].astype(BF16), w_out[li].astype(BF16), xs,
                                      modtab, norm_ffn_g[li], wr, br, tm, nct, first_block)
        so = xn.shape[1]
        t = b * so
        te, tv, pos, npad = _route_plan(rid.reshape(t, LANE), cnt)
        xsorted = _dispatch_rows(h2.reshape(t, d), pos, npad)
        ys = _gmm(te, tv, xsorted, moe_w_gate, moe_w_up, moe_w_down, li)
        y0 = jnp.take(ys, pos[:, 0], axis=0).reshape(b, so, d)
        y1 = jnp.take(ys, pos[:, 1], axis=0).reshape(b, so, d)
        xs = _combine(xn, y0, y1, rw, modtab, final_norm_g, tm, nct, first_block, last)
    return xs
```

```python
import functools
import math

import numpy as np
import jax
import jax.numpy as jnp
from jax import lax
from jax.experimental import pallas as pl
from jax.experimental.pallas import tpu as pltpu

F32 = jnp.float32
BF16 = jnp.bfloat16
HIGHEST = lax.Precision.HIGHEST

D_MODEL = 1024
GRID_W = 64
NORM_EPS = 1e-6
CHUNK = 64
ROPE_THETA = 10000.0
ML_HEADS, ML_DK, ML_DV = 4, 128, 128
DF_HEADS, DF_HD, DF_DV = 4, 64, 128
GL_HEADS, GL_DK, GL_DV, GL_RANK, GL_TAU = 4, 64, 128, 16, 16.0
N_BRANCH, BRANCH_W = 3, 512
N_GROUPS, EXPERTS_PER_GROUP, TOP_K, D_EXPERT = 4, 8, 2, 512
N_EXPERTS = N_GROUPS * EXPERTS_PER_GROUP
IN_SIZES = (
    ML_HEADS * ML_DK, ML_HEADS * ML_DK, ML_HEADS * ML_DV, ML_HEADS * ML_DV, 4 * ML_HEADS,
    DF_HEADS * 2 * DF_HD, DF_HEADS * 2 * DF_HD, DF_HEADS * DF_DV,
    GL_HEADS * GL_DK, GL_HEADS * GL_DK, GL_HEADS * GL_DV, GL_HEADS * GL_DV, 2 * GL_RANK,
    N_BRANCH * D_MODEL,
)

LANE = 128
SUBLANE = 8
VMEM_LIMIT = 56 * 1024 * 1024
MOE_TILE = 512

A_MQ, A_MK, A_MV, A_GQ, A_GK, A_GV, A_DV = 0, 4, 8, 12, 14, 16, 20
A_WIDTH = 24 * LANE
F_DQ, F_DK, F_MO, F_GG, F_GT, F_MG, F_GA = 0, 4, 8, 12, 16, 40, 41
F_WIDTH = 42 * LANE
HEAD_W = 4 * LANE


def _cparams(sem):
    return pltpu.CompilerParams(dimension_semantics=sem, vmem_limit_bytes=VMEM_LIMIT)


def _const_spec(shape):
    nd = len(shape)
    return pl.BlockSpec(shape, lambda *_: (0,) * nd, pipeline_mode=pl.Buffered(1))


def _logsig(x):
    return jnp.minimum(x, 0.0) - jnp.log1p(jnp.exp(-jnp.abs(x)))


def _dot(a, b):
    return jnp.dot(a, b, preferred_element_type=F32)


def _dot_nt(a, b):
    return lax.dot_general(a, b, (((1,), (1,)), ((), ())), preferred_element_type=F32)


def _dot_tn(a, b):
    return lax.dot_general(a, b, (((0,), (0,)), ((), ())), preferred_element_type=F32)


def _rms(x, g):
    return x * lax.rsqrt(jnp.mean(x * x, axis=-1, keepdims=True) + NORM_EPS) * g


def _mod_kernel(c_ref, w_ref, b_ref, o_ref):
    c = c_ref[...]
    s = c * jax.nn.sigmoid(c)
    o_ref[...] = jnp.dot(s, w_ref[...], precision=HIGHEST, preferred_element_type=F32) + b_ref[...]


def _modulation(cc, w_mod, b_mod, li):
    rows, d = cc.shape
    n = w_mod.shape[2]
    tn = 512
    return pl.pallas_call(
        _mod_kernel,
        out_shape=jax.ShapeDtypeStruct((rows, n), F32),
        grid=(n // tn,),
        in_specs=[pl.BlockSpec((rows, d), lambda j: (0, 0)),
                  pl.BlockSpec((None, d, tn), lambda j: (li, 0, j)),
                  pl.BlockSpec((1, tn), lambda j: (0, j))],
        out_specs=pl.BlockSpec((rows, tn), lambda j: (0, j)),
        compiler_params=_cparams(("parallel",)),
        name="adaln_mod",
    )(cc, w_mod, b_mod.reshape(1, n))


def _inproj_kernel(x_ref, mod_ref, g_ref, wa_ref, wf_ref, oa_ref, of_ref):
    y = _rms(x_ref[...], g_ref[...])
    h = (y * (1.0 + mod_ref[1:2, :]) + mod_ref[0:1, :]).astype(BF16)
    oa_ref[...] = _dot(h, wa_ref[...]).astype(BF16)
    of_ref[...] = _dot(h, wf_ref[...])


def _inproj(xs, modtab, g, wa, wf, tm, nct):
    b, s, d = xs.shape
    kind = lambda i: jnp.where(i >= nct, 1, 0)
    return pl.pallas_call(
        _inproj_kernel,
        out_shape=(jax.ShapeDtypeStruct((b, s, A_WIDTH), BF16), jax.ShapeDtypeStruct((b, s, F_WIDTH), F32)),
        grid=(b, s // tm),
        in_specs=[pl.BlockSpec((None, tm, d), lambda bi, i: (bi, i, 0)),
                  pl.BlockSpec((None, None, SUBLANE, d), lambda bi, i: (bi, kind(i), 0, 0)),
                  _const_spec((1, d)), _const_spec((d, A_WIDTH)), _const_spec((d, F_WIDTH))],
        out_specs=(pl.BlockSpec((None, tm, A_WIDTH), lambda bi, i: (bi, i, 0)),
                   pl.BlockSpec((None, tm, F_WIDTH), lambda bi, i: (bi, i, 0))),
        compiler_params=_cparams(("parallel", "parallel")),
        name="in_proj",
    )(xs, modtab, g.reshape(1, d), wa, wf)


def _chunk_ids(t, nct, nch):
    return t, jnp.where(t < nct, nct - 1 - t, nch - 1 - (t - nct))


def _head_cols(h):
    return slice(h * LANE, (h + 1) * LANE)


def _transpose_chunks(src_ref, dst_ref, nch):
    eye = (lax.broadcasted_iota(jnp.int32, (LANE, LANE), 0)
           == lax.broadcasted_iota(jnp.int32, (LANE, LANE), 1)).astype(BF16)

    def body(c, carry):
        blk = src_ref[pl.ds(pl.multiple_of(c * CHUNK, CHUNK), CHUNK), :]
        for h in range(HEAD_W // LANE):
            dst_ref[c, h] = _dot_nt(eye, blk[:, _head_cols(h)]).astype(BF16)
        return carry

    lax.fori_loop(0, nch, body, 0, unroll=2)


def _mlstm_kernel(q_ref, k_ref, v_ref, o_ref, gcol_ref, grow_ref, bcol_ref, brow_ref, ng_ref, out_ref,
                  hf_ref, hb_ref, ct_ref, kt_ref, *, nct, nch):
    L = CHUNK
    H = ML_HEADS
    _transpose_chunks(k_ref, kt_ref, nch)
    jj = lax.broadcasted_iota(jnp.int32, (L, L), 0)
    ii = lax.broadcasted_iota(jnp.int32, (L, L), 1)
    lane = lax.broadcasted_iota(jnp.int32, (L, LANE), 1)
    ones_col = jnp.where(lane == 0, 1.0, 0.0).astype(BF16)
    scale = ML_DK ** -0.5
    vis = ((ii <= jj), (ii >= jj))
    cum = tuple(jnp.where(m_, 1.0, 0.0).astype(BF16) for m_ in vis)
    cum_t = tuple(jnp.where(m_, 1.0, 0.0).astype(BF16) for m_ in ((jj <= ii), (jj >= ii)))
    ct_ref[...] = jnp.zeros(ct_ref.shape, F32)

    def split2(x):
        hi = x.astype(BF16)
        return hi, (x - hi.astype(F32)).astype(BF16)

    def step(t, carry):
        chunks = _chunk_ids(t, nct, nch)
        rows = [pl.ds(pl.multiple_of(c * L, L), L) for c in chunks]
        chains = [(d, h) for d in range(2) for h in range(H)]
        q, kt, vext, ct, sq, qc = {}, {}, {}, {}, {}, {}
        for d, h in chains:
            q[d, h] = q_ref[rows[d], _head_cols(h)]
            kt[d, h] = kt_ref[chunks[d], h]
            vext[d, h] = jnp.concatenate([v_ref[rows[d], _head_cols(h)], ones_col], axis=1)
            ct[d, h] = ct_ref[d, h]
            sq[d, h] = _dot(q[d, h], kt[d, h])
            qc[d, h] = _dot(q[d, h], ct[d, h].astype(BF16))
        gr, bc_all, br_all = [], [], []
        for d in range(2):
            gc = gcol_ref[rows[d], :] + bcol_ref[...]
            gr.append(grow_ref[chunks[d]] + brow_ref[...])
            chi, clo = split2(_logsig(gc))
            rhi, rlo = split2(_logsig(gr[d]))
            bc_all.append(_dot(cum[d], chi) + _dot(cum[d], clo))
            br_all.append(_dot(rhi, cum_t[d]) + _dot(rlo, cum_t[d]))
        w, w_inter, einv, decay, kw, new = {}, {}, {}, {}, {}, []
        for d, h in chains:
            m = carry[d * H + h]
            gi = 2 * d * H + h
            b_row = br_all[d][gi + H:gi + H + 1, :]
            b_col = bc_all[d][:, gi + H:gi + H + 1]
            b_last = b_row[:, L - 1:L] if d == 0 else b_row[:, 0:1]
            u_row = gr[d][gi:gi + 1, :] - b_row
            g_col = jnp.maximum(jnp.max(jnp.where(vis[d], u_row, -jnp.inf), axis=1, keepdims=True), m)
            w[d, h] = jnp.exp(jnp.where(vis[d], u_row - g_col, -jnp.inf)) * scale
            w_inter[d, h] = jnp.exp(m - g_col)
            einv[d, h] = jnp.exp(-(b_col + g_col))
            m_new = b_last + jnp.maximum(m, jnp.max(u_row, axis=1, keepdims=True))
            wk_row = jnp.exp(b_last + u_row - m_new) * scale
            decay[d, h] = jnp.exp(b_last + m - m_new)
            kw[d, h] = (kt[d, h].astype(F32) * wk_row).astype(BF16)
            new.append(m_new)
        for d, h in chains:
            ct_ref[d, h] = decay[d, h] * ct[d, h] + _dot(kw[d, h], vext[d, h])
        sv = {}
        for d, h in chains:
            sv[d, h] = _dot((sq[d, h] * w[d, h]).astype(BF16), vext[d, h])
        for d, h in chains:
            num = sv[d, h][:, :ML_DV] + w_inter[d, h] * qc[d, h][:, :ML_DV]
            den = sv[d, h][:, ML_DV:ML_DV + 1] + w_inter[d, h] * qc[d, h][:, ML_DV:ML_DV + 1]
            hh = num / jnp.maximum(jnp.abs(den), einv[d, h])
            if d == 0:
                hf_ref[rows[d], _head_cols(h)] = hh
            else:
                hb_ref[rows[d], _head_cols(h)] = hh
        return tuple(new)

    zero = jnp.zeros((1, 1), F32)
    lax.fori_loop(0, nch, step, (zero,) * (2 * H), unroll=2)
    for h in range(H):
        y = _rms(hf_ref[:, _head_cols(h)] + hb_ref[:, _head_cols(h)], ng_ref[...])
        out_ref[:, _head_cols(h)] = (y * jax.nn.sigmoid(o_ref[:, _head_cols(h)])).astype(BF16)


def _mlstm(oa, of, grow, gate_b, norm_g, nct, nch):
    b, s, _ = oa.shape
    bcol = jnp.zeros((1, LANE), F32).at[0, :4 * ML_HEADS].set(gate_b)
    brow = gate_b.reshape(4 * ML_HEADS, 1)
    sect = lambda base: (lambda bi: (bi, 0, base // 4))
    return pl.pallas_call(
        functools.partial(_mlstm_kernel, nct=nct, nch=nch),
        out_shape=jax.ShapeDtypeStruct((b, s, HEAD_W), BF16),
        grid=(b,),
        in_specs=[pl.BlockSpec((None, s, HEAD_W), sect(A_MQ)),
                  pl.BlockSpec((None, s, HEAD_W), sect(A_MK)),
                  pl.BlockSpec((None, s, HEAD_W), sect(A_MV)),
                  pl.BlockSpec((None, s, HEAD_W), sect(F_MO)),
                  pl.BlockSpec((None, s, LANE), lambda bi: (bi, 0, F_MG)),
                  pl.BlockSpec((None, nch, 4 * ML_HEADS, CHUNK), lambda bi: (bi, 0, 0, 0)),
                  pl.BlockSpec((1, LANE), lambda bi: (0, 0)),
                  pl.BlockSpec((4 * ML_HEADS, 1), lambda bi: (0, 0)),
                  pl.BlockSpec((1, ML_DV), lambda bi: (0, 0))],
        out_specs=pl.BlockSpec((None, s, HEAD_W), lambda bi: (bi, 0, 0)),
        scratch_shapes=[pltpu.VMEM((s, HEAD_W), F32), pltpu.VMEM((s, HEAD_W), F32),
                        pltpu.VMEM((2, ML_HEADS, ML_DK, 2 * LANE), F32),
                        pltpu.VMEM((nch, ML_HEADS, ML_DK, CHUNK), BF16)],
        compiler_params=_cparams(("parallel",)),
        name="mlstm",
    )(oa, oa, oa, of, of, grow, bcol, brow, norm_g.reshape(1, ML_DV))


def _gla_tables():
    L = CHUNK
    G = np.zeros((2, 7 * L, L), np.float32)
    lvl = np.full((2, L, L), 7, np.int32)
    for d in range(2):
        for p in range(L):
            if d == 0:
                G[d, p, :p + 1] = 1
            else:
                G[d, p, p:] = 1
        for li, s in enumerate((32, 16, 8, 4, 2, 1)):
            for p in range(L):
                base = (p // (2 * s)) * 2 * s
                row = (li + 1) * L + p
                if d == 0:
                    mid = base + s
                    if p >= mid:
                        G[d, row, mid + 1:p + 1] = 1
                    else:
                        G[d, row, p + 1:mid + 1] = 1
                else:
                    mid = base + s - 1
                    if p <= mid:
                        G[d, row, p:mid] = 1
                    else:
                        G[d, row, mid:p] = 1
            blk = np.arange(L) // (2 * s)
            upper = (np.arange(L) % (2 * s)) >= s
            same = blk[:, None] == blk[None, :]
            if d == 0:
                sel = same & upper[:, None] & ~upper[None, :]
            else:
                sel = same & ~upper[:, None] & upper[None, :]
            lvl[d][sel] = li
        lvl[d][np.arange(L), np.arange(L)] = 6
    return G, lvl


GL_PAIRS = GL_HEADS * GL_DK // LANE


def _gla_kernel(q_ref, k_ref, v_ref, g_ref, a_ref, wa_ref, ba_ref, gm_ref, lvl_ref, ng_ref, out_ref,
                acc_ref, la_ref, st_ref, vt_ref, *, nct, nch):
    L = CHUNK
    _transpose_chunks(v_ref, vt_ref, nch)
    for d in range(2):
        pre = jnp.dot(a_ref[...], wa_ref[d], precision=HIGHEST, preferred_element_type=F32) + ba_ref[d]
        la_ref[d] = _logsig(pre) * (1.0 / GL_TAU)
    acc_ref[...] = jnp.zeros(acc_ref.shape, F32)
    st_ref[...] = jnp.zeros(st_ref.shape, F32)
    first = lax.broadcasted_iota(jnp.int32, (L, LANE), 1) < GL_DK
    first2 = lax.broadcasted_iota(jnp.int32, (GL_DV, LANE), 1) < GL_DK

    def split(x):
        return jnp.concatenate([jnp.where(first, x, 0.0), jnp.where(first, 0.0, x)], axis=0).astype(BF16)

    def step(t, carry):
        chunks = _chunk_ids(t, nct, nch)
        rows = [pl.ds(pl.multiple_of(c * L, L), L) for c in chunks]
        chains = [(d, p) for d in range(2) for p in range(GL_PAIRS)]
        xs = []
        for d in range(2):
            lac = la_ref[d, rows[d], :]
            hi = lac.astype(BF16)
            lo = (lac - hi.astype(F32)).astype(BF16)
            xs.append(_dot(gm_ref[d], hi) + _dot(gm_ref[d], lo))
        q, k, cs, tot, st, inter, upd = {}, {}, {}, {}, {}, {}, {}
        for d, p in chains:
            q[d, p] = q_ref[rows[d], _head_cols(p)].astype(F32) * (GL_DK ** -0.5)
            k[d, p] = k_ref[rows[d], _head_cols(p)].astype(F32)
            cs[d, p] = xs[d][0:L, _head_cols(p)]
            tot[d, p] = cs[d, p][L - 1:L] if d == 0 else cs[d, p][0:1]
            st[d, p] = st_ref[d, p]
        for d, p in chains:
            inter[d, p] = _dot_nt(split(q[d, p] * jnp.exp(cs[d, p])), st[d, p].astype(BF16))
            ke = (k[d, p] * jnp.exp(tot[d, p] - cs[d, p])).astype(BF16)
            u = [_dot(vt_ref[chunks[d], 2 * p + hh], ke) for hh in range(2)]
            st_ref[d, p] = st[d, p] * jnp.exp(tot[d, p]) + jnp.where(first2, u[0], u[1])
        amat = {}
        for d, p in chains:
            amat[d, p] = jnp.where(lvl_ref[d] == 6, _dot_nt(split(q[d, p]), k[d, p].astype(BF16)), 0.0)
        for li in range(6):
            for d, p in chains:
                e = jnp.exp(xs[d][(li + 1) * L:(li + 2) * L, _head_cols(p)])
                lev = _dot_nt(split(q[d, p] * e), (k[d, p] * e).astype(BF16))
                amat[d, p] = jnp.where(lvl_ref[d] == li, lev, amat[d, p])
        for d, p in chains:
            a = amat[d, p].astype(BF16)
            for hh in range(2):
                cols = _head_cols(2 * p + hh)
                acc_ref[rows[d], cols] += (_dot(a[hh * L:(hh + 1) * L], v_ref[rows[d], cols])
                                           + inter[d, p][hh * L:(hh + 1) * L])
        return carry

    lax.fori_loop(0, nch, step, 0, unroll=2)
    for h in range(GL_HEADS):
        cols = _head_cols(h)
        g = g_ref[:, cols]
        out_ref[:, cols] = (_rms(acc_ref[:, cols], ng_ref[...]) * (g * jax.nn.sigmoid(g))).astype(BF16)


def _gla(oa, of, w_alpha, b_alpha, norm_g, nct, nch):
    b, s, _ = oa.shape
    gmat, lvl = _gla_tables()
    lvl = np.concatenate([lvl, lvl], axis=1)
    qk_w = GL_HEADS * GL_DK
    wa = jnp.zeros((2, LANE, qk_w), F32)
    for d in range(2):
        wa = wa.at[d, d * GL_RANK:(d + 1) * GL_RANK, :].set(w_alpha[d])
    ba = b_alpha.reshape(2, 1, qk_w)
    full = lambda nd: (lambda bi: (0,) * nd)
    return pl.pallas_call(
        functools.partial(_gla_kernel, nct=nct, nch=nch),
        out_shape=jax.ShapeDtypeStruct((b, s, HEAD_W), BF16),
        grid=(b,),
        in_specs=[pl.BlockSpec((None, s, qk_w), lambda bi: (bi, 0, A_GQ * LANE // qk_w)),
                  pl.BlockSpec((None, s, qk_w), lambda bi: (bi, 0, A_GK * LANE // qk_w)),
                  pl.BlockSpec((None, s, HEAD_W), lambda bi: (bi, 0, A_GV // 4)),
                  pl.BlockSpec((None, s, HEAD_W), lambda bi: (bi, 0, F_GG // 4)),
                  pl.BlockSpec((None, s, LANE), lambda bi: (bi, 0, F_GA)),
                  pl.BlockSpec((2, LANE, qk_w), full(3)),
                  pl.BlockSpec((2, 1, qk_w), full(3)),
                  pl.BlockSpec((2, 7 * CHUNK, CHUNK), full(3)),
                  pl.BlockSpec((2, 2 * CHUNK, CHUNK), full(3)),
                  pl.BlockSpec((1, GL_DV), full(2))],
        out_specs=pl.BlockSpec((None, s, HEAD_W), lambda bi: (bi, 0, 0)),
        scratch_shapes=[pltpu.VMEM((s, HEAD_W), F32), pltpu.VMEM((2, s, qk_w), F32),
                        pltpu.VMEM((2, GL_PAIRS, GL_DV, LANE), F32),
                        pltpu.VMEM((nch, GL_HEADS, GL_DV, CHUNK), BF16)],
        compiler_params=_cparams(("parallel",)),
        name="gla",
    )(oa, oa, oa, of, of, wa, ba, jnp.asarray(gmat, BF16), jnp.asarray(lvl), norm_g.reshape(1, GL_DV))


def _rope_tables(n):
    t = np.arange(n)
    n_freq = DF_HD // 4
    inv = jnp.asarray(ROPE_THETA, F32) ** (-jnp.arange(n_freq, dtype=F32) / n_freq)
    ang_r = jnp.asarray(t // GRID_W, F32)[:, None] * inv
    ang_c = jnp.asarray(t % GRID_W, F32)[:, None] * inv
    ang = jnp.concatenate([ang_r, ang_r, ang_c, ang_c] * 2, axis=1)
    first = (np.arange(LANE) % 32) < 16
    cos, sin = jnp.cos(ang), jnp.sin(ang)
    return cos, jnp.where(first, -sin, 0.0), jnp.where(first, 0.0, sin)


DF_ROW_GROUPS = 4


def _rope(x, cos, sa, sb):
    return x * cos + pltpu.roll(x, LANE - 16, 1) * sa + pltpu.roll(x, 16, 1) * sb


def _diff_kernel(q_ref, k_ref, v_ref, cq_ref, saq_ref, sbq_ref, ck_ref, sak_ref, sbk_ref, lam_ref, ng_ref,
                 out_ref, kr_ref, *, nct, n_ctx, lam_init):
    s = k_ref.shape[0]
    tq = q_ref.shape[0]
    i = pl.program_id(2)

    @pl.when(i == 0)
    def _():
        kr_ref[0:n_ctx, :] = k_ref[0:n_ctx, :].astype(BF16)
        kr_ref[n_ctx:s, :] = _rope(k_ref[n_ctx:s, :], ck_ref[...], sak_ref[...], sbk_ref[...]).astype(BF16)

    lp = lam_ref[...]
    lam = (jnp.exp(jnp.sum(lp[0:1] * lp[1:2], axis=1, keepdims=True))
           - jnp.exp(jnp.sum(lp[2:3] * lp[3:4], axis=1, keepdims=True)) + lam_init)
    lane = lax.broadcasted_iota(jnp.int32, q_ref.shape, 1)

    def attend(qb, nk):
        qb = qb * (DF_HD ** -0.5)
        qs = jnp.concatenate([jnp.where(lane < DF_HD, qb, 0.0), jnp.where(lane >= DF_HD, qb, 0.0)],
                             axis=0).astype(BF16)
        rg = 2 * tq // DF_ROW_GROUPS
        scs = [_dot_nt(qs[g * rg:(g + 1) * rg], kr_ref[0:nk, :]) for g in range(DF_ROW_GROUPS)]
        ovs = []
        for sc in scs:
            p = jnp.exp(sc - jnp.max(sc, axis=1, keepdims=True))
            rl = 1.0 / jnp.sum(p, axis=1, keepdims=True)
            ovs.append((p.astype(BF16), rl))
        ov = jnp.concatenate([_dot(p, v_ref[0:nk, :]) * rl for p, rl in ovs], axis=0)
        o = ov[0:tq] - lam * ov[tq:2 * tq]
        out_ref[...] = (_rms(o, ng_ref[...]) * (1.0 - lam_init)).astype(BF16)

    @pl.when(i < nct)
    def _():
        attend(q_ref[...], n_ctx)

    @pl.when(i >= nct)
    def _():
        attend(_rope(q_ref[...], cq_ref[...], saq_ref[...], sbq_ref[...]), s)


def _diff_attn(oa, of, rope, df_lambda, norm_g, tq, nct, n_ctx, lam_init):
    b, s, _ = oa.shape
    n = s - n_ctx
    cos, sa, sb = rope
    lam_p = jnp.zeros((4, LANE), F32).at[:, :DF_HD].set(df_lambda)
    qblk = lambda bi, h, i: (jnp.maximum(i - nct, 0), 0)
    full = lambda bi, h, i: (0, 0)
    return pl.pallas_call(
        functools.partial(_diff_kernel, nct=nct, n_ctx=n_ctx, lam_init=lam_init),
        out_shape=jax.ShapeDtypeStruct((b, s, DF_HEADS * DF_DV), BF16),
        grid=(b, DF_HEADS, s // tq),
        in_specs=[pl.BlockSpec((None, tq, LANE), lambda bi, h, i: (bi, i, F_DQ + h)),
                  pl.BlockSpec((None, s, LANE), lambda bi, h, i: (bi, 0, F_DK + h)),
                  pl.BlockSpec((None, s, LANE), lambda bi, h, i: (bi, 0, A_DV + h)),
                  pl.BlockSpec((tq, LANE), qblk), pl.BlockSpec((tq, LANE), qblk), pl.BlockSpec((tq, LANE), qblk),
                  pl.BlockSpec((n, LANE), full), pl.BlockSpec((n, LANE), full), pl.BlockSpec((n, LANE), full),
                  pl.BlockSpec((4, LANE), full), pl.BlockSpec((1, DF_DV), full)],
        out_specs=pl.BlockSpec((None, tq, LANE), lambda bi, h, i: (bi, i, h)),
        scratch_shapes=[pltpu.VMEM((s, LANE), BF16)],
        compiler_params=_cparams(("parallel", "parallel", "arbitrary")),
        name="diff_attn",
    )(of, of, oa, cos, sa, sb, cos, sa, sb, lam_p, norm_g.reshape(1, DF_DV))


def _merge_kernel(ml_ref, df_ref, gl_ref, g0_ref, g1_ref, g2_ref, wb_ref, wo_ref, x_ref, mod_ref, nf_ref,
                  wr_ref, br_ref, tri_ref, xo_ref, h2_ref, rid_ref, rw_ref, cnt_ref):
    y = (jax.nn.sigmoid(g0_ref[...]) * _dot(ml_ref[...], wb_ref[0])
         + jax.nn.sigmoid(g1_ref[...]) * _dot(df_ref[...], wb_ref[1])
         + jax.nn.sigmoid(g2_ref[...]) * _dot(gl_ref[...], wb_ref[2]))
    xn = x_ref[...] + mod_ref[2:3, :] * _dot(y.astype(BF16), wo_ref[...])
    xo_ref[...] = xn
    h2 = _rms(xn, nf_ref[...]) * (1.0 + mod_ref[4:5, :]) + mod_ref[3:4, :]
    h2_hi = h2.astype(BF16)
    h2_ref[...] = h2_hi
    h2_lo = (h2 - h2_hi.astype(F32)).astype(BF16)
    logits = (_dot(h2_hi, wr_ref[0]) + (_dot(h2_lo, wr_ref[0]) + _dot(h2_hi, wr_ref[1]))) + br_ref[...]
    lane = lax.broadcasted_iota(jnp.int32, logits.shape, 1)
    lane_f = lane.astype(F32)
    neg = -jnp.inf

    def first_max(vals):
        mx = jnp.max(vals, axis=1, keepdims=True)
        return mx, jnp.min(jnp.where(vals == mx, lane_f, float(LANE)), axis=1, keepdims=True)

    is_grp = lane < N_GROUPS
    gmax, gidx = first_max(jnp.where(is_grp, logits, neg))
    pg_top = 1.0 / jnp.sum(jnp.where(is_grp, jnp.exp(logits - gmax), 0.0), axis=1, keepdims=True)
    lo = N_GROUPS + gidx * EXPERTS_PER_GROUP
    in_grp = (lane_f >= lo) & (lane_f < lo + EXPERTS_PER_GROUP)
    le = jnp.where(in_grp, logits, neg)
    m1, e1 = first_max(le)
    m2, e2 = first_max(jnp.where(lane_f == e1, neg, le))
    r = jnp.exp(m2 - m1)
    w1 = pg_top / (1.0 + r)
    w2 = pg_top * r / (1.0 + r)

    @pl.when((pl.program_id(0) == 0) & (pl.program_id(1) == 0))
    def _():
        cnt_ref[...] = jnp.zeros(cnt_ref.shape, F32)

    hot1 = jnp.where(lane_f == e1, 1.0, 0.0)
    hot2 = jnp.where(lane_f == e2, 1.0, 0.0)
    tot1 = jnp.sum(hot1, axis=0, keepdims=True)
    tot2 = jnp.sum(hot2, axis=0, keepdims=True)
    cnt = cnt_ref[...]
    before1 = cnt + _dot(tri_ref[...], hot1.astype(BF16))
    before2 = cnt + tot1 + _dot(tri_ref[...], hot2.astype(BF16))
    rank1 = jnp.sum(hot1 * before1, axis=1, keepdims=True)
    rank2 = jnp.sum(hot2 * before2, axis=1, keepdims=True)
    cnt_ref[...] = cnt + tot1 + tot2
    rid = jnp.where(lane == 0, e1 - N_GROUPS, jnp.where(lane == 1, e2 - N_GROUPS,
                    jnp.where(lane == 2, rank1, jnp.where(lane == 3, rank2, 0.0))))
    rid_ref[...] = rid.astype(jnp.int32)
    rw_ref[...] = jnp.where(lane == 0, w1, jnp.where(lane == 1, w2, 0.0))


def _merge(ml, df, gl, of, wb, wo, xs, modtab, nf, wr, br, tm, nct, first_block):
    b, s, d = xs.shape
    nb = s // tm - first_block
    so = nb * tm
    kind = lambda i: jnp.where(i + first_block >= nct, 1, 0)
    row = lambda bi, i: (bi, i + first_block, 0)
    gate = lambda br_: (lambda bi, i: (bi, i + first_block, F_GT // (d // LANE) + br_))
    outrow = lambda bi, i: (bi, i, 0)
    tri = jnp.asarray(np.tril(np.ones((tm, tm), np.float32), -1), BF16)
    return pl.pallas_call(
        _merge_kernel,
        out_shape=(jax.ShapeDtypeStruct((b, so, d), F32), jax.ShapeDtypeStruct((b, so, d), BF16),
                   jax.ShapeDtypeStruct((b, so, LANE), jnp.int32), jax.ShapeDtypeStruct((b, so, LANE), F32),
                   jax.ShapeDtypeStruct((1, LANE), F32)),
        grid=(b, nb),
        in_specs=[pl.BlockSpec((None, tm, BRANCH_W), row), pl.BlockSpec((None, tm, BRANCH_W), row),
                  pl.BlockSpec((None, tm, BRANCH_W), row),
                  pl.BlockSpec((None, tm, d), gate(0)), pl.BlockSpec((None, tm, d), gate(1)),
                  pl.BlockSpec((None, tm, d), gate(2)),
                  _const_spec((N_BRANCH, BRANCH_W, d)), _const_spec((d, d)),
                  pl.BlockSpec((None, tm, d), row),
                  pl.BlockSpec((None, None, SUBLANE, d), lambda bi, i: (bi, kind(i), 0, 0)),
                  _const_spec((1, d)), _const_spec((2, d, LANE)), _const_spec((1, LANE)), _const_spec((tm, tm))],
        out_specs=(pl.BlockSpec((None, tm, d), outrow), pl.BlockSpec((None, tm, d), outrow),
                   pl.BlockSpec((None, tm, LANE), outrow), pl.BlockSpec((None, tm, LANE), outrow),
                   pl.BlockSpec((1, LANE), lambda bi, i: (0, 0))),
        compiler_params=_cparams(("arbitrary", "arbitrary")),
        name="merge_route",
    )(ml, df, gl, of, of, of, wb, wo, xs, modtab, nf.reshape(1, d), wr, br, tri)


def _gmm_kernel(te_ref, tv_ref, x_ref, wg_ref, wu_ref, wd_ref, y_ref, wgb_ref, wub_ref, wdb_ref):
    i = pl.program_id(0)

    @pl.when((i == 0) | (te_ref[i] != te_ref[jnp.maximum(i - 1, 0)]))
    def _():
        wgb_ref[...] = wg_ref[...].astype(BF16)
        wub_ref[...] = wu_ref[...].astype(BF16)
        wdb_ref[...] = wd_ref[...].astype(BF16)

    @pl.when(tv_ref[i] > 0)
    def _():
        x = x_ref[...]
        a = _dot(x, wgb_ref[...])
        hid = (a * jax.nn.sigmoid(a)) * _dot(x, wub_ref[...])
        y_ref[...] = _dot(hid.astype(BF16), wdb_ref[...]).astype(y_ref.dtype)

    @pl.when(tv_ref[i] == 0)
    def _():
        y_ref[...] = jnp.zeros(y_ref.shape, y_ref.dtype)


def _gmm(tile_expert, tile_valid, xs, wg, wu, wd, li):
    npad, d = xs.shape
    tm = MOE_TILE
    de = wg.shape[3]
    return pl.pallas_call(
        _gmm_kernel,
        out_shape=jax.ShapeDtypeStruct((npad, d), BF16),
        grid_spec=pltpu.PrefetchScalarGridSpec(
            num_scalar_prefetch=2,
            grid=(npad // tm,),
            in_specs=[pl.BlockSpec((tm, d), lambda i, te, tv: (i, 0)),
                      pl.BlockSpec((None, None, d, de), lambda i, te, tv: (li, te[i], 0, 0)),
                      pl.BlockSpec((None, None, d, de), lambda i, te, tv: (li, te[i], 0, 0)),
                      pl.BlockSpec((None, None, de, d), lambda i, te, tv: (li, te[i], 0, 0))],
            out_specs=pl.BlockSpec((tm, d), lambda i, te, tv: (i, 0)),
            scratch_shapes=[pltpu.VMEM((d, de), BF16), pltpu.VMEM((d, de), BF16), pltpu.VMEM((de, d), BF16)]),
        compiler_params=_cparams(("arbitrary",)),
        name="moe_gmm",
    )(tile_expert, tile_valid, xs, wg, wu, wd)


def _route_plan(rid, cnt):
    t = rid.shape[0]
    tm = MOE_TILE
    n_tiles = (TOP_K * t + tm - 1) // tm + N_EXPERTS
    counts = cnt[0, N_GROUPS:N_GROUPS + N_EXPERTS].astype(jnp.int32)
    tiles_e = (counts + tm - 1) // tm
    tile_end = jnp.cumsum(tiles_e)
    pad_start = (tile_end - tiles_e) * tm
    eid, rank = rid[:, 0:TOP_K], rid[:, TOP_K:2 * TOP_K]
    pos = rank + jnp.sum(jnp.where(eid[:, :, None] == jnp.arange(N_EXPERTS), pad_start, 0), axis=-1)
    tile = jnp.arange(n_tiles, dtype=jnp.int32)
    tile_expert = jnp.minimum(jnp.sum(tile[:, None] >= tile_end[None, :], axis=1), N_EXPERTS - 1).astype(jnp.int32)
    tile_valid = (tile < tile_end[-1]).astype(jnp.int32)
    return tile_expert, tile_valid, pos.astype(jnp.int32), n_tiles * tm


def _combine_kernel(x_ref, y0_ref, y1_ref, rw_ref, mod_ref, fg_ref, o_ref, *, final):
    rw = rw_ref[...]
    y = rw[:, 0:1] * y0_ref[...].astype(F32) + rw[:, 1:2] * y1_ref[...].astype(F32)
    xn = x_ref[...] + mod_ref[5:6, :] * y
    o_ref[...] = _rms(xn, fg_ref[...]) if final else xn


def _combine(xs, y0, y1, rw, modtab, fg, tm, nct, first_block, final):
    b, s, d = xs.shape
    kind = lambda i: jnp.where(i + first_block >= nct, 1, 0)
    row = lambda bi, i: (bi, i, 0)
    return pl.pallas_call(
        functools.partial(_combine_kernel, final=final),
        out_shape=jax.ShapeDtypeStruct((b, s, d), F32),
        grid=(b, s // tm),
        in_specs=[pl.BlockSpec((None, tm, d), row), pl.BlockSpec((None, tm, d), row),
                  pl.BlockSpec((None, tm, d), row), pl.BlockSpec((None, tm, LANE), row),
                  pl.BlockSpec((None, None, SUBLANE, d), lambda bi, i: (bi, kind(i), 0, 0)),
                  _const_spec((1, d))],
        out_specs=pl.BlockSpec((None, tm, d), row),
        compiler_params=_cparams(("parallel", "parallel")),
        name="moe_combine",
    )(xs, y0, y1, rw, modtab, fg.reshape(1, d))


def _pack_w_in(w):
    d = w.shape[0]
    mq, mk, mv, mo, mg, dq, dk, dv, gq, gk, gv, gg, ga, gt = jnp.split(
        w, [int(i) for i in np.cumsum(IN_SIZES)[:-1]], axis=1)

    def pad_cols(t):
        return jnp.pad(t, ((0, 0), (0, LANE - t.shape[1])))

    wa = jnp.concatenate([mq, mk, mv, gq, gk, gv, dv], axis=1).astype(BF16)
    wf = jnp.concatenate([dq, dk, mo, gg, gt, pad_cols(mg), pad_cols(ga)], axis=1).astype(BF16)
    return wa, wf


def _dispatch_rows(h2, pos, npad):
    t = h2.shape[0]
    tok = jnp.arange(t, dtype=jnp.int32)
    src = jnp.zeros((npad,), jnp.int32).at[pos[:, 0]].set(tok).at[pos[:, 1]].set(tok)
    return jnp.take(h2, src, axis=0, mode="clip")


def kernel(x, c, ctx, c_ctx, w_mod, b_mod, norm_mix_g, norm_ffn_g, w_in, ml_gate_b, ml_norm_g, df_lambda,
           df_norm_g, gl_w_alpha, gl_b_alpha, gl_norm_g, w_branch, w_out, router_group_w, router_group_b,
           router_expert_w, router_expert_b, moe_w_gate, moe_w_up, moe_w_down, final_norm_g):
    b, n, d = x.shape
    n_ctx = ctx.shape[1]
    s = n_ctx + n
    depth = w_mod.shape[0]
    tm = 256 if n_ctx % 256 == 0 else 128
    assert n_ctx % tm == 0 and n % tm == 0 and n % GRID_W == 0 and n_ctx % CHUNK == 0 and d == D_MODEL
    nct = n_ctx // tm
    nch, nch_ctx = s // CHUNK, n_ctx // CHUNK
    rope = _rope_tables(n)
    xs = jnp.concatenate([ctx, x], axis=1)
    mod_rows = -(-(b + 1) // SUBLANE) * SUBLANE
    cc = jnp.zeros((mod_rows, d), F32).at[:b].set(c).at[b].set(c_ctx)
    for li in range(depth):
        last = li == depth - 1
        lam_init = 0.8 - 0.6 * math.exp(-0.3 * li)
        mod = _modulation(cc, w_mod, b_mod[li], li).reshape(mod_rows, 6, d)
        mod = jnp.pad(mod, ((0, 0), (0, SUBLANE - 6), (0, 0)))
        modtab = jnp.stack([jnp.broadcast_to(mod[b], (b, SUBLANE, d)), mod[:b]], axis=1)
        wa, wf = _pack_w_in(w_in[li])
        oa, of = _inproj(xs, modtab, norm_mix_g[li], wa, wf, tm, nct)
        grow = of[:, :, F_MG * LANE:F_MG * LANE + 4 * ML_HEADS]
        grow = grow.reshape(b, nch, CHUNK, 4 * ML_HEADS).transpose(0, 1, 3, 2)
        ml = _mlstm(oa, of, grow, ml_gate_b[li], ml_norm_g[li], nch_ctx, nch)
        df = _diff_attn(oa, of, rope, df_lambda[li], df_norm_g[li], tm, nct, n_ctx, lam_init)
        gl = _gla(oa, of, gl_w_alpha[li], gl_b_alpha[li], gl_norm_g[li], nch_ctx, nch)
        wr = jnp.zeros((d, LANE), F32).at[:, :N_GROUPS].set(router_group_w[li])
        wr = wr.at[:, N_GROUPS:N_GROUPS + N_EXPERTS].set(router_expert_w[li])
        wr_hi = wr.astype(BF16)
        wr = jnp.stack([wr_hi, (wr - wr_hi.astype(F32)).astype(BF16)])
        br = jnp.zeros((1, LANE), F32).at[0, :N_GROUPS].set(router_group_b[li])
        br = br.at[0, N_GROUPS:N_GROUPS + N_EXPERTS].set(router_expert_b[li])
        first_block = nct if last else 0
        xn, h2, rid, rw, cnt = _merge(ml, df, gl, of, w_branch[li].astype(BF16), w_out[li].astype(BF16), xs,
                                      modtab, norm_ffn_g[li], wr, br, tm, nct, first_block)
        so = xn.shape[1]
        t = b * so
        te, tv, pos, npad = _route_plan(rid.reshape(t, LANE), cnt)
        xsorted = _dispatch_rows(h2.reshape(t, d), pos, npad)
        ys = _gmm(te, tv, xsorted, moe_w_gate, moe_w_up, moe_w_down, li)
        y0 = jnp.take(ys, pos[:, 0], axis=0, mode="clip").reshape(b, so, d)
        y1 = jnp.take(ys, pos[:, 1], axis=0, mode="clip").reshape(b, so, d)
        xs = _combine(xn, y0, y1, rw, modtab, final_norm_g, tm, nct, first_block, last)
    return xs
```

```python
import functools
import math

import numpy as np
import jax
import jax.numpy as jnp
from jax import lax
from jax.experimental import pallas as pl
from jax.experimental.pallas import tpu as pltpu

F32 = jnp.float32
BF16 = jnp.bfloat16
HIGHEST = lax.Precision.HIGHEST

D_MODEL = 1024
GRID_W = 64
NORM_EPS = 1e-6
CHUNK = 64
ROPE_THETA = 10000.0
ML_HEADS, ML_DK, ML_DV = 4, 128, 128
DF_HEADS, DF_HD, DF_DV = 4, 64, 128
GL_HEADS, GL_DK, GL_DV, GL_RANK, GL_TAU = 4, 64, 128, 16, 16.0
N_BRANCH, BRANCH_W = 3, 512
N_GROUPS, EXPERTS_PER_GROUP, TOP_K, D_EXPERT = 4, 8, 2, 512
N_EXPERTS = N_GROUPS * EXPERTS_PER_GROUP
IN_SIZES = (
    ML_HEADS * ML_DK, ML_HEADS * ML_DK, ML_HEADS * ML_DV, ML_HEADS * ML_DV, 4 * ML_HEADS,
    DF_HEADS * 2 * DF_HD, DF_HEADS * 2 * DF_HD, DF_HEADS * DF_DV,
    GL_HEADS * GL_DK, GL_HEADS * GL_DK, GL_HEADS * GL_DV, GL_HEADS * GL_DV, 2 * GL_RANK,
    N_BRANCH * D_MODEL,
)

LANE = 128
SUBLANE = 8
VMEM_LIMIT = 56 * 1024 * 1024
MOE_TILE = 512

A_MQ, A_MK, A_MV, A_GQ, A_GK, A_GV, A_DV = 0, 4, 8, 12, 14, 16, 20
A_WIDTH = 24 * LANE
F_DQ, F_DK, F_MO, F_GG, F_GT, F_MG, F_GA = 0, 4, 8, 12, 16, 40, 41
F_WIDTH = 42 * LANE
HEAD_W = 4 * LANE


def _cparams(sem):
    return pltpu.CompilerParams(dimension_semantics=sem, vmem_limit_bytes=VMEM_LIMIT)


def _const_spec(shape):
    nd = len(shape)
    return pl.BlockSpec(shape, lambda *_: (0,) * nd, pipeline_mode=pl.Buffered(1))


def _logsig(x):
    return jnp.minimum(x, 0.0) - jnp.log1p(jnp.exp(-jnp.abs(x)))


def _dot(a, b):
    return jnp.dot(a, b, preferred_element_type=F32)


def _dot_nt(a, b):
    return lax.dot_general(a, b, (((1,), (1,)), ((), ())), preferred_element_type=F32)


def _dot_tn(a, b):
    return lax.dot_general(a, b, (((0,), (0,)), ((), ())), preferred_element_type=F32)


def _rms(x, g):
    return x * lax.rsqrt(jnp.mean(x * x, axis=-1, keepdims=True) + NORM_EPS) * g


def _mod_kernel(c_ref, w_ref, b_ref, o_ref):
    c = c_ref[...]
    s = c * jax.nn.sigmoid(c)
    o_ref[...] = jnp.dot(s, w_ref[...], precision=HIGHEST, preferred_element_type=F32) + b_ref[...]


def _modulation(cc, w_mod, b_mod, li):
    rows, d = cc.shape
    n = w_mod.shape[2]
    tn = 512
    return pl.pallas_call(
        _mod_kernel,
        out_shape=jax.ShapeDtypeStruct((rows, n), F32),
        grid=(n // tn,),
        in_specs=[pl.BlockSpec((rows, d), lambda j: (0, 0)),
                  pl.BlockSpec((None, d, tn), lambda j: (li, 0, j)),
                  pl.BlockSpec((1, tn), lambda j: (0, j))],
        out_specs=pl.BlockSpec((rows, tn), lambda j: (0, j)),
        compiler_params=_cparams(("parallel",)),
        name="adaln_mod",
    )(cc, w_mod, b_mod.reshape(1, n))


def _inproj_kernel(x_ref, mod_ref, g_ref, wa_ref, wf_ref, oa_ref, of_ref):
    y = _rms(x_ref[...], g_ref[...])
    h = (y * (1.0 + mod_ref[1:2, :]) + mod_ref[0:1, :]).astype(BF16)
    oa_ref[...] = _dot(h, wa_ref[...]).astype(BF16)
    of_ref[...] = _dot(h, wf_ref[...])


def _inproj(xs, modtab, g, wa, wf, tm, nct):
    b, s, d = xs.shape
    kind = lambda i: jnp.where(i >= nct, 1, 0)
    return pl.pallas_call(
        _inproj_kernel,
        out_shape=(jax.ShapeDtypeStruct((b, s, A_WIDTH), BF16), jax.ShapeDtypeStruct((b, s, F_WIDTH), F32)),
        grid=(b, s // tm),
        in_specs=[pl.BlockSpec((None, tm, d), lambda bi, i: (bi, i, 0)),
                  pl.BlockSpec((None, None, SUBLANE, d), lambda bi, i: (bi, kind(i), 0, 0)),
                  _const_spec((1, d)), _const_spec((d, A_WIDTH)), _const_spec((d, F_WIDTH))],
        out_specs=(pl.BlockSpec((None, tm, A_WIDTH), lambda bi, i: (bi, i, 0)),
                   pl.BlockSpec((None, tm, F_WIDTH), lambda bi, i: (bi, i, 0))),
        compiler_params=_cparams(("parallel", "parallel")),
        name="in_proj",
    )(xs, modtab, g.reshape(1, d), wa, wf)


def _chunk_ids(t, nct, nch):
    return t, jnp.where(t < nct, nct - 1 - t, nch - 1 - (t - nct))


def _head_cols(h):
    return slice(h * LANE, (h + 1) * LANE)


def _transpose_chunks(src_ref, dst_ref, nch):
    eye = (lax.broadcasted_iota(jnp.int32, (LANE, LANE), 0)
           == lax.broadcasted_iota(jnp.int32, (LANE, LANE), 1)).astype(BF16)

    def body(c, carry):
        blk = src_ref[pl.ds(pl.multiple_of(c * CHUNK, CHUNK), CHUNK), :]
        for h in range(HEAD_W // LANE):
            dst_ref[c, h] = _dot_nt(eye, blk[:, _head_cols(h)]).astype(BF16)
        return carry

    lax.fori_loop(0, nch, body, 0, unroll=2)


def _mlstm_kernel(q_ref, k_ref, v_ref, o_ref, gcol_ref, grow_ref, bcol_ref, brow_ref, ng_ref, out_ref,
                  hf_ref, hb_ref, ct_ref, kt_ref, *, nct, nch):
    L = CHUNK
    H = ML_HEADS
    _transpose_chunks(k_ref, kt_ref, nch)
    jj = lax.broadcasted_iota(jnp.int32, (L, L), 0)
    ii = lax.broadcasted_iota(jnp.int32, (L, L), 1)
    lane = lax.broadcasted_iota(jnp.int32, (L, LANE), 1)
    ones_col = jnp.where(lane == 0, 1.0, 0.0).astype(BF16)
    scale = ML_DK ** -0.5
    vis = ((ii <= jj), (ii >= jj))
    cum = tuple(jnp.where(m_, 1.0, 0.0).astype(BF16) for m_ in vis)
    cum_t = tuple(jnp.where(m_, 1.0, 0.0).astype(BF16) for m_ in ((jj <= ii), (jj >= ii)))
    ct_ref[...] = jnp.zeros(ct_ref.shape, F32)

    def split2(x):
        hi = x.astype(BF16)
        return hi, (x - hi.astype(F32)).astype(BF16)

    def step(t, carry):
        chunks = _chunk_ids(t, nct, nch)
        rows = [pl.ds(pl.multiple_of(c * L, L), L) for c in chunks]
        chains = [(d, h) for d in range(2) for h in range(H)]
        q, kt, vext, ct, sq, qc = {}, {}, {}, {}, {}, {}
        for d, h in chains:
            q[d, h] = q_ref[rows[d], _head_cols(h)]
            kt[d, h] = kt_ref[chunks[d], h]
            vext[d, h] = jnp.concatenate([v_ref[rows[d], _head_cols(h)], ones_col], axis=1)
            ct[d, h] = ct_ref[d, h]
            sq[d, h] = _dot(q[d, h], kt[d, h])
            qc[d, h] = _dot(q[d, h], ct[d, h].astype(BF16))
        gr, bc_all, br_all = [], [], []
        for d in range(2):
            gc = gcol_ref[rows[d], :] + bcol_ref[...]
            gr.append(grow_ref[chunks[d]] + brow_ref[...])
            chi, clo = split2(_logsig(gc))
            rhi, rlo = split2(_logsig(gr[d]))
            bc_all.append(_dot(cum[d], chi) + _dot(cum[d], clo))
            br_all.append(_dot(rhi, cum_t[d]) + _dot(rlo, cum_t[d]))
        w, w_inter, einv, decay, kw, new = {}, {}, {}, {}, {}, []
        for d, h in chains:
            m = carry[d * H + h]
            gi = 2 * d * H + h
            b_row = br_all[d][gi + H:gi + H + 1, :]
            b_col = bc_all[d][:, gi + H:gi + H + 1]
            b_last = b_row[:, L - 1:L] if d == 0 else b_row[:, 0:1]
            u_row = gr[d][gi:gi + 1, :] - b_row
            g_col = jnp.maximum(jnp.max(jnp.where(vis[d], u_row, -jnp.inf), axis=1, keepdims=True), m)
            w[d, h] = jnp.exp(jnp.where(vis[d], u_row - g_col, -jnp.inf)) * scale
            w_inter[d, h] = jnp.exp(m - g_col)
            einv[d, h] = jnp.exp(-(b_col + g_col))
            m_new = b_last + jnp.maximum(m, jnp.max(u_row, axis=1, keepdims=True))
            wk_row = jnp.exp(b_last + u_row - m_new) * scale
            decay[d, h] = jnp.exp(b_last + m - m_new)
            kw[d, h] = (kt[d, h].astype(F32) * wk_row).astype(BF16)
            new.append(m_new)
        for d, h in chains:
            ct_ref[d, h] = decay[d, h] * ct[d, h] + _dot(kw[d, h], vext[d, h])
        sv = {}
        for d, h in chains:
            sv[d, h] = _dot((sq[d, h] * w[d, h]).astype(BF16), vext[d, h])
        for d, h in chains:
            num = sv[d, h][:, :ML_DV] + w_inter[d, h] * qc[d, h][:, :ML_DV]
            den = sv[d, h][:, ML_DV:ML_DV + 1] + w_inter[d, h] * qc[d, h][:, ML_DV:ML_DV + 1]
            hh = num / jnp.maximum(jnp.abs(den), einv[d, h])
            if d == 0:
                hf_ref[rows[d], _head_cols(h)] = hh
            else:
                hb_ref[rows[d], _head_cols(h)] = hh
        return tuple(new)

    zero = jnp.zeros((1, 1), F32)
    lax.fori_loop(0, nch, step, (zero,) * (2 * H), unroll=2)
    for h in range(H):
        y = _rms(hf_ref[:, _head_cols(h)] + hb_ref[:, _head_cols(h)], ng_ref[...])
        out_ref[:, _head_cols(h)] = (y * jax.nn.sigmoid(o_ref[:, _head_cols(h)])).astype(BF16)


def _mlstm(oa, of, grow, gate_b, norm_g, nct, nch):
    b, s, _ = oa.shape
    bcol = jnp.zeros((1, LANE), F32).at[0, :4 * ML_HEADS].set(gate_b)
    brow = gate_b.reshape(4 * ML_HEADS, 1)
    sect = lambda base: (lambda bi: (bi, 0, base // 4))
    return pl.pallas_call(
        functools.partial(_mlstm_kernel, nct=nct, nch=nch),
        out_shape=jax.ShapeDtypeStruct((b, s, HEAD_W), BF16),
        grid=(b,),
        in_specs=[pl.BlockSpec((None, s, HEAD_W), sect(A_MQ)),
                  pl.BlockSpec((None, s, HEAD_W), sect(A_MK)),
                  pl.BlockSpec((None, s, HEAD_W), sect(A_MV)),
                  pl.BlockSpec((None, s, HEAD_W), sect(F_MO)),
                  pl.BlockSpec((None, s, LANE), lambda bi: (bi, 0, F_MG)),
                  pl.BlockSpec((None, nch, 4 * ML_HEADS, CHUNK), lambda bi: (bi, 0, 0, 0)),
                  pl.BlockSpec((1, LANE), lambda bi: (0, 0)),
                  pl.BlockSpec((4 * ML_HEADS, 1), lambda bi: (0, 0)),
                  pl.BlockSpec((1, ML_DV), lambda bi: (0, 0))],
        out_specs=pl.BlockSpec((None, s, HEAD_W), lambda bi: (bi, 0, 0)),
        scratch_shapes=[pltpu.VMEM((s, HEAD_W), F32), pltpu.VMEM((s, HEAD_W), F32),
                        pltpu.VMEM((2, ML_HEADS, ML_DK, 2 * LANE), F32),
                        pltpu.VMEM((nch, ML_HEADS, ML_DK, CHUNK), BF16)],
        compiler_params=_cparams(("parallel",)),
        name="mlstm",
    )(oa, oa, oa, of, of, grow, bcol, brow, norm_g.reshape(1, ML_DV))


def _gla_tables():
    L = CHUNK
    G = np.zeros((2, 7 * L, L), np.float32)
    lvl = np.full((2, L, L), 7, np.int32)
    for d in range(2):
        for p in range(L):
            if d == 0:
                G[d, p, :p + 1] = 1
            else:
                G[d, p, p:] = 1
        for li, s in enumerate((32, 16, 8, 4, 2, 1)):
            for p in range(L):
                base = (p // (2 * s)) * 2 * s
                row = (li + 1) * L + p
                if d == 0:
                    mid = base + s
                    if p >= mid:
                        G[d, row, mid + 1:p + 1] = 1
                    else:
                        G[d, row, p + 1:mid + 1] = 1
                else:
                    mid = base + s - 1
                    if p <= mid:
                        G[d, row, p:mid] = 1
                    else:
                        G[d, row, mid:p] = 1
            blk = np.arange(L) // (2 * s)
            upper = (np.arange(L) % (2 * s)) >= s
            same = blk[:, None] == blk[None, :]
            if d == 0:
                sel = same & upper[:, None] & ~upper[None, :]
            else:
                sel = same & ~upper[:, None] & upper[None, :]
            lvl[d][sel] = li
        lvl[d][np.arange(L), np.arange(L)] = 6
    return G, lvl


GL_PAIRS = GL_HEADS * GL_DK // LANE


def _gla_kernel(q_ref, k_ref, v_ref, g_ref, a_ref, wa_ref, ba_ref, gm_ref, lvl_ref, ng_ref, out_ref,
                acc_ref, la_ref, st_ref, vt_ref, *, nct, nch):
    L = CHUNK
    _transpose_chunks(v_ref, vt_ref, nch)
    for d in range(2):
        pre = jnp.dot(a_ref[...], wa_ref[d], precision=HIGHEST, preferred_element_type=F32) + ba_ref[d]
        la_ref[d] = _logsig(pre) * (1.0 / GL_TAU)
    acc_ref[...] = jnp.zeros(acc_ref.shape, F32)
    st_ref[...] = jnp.zeros(st_ref.shape, F32)
    first = lax.broadcasted_iota(jnp.int32, (L, LANE), 1) < GL_DK
    first2 = lax.broadcasted_iota(jnp.int32, (GL_DV, LANE), 1) < GL_DK

    def split(x):
        return jnp.concatenate([jnp.where(first, x, 0.0), jnp.where(first, 0.0, x)], axis=0).astype(BF16)

    def step(t, carry):
        chunks = _chunk_ids(t, nct, nch)
        rows = [pl.ds(pl.multiple_of(c * L, L), L) for c in chunks]
        chains = [(d, p) for d in range(2) for p in range(GL_PAIRS)]
        xs = []
        for d in range(2):
            lac = la_ref[d, rows[d], :]
            hi = lac.astype(BF16)
            lo = (lac - hi.astype(F32)).astype(BF16)
            xs.append(_dot(gm_ref[d], hi) + _dot(gm_ref[d], lo))
        q, k, cs, tot, st, inter, upd = {}, {}, {}, {}, {}, {}, {}
        for d, p in chains:
            q[d, p] = q_ref[rows[d], _head_cols(p)].astype(F32) * (GL_DK ** -0.5)
            k[d, p] = k_ref[rows[d], _head_cols(p)].astype(F32)
            cs[d, p] = xs[d][0:L, _head_cols(p)]
            tot[d, p] = cs[d, p][L - 1:L] if d == 0 else cs[d, p][0:1]
            st[d, p] = st_ref[d, p]
        for d, p in chains:
            inter[d, p] = _dot_nt(split(q[d, p] * jnp.exp(cs[d, p])), st[d, p].astype(BF16))
            ke = (k[d, p] * jnp.exp(tot[d, p] - cs[d, p])).astype(BF16)
            u = [_dot(vt_ref[chunks[d], 2 * p + hh], ke) for hh in range(2)]
            st_ref[d, p] = st[d, p] * jnp.exp(tot[d, p]) + jnp.where(first2, u[0], u[1])
        amat = {}
        for d, p in chains:
            amat[d, p] = jnp.where(lvl_ref[d] == 6, _dot_nt(split(q[d, p]), k[d, p].astype(BF16)), 0.0)
        for li in range(6):
            for d, p in chains:
                e = jnp.exp(xs[d][(li + 1) * L:(li + 2) * L, _head_cols(p)])
                lev = _dot_nt(split(q[d, p] * e), (k[d, p] * e).astype(BF16))
                amat[d, p] = jnp.where(lvl_ref[d] == li, lev, amat[d, p])
        for d, p in chains:
            a = amat[d, p].astype(BF16)
            for hh in range(2):
                cols = _head_cols(2 * p + hh)
                acc_ref[rows[d], cols] += (_dot(a[hh * L:(hh + 1) * L], v_ref[rows[d], cols])
                                           + inter[d, p][hh * L:(hh + 1) * L])
        return carry

    lax.fori_loop(0, nch, step, 0, unroll=2)
    for h in range(GL_HEADS):
        cols = _head_cols(h)
        g = g_ref[:, cols]
        out_ref[:, cols] = (_rms(acc_ref[:, cols], ng_ref[...]) * (g * jax.nn.sigmoid(g))).astype(BF16)


def _gla(oa, of, w_alpha, b_alpha, norm_g, nct, nch):
    b, s, _ = oa.shape
    gmat, lvl = _gla_tables()
    lvl = np.concatenate([lvl, lvl], axis=1)
    qk_w = GL_HEADS * GL_DK
    wa = jnp.zeros((2, LANE, qk_w), F32)
    for d in range(2):
        wa = wa.at[d, d * GL_RANK:(d + 1) * GL_RANK, :].set(w_alpha[d])
    ba = b_alpha.reshape(2, 1, qk_w)
    full = lambda nd: (lambda bi: (0,) * nd)
    return pl.pallas_call(
        functools.partial(_gla_kernel, nct=nct, nch=nch),
        out_shape=jax.ShapeDtypeStruct((b, s, HEAD_W), BF16),
        grid=(b,),
        in_specs=[pl.BlockSpec((None, s, qk_w), lambda bi: (bi, 0, A_GQ * LANE // qk_w)),
                  pl.BlockSpec((None, s, qk_w), lambda bi: (bi, 0, A_GK * LANE // qk_w)),
                  pl.BlockSpec((None, s, HEAD_W), lambda bi: (bi, 0, A_GV // 4)),
                  pl.BlockSpec((None, s, HEAD_W), lambda bi: (bi, 0, F_GG // 4)),
                  pl.BlockSpec((None, s, LANE), lambda bi: (bi, 0, F_GA)),
                  pl.BlockSpec((2, LANE, qk_w), full(3)),
                  pl.BlockSpec((2, 1, qk_w), full(3)),
                  pl.BlockSpec((2, 7 * CHUNK, CHUNK), full(3)),
                  pl.BlockSpec((2, 2 * CHUNK, CHUNK), full(3)),
                  pl.BlockSpec((1, GL_DV), full(2))],
        out_specs=pl.BlockSpec((None, s, HEAD_W), lambda bi: (bi, 0, 0)),
        scratch_shapes=[pltpu.VMEM((s, HEAD_W), F32), pltpu.VMEM((2, s, qk_w), F32),
                        pltpu.VMEM((2, GL_PAIRS, GL_DV, LANE), F32),
                        pltpu.VMEM((nch, GL_HEADS, GL_DV, CHUNK), BF16)],
        compiler_params=_cparams(("parallel",)),
        name="gla",
    )(oa, oa, oa, of, of, wa, ba, jnp.asarray(gmat, BF16), jnp.asarray(lvl), norm_g.reshape(1, GL_DV))


def _rope_tables(n):
    t = np.arange(n)
    n_freq = DF_HD // 4
    inv = jnp.asarray(ROPE_THETA, F32) ** (-jnp.arange(n_freq, dtype=F32) / n_freq)
    ang_r = jnp.asarray(t // GRID_W, F32)[:, None] * inv
    ang_c = jnp.asarray(t % GRID_W, F32)[:, None] * inv
    ang = jnp.concatenate([ang_r, ang_r, ang_c, ang_c] * 2, axis=1)
    first = (np.arange(LANE) % 32) < 16
    cos, sin = jnp.cos(ang), jnp.sin(ang)
    return cos, jnp.where(first, -sin, 0.0), jnp.where(first, 0.0, sin)


DF_ROW_GROUPS = 4


def _rope(x, cos, sa, sb):
    return x * cos + pltpu.roll(x, LANE - 16, 1) * sa + pltpu.roll(x, 16, 1) * sb


def _diff_kernel(q_ref, k_ref, v_ref, cq_ref, saq_ref, sbq_ref, ck_ref, sak_ref, sbk_ref, lam_ref, ng_ref,
                 out_ref, kr_ref, *, nct, n_ctx, lam_init):
    s = k_ref.shape[0]
    tq = q_ref.shape[0]
    i = pl.program_id(2)

    @pl.when(i == 0)
    def _():
        kr_ref[0:n_ctx, :] = k_ref[0:n_ctx, :].astype(BF16)
        kr_ref[n_ctx:s, :] = _rope(k_ref[n_ctx:s, :], ck_ref[...], sak_ref[...], sbk_ref[...]).astype(BF16)

    lp = lam_ref[...]
    lam = (jnp.exp(jnp.sum(lp[0:1] * lp[1:2], axis=1, keepdims=True))
           - jnp.exp(jnp.sum(lp[2:3] * lp[3:4], axis=1, keepdims=True)) + lam_init)
    lane = lax.broadcasted_iota(jnp.int32, q_ref.shape, 1)

    def attend(qb, nk):
        qb = qb * (DF_HD ** -0.5)
        qs = jnp.concatenate([jnp.where(lane < DF_HD, qb, 0.0), jnp.where(lane >= DF_HD, qb, 0.0)],
                             axis=0).astype(BF16)
        rg = 2 * tq // DF_ROW_GROUPS
        scs = [_dot_nt(qs[g * rg:(g + 1) * rg], kr_ref[0:nk, :]) for g in range(DF_ROW_GROUPS)]
        ovs = []
        for sc in scs:
            p = jnp.exp(sc - jnp.max(sc, axis=1, keepdims=True))
            rl = 1.0 / jnp.sum(p, axis=1, keepdims=True)
            ovs.append((p.astype(BF16), rl))
        ov = jnp.concatenate([_dot(p, v_ref[0:nk, :]) * rl for p, rl in ovs], axis=0)
        o = ov[0:tq] - lam * ov[tq:2 * tq]
        out_ref[...] = (_rms(o, ng_ref[...]) * (1.0 - lam_init)).astype(BF16)

    @pl.when(i < nct)
    def _():
        attend(q_ref[...], n_ctx)

    @pl.when(i >= nct)
    def _():
        attend(_rope(q_ref[...], cq_ref[...], saq_ref[...], sbq_ref[...]), s)


def _diff_attn(oa, of, rope, df_lambda, norm_g, tq, nct, n_ctx, lam_init):
    b, s, _ = oa.shape
    n = s - n_ctx
    cos, sa, sb = rope
    lam_p = jnp.zeros((4, LANE), F32).at[:, :DF_HD].set(df_lambda)
    qblk = lambda bi, h, i: (jnp.maximum(i - nct, 0), 0)
    full = lambda bi, h, i: (0, 0)
    return pl.pallas_call(
        functools.partial(_diff_kernel, nct=nct, n_ctx=n_ctx, lam_init=lam_init),
        out_shape=jax.ShapeDtypeStruct((b, s, DF_HEADS * DF_DV), BF16),
        grid=(b, DF_HEADS, s // tq),
        in_specs=[pl.BlockSpec((None, tq, LANE), lambda bi, h, i: (bi, i, F_DQ + h)),
                  pl.BlockSpec((None, s, LANE), lambda bi, h, i: (bi, 0, F_DK + h)),
                  pl.BlockSpec((None, s, LANE), lambda bi, h, i: (bi, 0, A_DV + h)),
                  pl.BlockSpec((tq, LANE), qblk), pl.BlockSpec((tq, LANE), qblk), pl.BlockSpec((tq, LANE), qblk),
                  pl.BlockSpec((n, LANE), full), pl.BlockSpec((n, LANE), full), pl.BlockSpec((n, LANE), full),
                  pl.BlockSpec((4, LANE), full), pl.BlockSpec((1, DF_DV), full)],
        out_specs=pl.BlockSpec((None, tq, LANE), lambda bi, h, i: (bi, i, h)),
        scratch_shapes=[pltpu.VMEM((s, LANE), BF16)],
        compiler_params=_cparams(("parallel", "parallel", "arbitrary")),
        name="diff_attn",
    )(of, of, oa, cos, sa, sb, cos, sa, sb, lam_p, norm_g.reshape(1, DF_DV))


def _merge_kernel(ml_ref, df_ref, gl_ref, g0_ref, g1_ref, g2_ref, wb_ref, wo_ref, x_ref, mod_ref, nf_ref,
                  wr_ref, br_ref, tri_ref, xo_ref, h2_ref, rid_ref, rw_ref, cnt_ref):
    y = (jax.nn.sigmoid(g0_ref[...]) * _dot(ml_ref[...], wb_ref[0])
         + jax.nn.sigmoid(g1_ref[...]) * _dot(df_ref[...], wb_ref[1])
         + jax.nn.sigmoid(g2_ref[...]) * _dot(gl_ref[...], wb_ref[2]))
    xn = x_ref[...] + mod_ref[2:3, :] * _dot(y.astype(BF16), wo_ref[...])
    xo_ref[...] = xn
    h2 = _rms(xn, nf_ref[...]) * (1.0 + mod_ref[4:5, :]) + mod_ref[3:4, :]
    h2_hi = h2.astype(BF16)
    h2_ref[...] = h2_hi
    h2_lo = (h2 - h2_hi.astype(F32)).astype(BF16)
    logits = (_dot(h2_hi, wr_ref[0]) + (_dot(h2_lo, wr_ref[0]) + _dot(h2_hi, wr_ref[1]))) + br_ref[...]
    lane = lax.broadcasted_iota(jnp.int32, logits.shape, 1)
    lane_f = lane.astype(F32)
    neg = -jnp.inf

    def first_max(vals):
        mx = jnp.max(vals, axis=1, keepdims=True)
        return mx, jnp.min(jnp.where(vals == mx, lane_f, float(LANE)), axis=1, keepdims=True)

    is_grp = lane < N_GROUPS
    gmax, gidx = first_max(jnp.where(is_grp, logits, neg))
    pg_top = 1.0 / jnp.sum(jnp.where(is_grp, jnp.exp(logits - gmax), 0.0), axis=1, keepdims=True)
    lo = N_GROUPS + gidx * EXPERTS_PER_GROUP
    in_grp = (lane_f >= lo) & (lane_f < lo + EXPERTS_PER_GROUP)
    le = jnp.where(in_grp, logits, neg)
    m1, e1 = first_max(le)
    m2, e2 = first_max(jnp.where(lane_f == e1, neg, le))
    r = jnp.exp(m2 - m1)
    w1 = pg_top / (1.0 + r)
    w2 = pg_top * r / (1.0 + r)

    @pl.when((pl.program_id(0) == 0) & (pl.program_id(1) == 0))
    def _():
        cnt_ref[...] = jnp.zeros(cnt_ref.shape, F32)

    hot1 = jnp.where(lane_f == e1, 1.0, 0.0)
    hot2 = jnp.where(lane_f == e2, 1.0, 0.0)
    tot1 = jnp.sum(hot1, axis=0, keepdims=True)
    tot2 = jnp.sum(hot2, axis=0, keepdims=True)
    cnt = cnt_ref[...]
    before1 = cnt + _dot(tri_ref[...], hot1.astype(BF16))
    before2 = cnt + tot1 + _dot(tri_ref[...], hot2.astype(BF16))
    rank1 = jnp.sum(hot1 * before1, axis=1, keepdims=True)
    rank2 = jnp.sum(hot2 * before2, axis=1, keepdims=True)
    cnt_ref[...] = cnt + tot1 + tot2
    rid = jnp.where(lane == 0, e1 - N_GROUPS, jnp.where(lane == 1, e2 - N_GROUPS,
                    jnp.where(lane == 2, rank1, jnp.where(lane == 3, rank2, 0.0))))
    rid_ref[...] = rid.astype(jnp.int32)
    rw_ref[...] = jnp.where(lane == 0, w1, jnp.where(lane == 1, w2, 0.0))


def _merge(ml, df, gl, of, wb, wo, xs, modtab, nf, wr, br, tm, nct, first_block):
    b, s, d = xs.shape
    nb = s // tm - first_block
    so = nb * tm
    kind = lambda i: jnp.where(i + first_block >= nct, 1, 0)
    row = lambda bi, i: (bi, i + first_block, 0)
    gate = lambda br_: (lambda bi, i: (bi, i + first_block, F_GT // (d // LANE) + br_))
    outrow = lambda bi, i: (bi, i, 0)
    tri = jnp.asarray(np.tril(np.ones((tm, tm), np.float32), -1), BF16)
    return pl.pallas_call(
        _merge_kernel,
        out_shape=(jax.ShapeDtypeStruct((b, so, d), F32), jax.ShapeDtypeStruct((b, so, d), BF16),
                   jax.ShapeDtypeStruct((b, so, LANE), jnp.int32), jax.ShapeDtypeStruct((b, so, LANE), F32),
                   jax.ShapeDtypeStruct((1, LANE), F32)),
        grid=(b, nb),
        in_specs=[pl.BlockSpec((None, tm, BRANCH_W), row), pl.BlockSpec((None, tm, BRANCH_W), row),
                  pl.BlockSpec((None, tm, BRANCH_W), row),
                  pl.BlockSpec((None, tm, d), gate(0)), pl.BlockSpec((None, tm, d), gate(1)),
                  pl.BlockSpec((None, tm, d), gate(2)),
                  _const_spec((N_BRANCH, BRANCH_W, d)), _const_spec((d, d)),
                  pl.BlockSpec((None, tm, d), row),
                  pl.BlockSpec((None, None, SUBLANE, d), lambda bi, i: (bi, kind(i), 0, 0)),
                  _const_spec((1, d)), _const_spec((2, d, LANE)), _const_spec((1, LANE)), _const_spec((tm, tm))],
        out_specs=(pl.BlockSpec((None, tm, d), outrow), pl.BlockSpec((None, tm, d), outrow),
                   pl.BlockSpec((None, tm, LANE), outrow), pl.BlockSpec((None, tm, LANE), outrow),
                   pl.BlockSpec((1, LANE), lambda bi, i: (0, 0))),
        compiler_params=_cparams(("arbitrary", "arbitrary")),
        name="merge_route",
    )(ml, df, gl, of, of, of, wb, wo, xs, modtab, nf.reshape(1, d), wr, br, tri)


def _gmm_kernel(te_ref, tv_ref, x_ref, wg_ref, wu_ref, wd_ref, y_ref, wgb_ref, wub_ref, wdb_ref):
    i = pl.program_id(0)

    @pl.when((i == 0) | (te_ref[i] != te_ref[jnp.maximum(i - 1, 0)]))
    def _():
        wgb_ref[...] = wg_ref[...].astype(BF16)
        wub_ref[...] = wu_ref[...].astype(BF16)
        wdb_ref[...] = wd_ref[...].astype(BF16)

    @pl.when(tv_ref[i] > 0)
    def _():
        x = x_ref[...]
        a = _dot(x, wgb_ref[...])
        hid = (a * jax.nn.sigmoid(a)) * _dot(x, wub_ref[...])
        y_ref[...] = _dot(hid.astype(BF16), wdb_ref[...]).astype(y_ref.dtype)

    @pl.when(tv_ref[i] == 0)
    def _():
        y_ref[...] = jnp.zeros(y_ref.shape, y_ref.dtype)


def _gmm(tile_expert, tile_valid, xs, wg, wu, wd, li):
    npad, d = xs.shape
    tm = MOE_TILE
    de = wg.shape[3]
    return pl.pallas_call(
        _gmm_kernel,
        out_shape=jax.ShapeDtypeStruct((npad, d), BF16),
        grid_spec=pltpu.PrefetchScalarGridSpec(
            num_scalar_prefetch=2,
            grid=(npad // tm,),
            in_specs=[pl.BlockSpec((tm, d), lambda i, te, tv: (i, 0)),
                      pl.BlockSpec((None, None, d, de), lambda i, te, tv: (li, te[i], 0, 0)),
                      pl.BlockSpec((None, None, d, de), lambda i, te, tv: (li, te[i], 0, 0)),
                      pl.BlockSpec((None, None, de, d), lambda i, te, tv: (li, te[i], 0, 0))],
            out_specs=pl.BlockSpec((tm, d), lambda i, te, tv: (i, 0)),
            scratch_shapes=[pltpu.VMEM((d, de), BF16), pltpu.VMEM((d, de), BF16), pltpu.VMEM((de, d), BF16)]),
        compiler_params=_cparams(("arbitrary",)),
        name="moe_gmm",
    )(tile_expert, tile_valid, xs, wg, wu, wd)


def _route_plan(rid, cnt):
    t = rid.shape[0]
    tm = MOE_TILE
    n_tiles = (TOP_K * t + tm - 1) // tm + N_EXPERTS
    counts = cnt[0, N_GROUPS:N_GROUPS + N_EXPERTS].astype(jnp.int32)
    tiles_e = (counts + tm - 1) // tm
    tile_end = jnp.cumsum(tiles_e)
    pad_start = (tile_end - tiles_e) * tm
    eid, rank = rid[:, 0:TOP_K], rid[:, TOP_K:2 * TOP_K]
    pos = rank + jnp.sum(jnp.where(eid[:, :, None] == jnp.arange(N_EXPERTS), pad_start, 0), axis=-1)
    tile = jnp.arange(n_tiles, dtype=jnp.int32)
    tile_expert = jnp.minimum(jnp.sum(tile[:, None] >= tile_end[None, :], axis=1), N_EXPERTS - 1).astype(jnp.int32)
    tile_valid = (tile < tile_end[-1]).astype(jnp.int32)
    return tile_expert, tile_valid, pos.astype(jnp.int32), n_tiles * tm


def _combine_kernel(x_ref, y0_ref, y1_ref, rw_ref, mod_ref, fg_ref, o_ref, *, final):
    rw = rw_ref[...]
    y = rw[:, 0:1] * y0_ref[...].astype(F32) + rw[:, 1:2] * y1_ref[...].astype(F32)
    xn = x_ref[...] + mod_ref[5:6, :] * y
    o_ref[...] = _rms(xn, fg_ref[...]) if final else xn


def _combine(xs, y0, y1, rw, modtab, fg, tm, nct, first_block, final):
    b, s, d = xs.shape
    kind = lambda i: jnp.where(i + first_block >= nct, 1, 0)
    row = lambda bi, i: (bi, i, 0)
    return pl.pallas_call(
        functools.partial(_combine_kernel, final=final),
        out_shape=jax.ShapeDtypeStruct((b, s, d), F32),
        grid=(b, s // tm),
        in_specs=[pl.BlockSpec((None, tm, d), row), pl.BlockSpec((None, tm, d), row),
                  pl.BlockSpec((None, tm, d), row), pl.BlockSpec((None, tm, LANE), row),
                  pl.BlockSpec((None, None, SUBLANE, d), lambda bi, i: (bi, kind(i), 0, 0)),
                  _const_spec((1, d))],
        out_specs=pl.BlockSpec((None, tm, d), row),
        compiler_params=_cparams(("parallel", "parallel")),
        name="moe_combine",
    )(xs, y0, y1, rw, modtab, fg.reshape(1, d))


def _pack_w_in(w):
    d = w.shape[0]
    mq, mk, mv, mo, mg, dq, dk, dv, gq, gk, gv, gg, ga, gt = jnp.split(
        w, [int(i) for i in np.cumsum(IN_SIZES)[:-1]], axis=1)

    def pad_cols(t):
        return jnp.pad(t, ((0, 0), (0, LANE - t.shape[1])))

    wa = jnp.concatenate([mq, mk, mv, gq, gk, gv, dv], axis=1).astype(BF16)
    wf = jnp.concatenate([dq, dk, mo, gg, gt, pad_cols(mg), pad_cols(ga)], axis=1).astype(BF16)
    return wa, wf


def _dispatch_rows(h2, pos, npad):
    t = h2.shape[0]
    tok = jnp.repeat(jnp.arange(t, dtype=jnp.int32), TOP_K)
    src = jnp.zeros((npad,), jnp.int32).at[pos.reshape(-1)].set(tok, unique_indices=True, mode="promise_in_bounds")
    return jnp.take(h2, src, axis=0, mode="clip")


def kernel(x, c, ctx, c_ctx, w_mod, b_mod, norm_mix_g, norm_ffn_g, w_in, ml_gate_b, ml_norm_g, df_lambda,
           df_norm_g, gl_w_alpha, gl_b_alpha, gl_norm_g, w_branch, w_out, router_group_w, router_group_b,
           router_expert_w, router_expert_b, moe_w_gate, moe_w_up, moe_w_down, final_norm_g):
    b, n, d = x.shape
    n_ctx = ctx.shape[1]
    s = n_ctx + n
    depth = w_mod.shape[0]
    tm = 256 if n_ctx % 256 == 0 else 128
    assert n_ctx % tm == 0 and n % tm == 0 and n % GRID_W == 0 and n_ctx % CHUNK == 0 and d == D_MODEL
    nct = n_ctx // tm
    nch, nch_ctx = s // CHUNK, n_ctx // CHUNK
    rope = _rope_tables(n)
    xs = jnp.concatenate([ctx, x], axis=1)
    mod_rows = -(-(b + 1) // SUBLANE) * SUBLANE
    cc = jnp.zeros((mod_rows, d), F32).at[:b].set(c).at[b].set(c_ctx)
    for li in range(depth):
        last = li == depth - 1
        lam_init = 0.8 - 0.6 * math.exp(-0.3 * li)
        mod = _modulation(cc, w_mod, b_mod[li], li).reshape(mod_rows, 6, d)
        mod = jnp.pad(mod, ((0, 0), (0, SUBLANE - 6), (0, 0)))
        modtab = jnp.stack([jnp.broadcast_to(mod[b], (b, SUBLANE, d)), mod[:b]], axis=1)
        wa, wf = _pack_w_in(w_in[li])
        oa, of = _inproj(xs, modtab, norm_mix_g[li], wa, wf, tm, nct)
        grow = of[:, :, F_MG * LANE:F_MG * LANE + 4 * ML_HEADS]
        grow = grow.reshape(b, nch, CHUNK, 4 * ML_HEADS).transpose(0, 1, 3, 2)
        ml = _mlstm(oa, of, grow, ml_gate_b[li], ml_norm_g[li], nch_ctx, nch)
        df = _diff_attn(oa, of, rope, df_lambda[li], df_norm_g[li], tm, nct, n_ctx, lam_init)
        gl = _gla(oa, of, gl_w_alpha[li], gl_b_alpha[li], gl_norm_g[li], nch_ctx, nch)
        wr = jnp.zeros((d, LANE), F32).at[:, :N_GROUPS].set(router_group_w[li])
        wr = wr.at[:, N_GROUPS:N_GROUPS + N_EXPERTS].set(router_expert_w[li])
        wr_hi = wr.astype(BF16)
        wr = jnp.stack([wr_hi, (wr - wr_hi.astype(F32)).astype(BF16)])
        br = jnp.zeros((1, LANE), F32).at[0, :N_GROUPS].set(router_group_b[li])
        br = br.at[0, N_GROUPS:N_GROUPS + N_EXPERTS].set(router_expert_b[li])
        first_block = nct if last else 0
        xn, h2, rid, rw, cnt = _merge(ml, df, gl, of, w_branch[li].astype(BF16), w_out[li].astype(BF16), xs,
                                      modtab, norm_ffn_g[li], wr, br, tm, nct, first_block)
        so = xn.shape[1]
        t = b * so
        te, tv, pos, npad = _route_plan(rid.reshape(t, LANE), cnt)
        xsorted = _dispatch_rows(h2.reshape(t, d), pos, npad)
        ys = _gmm(te, tv, xsorted, moe_w_gate, moe_w_up, moe_w_down, li)
        y0 = jnp.take(ys, pos[:, 0], axis=0, mode="clip").reshape(b, so, d)
        y1 = jnp.take(ys, pos[:, 1], axis=0, mode="clip").reshape(b, so, d)
        xs = _combine(xn, y0, y1, rw, modtab, final_norm_g, tm, nct, first_block, last)
    return xs
```

```python
import functools
import math

import numpy as np
import jax
import jax.numpy as jnp
from jax import lax
from jax.experimental import pallas as pl
from jax.experimental.pallas import tpu as pltpu

F32 = jnp.float32
BF16 = jnp.bfloat16
HIGHEST = lax.Precision.HIGHEST

D_MODEL = 1024
GRID_W = 64
NORM_EPS = 1e-6
CHUNK = 64
ROPE_THETA = 10000.0
ML_HEADS, ML_DK, ML_DV = 4, 128, 128
DF_HEADS, DF_HD, DF_DV = 4, 64, 128
GL_HEADS, GL_DK, GL_DV, GL_RANK, GL_TAU = 4, 64, 128, 16, 16.0
N_BRANCH, BRANCH_W = 3, 512
N_GROUPS, EXPERTS_PER_GROUP, TOP_K, D_EXPERT = 4, 8, 2, 512
N_EXPERTS = N_GROUPS * EXPERTS_PER_GROUP
IN_SIZES = (
    ML_HEADS * ML_DK, ML_HEADS * ML_DK, ML_HEADS * ML_DV, ML_HEADS * ML_DV, 4 * ML_HEADS,
    DF_HEADS * 2 * DF_HD, DF_HEADS * 2 * DF_HD, DF_HEADS * DF_DV,
    GL_HEADS * GL_DK, GL_HEADS * GL_DK, GL_HEADS * GL_DV, GL_HEADS * GL_DV, 2 * GL_RANK,
    N_BRANCH * D_MODEL,
)

LANE = 128
SUBLANE = 8
VMEM_LIMIT = 56 * 1024 * 1024
MOE_TILE = 512

A_MQ, A_MK, A_MV, A_GQ, A_GK, A_GV, A_DV = 0, 4, 8, 12, 14, 16, 20
A_WIDTH = 24 * LANE
F_DQ, F_DK, F_MO, F_GG, F_GT, F_MG, F_GA = 0, 4, 8, 12, 16, 40, 41
F_WIDTH = 42 * LANE
HEAD_W = 4 * LANE


def _cparams(sem):
    return pltpu.CompilerParams(dimension_semantics=sem, vmem_limit_bytes=VMEM_LIMIT)


def _const_spec(shape):
    nd = len(shape)
    return pl.BlockSpec(shape, lambda *_: (0,) * nd, pipeline_mode=pl.Buffered(1))


def _logsig(x):
    return jnp.minimum(x, 0.0) - jnp.log1p(jnp.exp(-jnp.abs(x)))


def _dot(a, b):
    return jnp.dot(a, b, preferred_element_type=F32)


def _dot_nt(a, b):
    return lax.dot_general(a, b, (((1,), (1,)), ((), ())), preferred_element_type=F32)


def _dot_tn(a, b):
    return lax.dot_general(a, b, (((0,), (0,)), ((), ())), preferred_element_type=F32)


def _rms(x, g):
    return x * lax.rsqrt(jnp.mean(x * x, axis=-1, keepdims=True) + NORM_EPS) * g


def _mod_kernel(c_ref, w_ref, b_ref, o_ref):
    c = c_ref[...]
    s = c * jax.nn.sigmoid(c)
    o_ref[...] = jnp.dot(s, w_ref[...], precision=HIGHEST, preferred_element_type=F32) + b_ref[...]


def _modulation(cc, w_mod, b_mod, li):
    rows, d = cc.shape
    n = w_mod.shape[2]
    tn = 512
    return pl.pallas_call(
        _mod_kernel,
        out_shape=jax.ShapeDtypeStruct((rows, n), F32),
        grid=(n // tn,),
        in_specs=[pl.BlockSpec((rows, d), lambda j: (0, 0)),
                  pl.BlockSpec((None, d, tn), lambda j: (li, 0, j)),
                  pl.BlockSpec((1, tn), lambda j: (0, j))],
        out_specs=pl.BlockSpec((rows, tn), lambda j: (0, j)),
        compiler_params=_cparams(("parallel",)),
        name="adaln_mod",
    )(cc, w_mod, b_mod.reshape(1, n))


def _inproj_kernel(x_ref, mod_ref, g_ref, wa_ref, wf_ref, oa_ref, of_ref):
    y = _rms(x_ref[...], g_ref[...])
    h = (y * (1.0 + mod_ref[1:2, :]) + mod_ref[0:1, :]).astype(BF16)
    oa_ref[...] = _dot(h, wa_ref[...]).astype(BF16)
    of_ref[...] = _dot(h, wf_ref[...])


def _inproj(xs, modtab, g, wa, wf, tm, nct):
    b, s, d = xs.shape
    kind = lambda i: jnp.where(i >= nct, 1, 0)
    return pl.pallas_call(
        _inproj_kernel,
        out_shape=(jax.ShapeDtypeStruct((b, s, A_WIDTH), BF16), jax.ShapeDtypeStruct((b, s, F_WIDTH), F32)),
        grid=(b, s // tm),
        in_specs=[pl.BlockSpec((None, tm, d), lambda bi, i: (bi, i, 0)),
                  pl.BlockSpec((None, None, SUBLANE, d), lambda bi, i: (bi, kind(i), 0, 0)),
                  _const_spec((1, d)), _const_spec((d, A_WIDTH)), _const_spec((d, F_WIDTH))],
        out_specs=(pl.BlockSpec((None, tm, A_WIDTH), lambda bi, i: (bi, i, 0)),
                   pl.BlockSpec((None, tm, F_WIDTH), lambda bi, i: (bi, i, 0))),
        compiler_params=_cparams(("parallel", "parallel")),
        name="in_proj",
    )(xs, modtab, g.reshape(1, d), wa, wf)


def _chunk_ids(t, nct, nch):
    return t, jnp.where(t < nct, nct - 1 - t, nch - 1 - (t - nct))


def _head_cols(h):
    return slice(h * LANE, (h + 1) * LANE)


def _transpose_chunks(src_ref, dst_ref, nch):
    eye = (lax.broadcasted_iota(jnp.int32, (LANE, LANE), 0)
           == lax.broadcasted_iota(jnp.int32, (LANE, LANE), 1)).astype(BF16)

    def body(c, carry):
        blk = src_ref[pl.ds(pl.multiple_of(c * CHUNK, CHUNK), CHUNK), :]
        for h in range(HEAD_W // LANE):
            dst_ref[c, h] = _dot_nt(eye, blk[:, _head_cols(h)]).astype(BF16)
        return carry

    lax.fori_loop(0, nch, body, 0, unroll=2)


def _mlstm_kernel(q_ref, k_ref, v_ref, o_ref, gcol_ref, grow_ref, bcol_ref, brow_ref, ng_ref, out_ref,
                  hf_ref, hb_ref, ct_ref, kt_ref, *, nct, nch):
    L = CHUNK
    H = ML_HEADS
    _transpose_chunks(k_ref, kt_ref, nch)
    jj = lax.broadcasted_iota(jnp.int32, (L, L), 0)
    ii = lax.broadcasted_iota(jnp.int32, (L, L), 1)
    lane = lax.broadcasted_iota(jnp.int32, (L, LANE), 1)
    ones_col = jnp.where(lane == 0, 1.0, 0.0).astype(BF16)
    scale = ML_DK ** -0.5
    vis = ((ii <= jj), (ii >= jj))
    cum = tuple(jnp.where(m_, 1.0, 0.0).astype(BF16) for m_ in vis)
    cum_t = tuple(jnp.where(m_, 1.0, 0.0).astype(BF16) for m_ in ((jj <= ii), (jj >= ii)))
    ct_ref[...] = jnp.zeros(ct_ref.shape, F32)

    def split2(x):
        hi = x.astype(BF16)
        return hi, (x - hi.astype(F32)).astype(BF16)

    def step(t, carry):
        chunks = _chunk_ids(t, nct, nch)
        rows = [pl.ds(pl.multiple_of(c * L, L), L) for c in chunks]
        chains = [(d, h) for d in range(2) for h in range(H)]
        q, kt, vext, ct, sq, qc = {}, {}, {}, {}, {}, {}
        for d, h in chains:
            q[d, h] = q_ref[rows[d], _head_cols(h)]
            kt[d, h] = kt_ref[chunks[d], h]
            vext[d, h] = jnp.concatenate([v_ref[rows[d], _head_cols(h)], ones_col], axis=1)
            ct[d, h] = ct_ref[d, h]
            sq[d, h] = _dot(q[d, h], kt[d, h])
            qc[d, h] = _dot(q[d, h], ct[d, h].astype(BF16))
        gr, bc_all, br_all = [], [], []
        for d in range(2):
            gc = gcol_ref[rows[d], :] + bcol_ref[...]
            gr.append(grow_ref[chunks[d]] + brow_ref[...])
            chi, clo = split2(_logsig(gc))
            rhi, rlo = split2(_logsig(gr[d]))
            bc_all.append(_dot(cum[d], chi) + _dot(cum[d], clo))
            br_all.append(_dot(rhi, cum_t[d]) + _dot(rlo, cum_t[d]))
        w, w_inter, einv, decay, kw, new = {}, {}, {}, {}, {}, []
        for d, h in chains:
            m = carry[d * H + h]
            gi = 2 * d * H + h
            b_row = br_all[d][gi + H:gi + H + 1, :]
            b_col = bc_all[d][:, gi + H:gi + H + 1]
            b_last = b_row[:, L - 1:L] if d == 0 else b_row[:, 0:1]
            u_row = gr[d][gi:gi + 1, :] - b_row
            g_col = jnp.maximum(jnp.max(jnp.where(vis[d], u_row, -jnp.inf), axis=1, keepdims=True), m)
            w[d, h] = jnp.exp(jnp.where(vis[d], u_row - g_col, -jnp.inf)) * scale
            w_inter[d, h] = jnp.exp(m - g_col)
            einv[d, h] = jnp.exp(-(b_col + g_col))
            m_new = b_last + jnp.maximum(m, jnp.max(u_row, axis=1, keepdims=True))
            wk_row = jnp.exp(b_last + u_row - m_new) * scale
            decay[d, h] = jnp.exp(b_last + m - m_new)
            kw[d, h] = (kt[d, h].astype(F32) * wk_row).astype(BF16)
            new.append(m_new)
        for d, h in chains:
            ct_ref[d, h] = decay[d, h] * ct[d, h] + _dot(kw[d, h], vext[d, h])
        sv = {}
        for d, h in chains:
            sv[d, h] = _dot((sq[d, h] * w[d, h]).astype(BF16), vext[d, h])
        for d, h in chains:
            num = sv[d, h][:, :ML_DV] + w_inter[d, h] * qc[d, h][:, :ML_DV]
            den = sv[d, h][:, ML_DV:ML_DV + 1] + w_inter[d, h] * qc[d, h][:, ML_DV:ML_DV + 1]
            hh = num / jnp.maximum(jnp.abs(den), einv[d, h])
            if d == 0:
                hf_ref[rows[d], _head_cols(h)] = hh
            else:
                hb_ref[rows[d], _head_cols(h)] = hh
        return tuple(new)

    zero = jnp.zeros((1, 1), F32)
    lax.fori_loop(0, nch, step, (zero,) * (2 * H), unroll=2)
    for h in range(H):
        y = _rms(hf_ref[:, _head_cols(h)] + hb_ref[:, _head_cols(h)], ng_ref[...])
        out_ref[:, _head_cols(h)] = (y * jax.nn.sigmoid(o_ref[:, _head_cols(h)])).astype(BF16)


def _mlstm(oa, of, grow, gate_b, norm_g, nct, nch):
    b, s, _ = oa.shape
    bcol = jnp.zeros((1, LANE), F32).at[0, :4 * ML_HEADS].set(gate_b)
    brow = gate_b.reshape(4 * ML_HEADS, 1)
    sect = lambda base: (lambda bi: (bi, 0, base // 4))
    return pl.pallas_call(
        functools.partial(_mlstm_kernel, nct=nct, nch=nch),
        out_shape=jax.ShapeDtypeStruct((b, s, HEAD_W), BF16),
        grid=(b,),
        in_specs=[pl.BlockSpec((None, s, HEAD_W), sect(A_MQ)),
                  pl.BlockSpec((None, s, HEAD_W), sect(A_MK)),
                  pl.BlockSpec((None, s, HEAD_W), sect(A_MV)),
                  pl.BlockSpec((None, s, HEAD_W), sect(F_MO)),
                  pl.BlockSpec((None, s, LANE), lambda bi: (bi, 0, F_MG)),
                  pl.BlockSpec((None, nch, 4 * ML_HEADS, CHUNK), lambda bi: (bi, 0, 0, 0)),
                  pl.BlockSpec((1, LANE), lambda bi: (0, 0)),
                  pl.BlockSpec((4 * ML_HEADS, 1), lambda bi: (0, 0)),
                  pl.BlockSpec((1, ML_DV), lambda bi: (0, 0))],
        out_specs=pl.BlockSpec((None, s, HEAD_W), lambda bi: (bi, 0, 0)),
        scratch_shapes=[pltpu.VMEM((s, HEAD_W), F32), pltpu.VMEM((s, HEAD_W), F32),
                        pltpu.VMEM((2, ML_HEADS, ML_DK, 2 * LANE), F32),
                        pltpu.VMEM((nch, ML_HEADS, ML_DK, CHUNK), BF16)],
        compiler_params=_cparams(("parallel",)),
        name="mlstm",
    )(oa, oa, oa, of, of, grow, bcol, brow, norm_g.reshape(1, ML_DV))


def _gla_tables():
    L = CHUNK
    G = np.zeros((2, 7 * L, L), np.float32)
    lvl = np.full((2, L, L), 7, np.int32)
    for d in range(2):
        for p in range(L):
            if d == 0:
                G[d, p, :p + 1] = 1
            else:
                G[d, p, p:] = 1
        for li, s in enumerate((32, 16, 8, 4, 2, 1)):
            for p in range(L):
                base = (p // (2 * s)) * 2 * s
                row = (li + 1) * L + p
                if d == 0:
                    mid = base + s
                    if p >= mid:
                        G[d, row, mid + 1:p + 1] = 1
                    else:
                        G[d, row, p + 1:mid + 1] = 1
                else:
                    mid = base + s - 1
                    if p <= mid:
                        G[d, row, p:mid] = 1
                    else:
                        G[d, row, mid:p] = 1
            blk = np.arange(L) // (2 * s)
            upper = (np.arange(L) % (2 * s)) >= s
            same = blk[:, None] == blk[None, :]
            if d == 0:
                sel = same & upper[:, None] & ~upper[None, :]
            else:
                sel = same & ~upper[:, None] & upper[None, :]
            lvl[d][sel] = li
        lvl[d][np.arange(L), np.arange(L)] = 6
    return G, lvl


GL_PAIRS = GL_HEADS * GL_DK // LANE


def _gla_kernel(q_ref, k_ref, v_ref, g_ref, a_ref, wa_ref, ba_ref, gm_ref, lvl_ref, ng_ref, out_ref,
                acc_ref, la_ref, st_ref, vt_ref, *, nct, nch):
    L = CHUNK
    _transpose_chunks(v_ref, vt_ref, nch)
    for d in range(2):
        pre = jnp.dot(a_ref[...], wa_ref[d], precision=HIGHEST, preferred_element_type=F32) + ba_ref[d]
        la_ref[d] = _logsig(pre) * (1.0 / GL_TAU)
    acc_ref[...] = jnp.zeros(acc_ref.shape, F32)
    st_ref[...] = jnp.zeros(st_ref.shape, F32)
    first = lax.broadcasted_iota(jnp.int32, (L, LANE), 1) < GL_DK
    first2 = lax.broadcasted_iota(jnp.int32, (GL_DV, LANE), 1) < GL_DK

    def split(x):
        return jnp.concatenate([jnp.where(first, x, 0.0), jnp.where(first, 0.0, x)], axis=0).astype(BF16)

    def step(t, carry):
        chunks = _chunk_ids(t, nct, nch)
        rows = [pl.ds(pl.multiple_of(c * L, L), L) for c in chunks]
        chains = [(d, p) for d in range(2) for p in range(GL_PAIRS)]
        xs = []
        for d in range(2):
            lac = la_ref[d, rows[d], :]
            hi = lac.astype(BF16)
            lo = (lac - hi.astype(F32)).astype(BF16)
            xs.append(_dot(gm_ref[d], hi) + _dot(gm_ref[d], lo))
        q, k, cs, tot, st, inter, upd = {}, {}, {}, {}, {}, {}, {}
        for d, p in chains:
            q[d, p] = q_ref[rows[d], _head_cols(p)].astype(F32) * (GL_DK ** -0.5)
            k[d, p] = k_ref[rows[d], _head_cols(p)].astype(F32)
            cs[d, p] = xs[d][0:L, _head_cols(p)]
            tot[d, p] = cs[d, p][L - 1:L] if d == 0 else cs[d, p][0:1]
            st[d, p] = st_ref[d, p]
        for d, p in chains:
            inter[d, p] = _dot_nt(split(q[d, p] * jnp.exp(cs[d, p])), st[d, p].astype(BF16))
            ke = (k[d, p] * jnp.exp(tot[d, p] - cs[d, p])).astype(BF16)
            u = [_dot(vt_ref[chunks[d], 2 * p + hh], ke) for hh in range(2)]
            st_ref[d, p] = st[d, p] * jnp.exp(tot[d, p]) + jnp.where(first2, u[0], u[1])
        amat = {}
        for d, p in chains:
            amat[d, p] = jnp.where(lvl_ref[d] == 6, _dot_nt(split(q[d, p]), k[d, p].astype(BF16)), 0.0)
        for li in range(6):
            for d, p in chains:
                e = jnp.exp(xs[d][(li + 1) * L:(li + 2) * L, _head_cols(p)])
                lev = _dot_nt(split(q[d, p] * e), (k[d, p] * e).astype(BF16))
                amat[d, p] = jnp.where(lvl_ref[d] == li, lev, amat[d, p])
        for d, p in chains:
            a = amat[d, p].astype(BF16)
            for hh in range(2):
                cols = _head_cols(2 * p + hh)
                acc_ref[rows[d], cols] += (_dot(a[hh * L:(hh + 1) * L], v_ref[rows[d], cols])
                                           + inter[d, p][hh * L:(hh + 1) * L])
        return carry

    lax.fori_loop(0, nch, step, 0, unroll=4)
    for h in range(GL_HEADS):
        cols = _head_cols(h)
        g = g_ref[:, cols]
        out_ref[:, cols] = (_rms(acc_ref[:, cols], ng_ref[...]) * (g * jax.nn.sigmoid(g))).astype(BF16)


def _gla(oa, of, w_alpha, b_alpha, norm_g, nct, nch):
    b, s, _ = oa.shape
    gmat, lvl = _gla_tables()
    lvl = np.concatenate([lvl, lvl], axis=1)
    qk_w = GL_HEADS * GL_DK
    wa = jnp.zeros((2, LANE, qk_w), F32)
    for d in range(2):
        wa = wa.at[d, d * GL_RANK:(d + 1) * GL_RANK, :].set(w_alpha[d])
    ba = b_alpha.reshape(2, 1, qk_w)
    full = lambda nd: (lambda bi: (0,) * nd)
    return pl.pallas_call(
        functools.partial(_gla_kernel, nct=nct, nch=nch),
        out_shape=jax.ShapeDtypeStruct((b, s, HEAD_W), BF16),
        grid=(b,),
        in_specs=[pl.BlockSpec((None, s, qk_w), lambda bi: (bi, 0, A_GQ * LANE // qk_w)),
                  pl.BlockSpec((None, s, qk_w), lambda bi: (bi, 0, A_GK * LANE // qk_w)),
                  pl.BlockSpec((None, s, HEAD_W), lambda bi: (bi, 0, A_GV // 4)),
                  pl.BlockSpec((None, s, HEAD_W), lambda bi: (bi, 0, F_GG // 4)),
                  pl.BlockSpec((None, s, LANE), lambda bi: (bi, 0, F_GA)),
                  pl.BlockSpec((2, LANE, qk_w), full(3)),
                  pl.BlockSpec((2, 1, qk_w), full(3)),
                  pl.BlockSpec((2, 7 * CHUNK, CHUNK), full(3)),
                  pl.BlockSpec((2, 2 * CHUNK, CHUNK), full(3)),
                  pl.BlockSpec((1, GL_DV), full(2))],
        out_specs=pl.BlockSpec((None, s, HEAD_W), lambda bi: (bi, 0, 0)),
        scratch_shapes=[pltpu.VMEM((s, HEAD_W), F32), pltpu.VMEM((2, s, qk_w), F32),
                        pltpu.VMEM((2, GL_PAIRS, GL_DV, LANE), F32),
                        pltpu.VMEM((nch, GL_HEADS, GL_DV, CHUNK), BF16)],
        compiler_params=_cparams(("parallel",)),
        name="gla",
    )(oa, oa, oa, of, of, wa, ba, jnp.asarray(gmat, BF16), jnp.asarray(lvl), norm_g.reshape(1, GL_DV))


def _rope_tables(n):
    t = np.arange(n)
    n_freq = DF_HD // 4
    inv = jnp.asarray(ROPE_THETA, F32) ** (-jnp.arange(n_freq, dtype=F32) / n_freq)
    ang_r = jnp.asarray(t // GRID_W, F32)[:, None] * inv
    ang_c = jnp.asarray(t % GRID_W, F32)[:, None] * inv
    ang = jnp.concatenate([ang_r, ang_r, ang_c, ang_c] * 2, axis=1)
    first = (np.arange(LANE) % 32) < 16
    cos, sin = jnp.cos(ang), jnp.sin(ang)
    return cos, jnp.where(first, -sin, 0.0), jnp.where(first, 0.0, sin)


DF_ROW_GROUPS = 4


def _rope(x, cos, sa, sb):
    return x * cos + pltpu.roll(x, LANE - 16, 1) * sa + pltpu.roll(x, 16, 1) * sb


def _diff_kernel(q_ref, k_ref, v_ref, cq_ref, saq_ref, sbq_ref, ck_ref, sak_ref, sbk_ref, lam_ref, ng_ref,
                 out_ref, kr_ref, *, nct, n_ctx, lam_init):
    s = k_ref.shape[0]
    tq = q_ref.shape[0]
    i = pl.program_id(2)

    @pl.when(i == 0)
    def _():
        kr_ref[0:n_ctx, :] = k_ref[0:n_ctx, :].astype(BF16)
        kr_ref[n_ctx:s, :] = _rope(k_ref[n_ctx:s, :], ck_ref[...], sak_ref[...], sbk_ref[...]).astype(BF16)

    lp = lam_ref[...]
    lam = (jnp.exp(jnp.sum(lp[0:1] * lp[1:2], axis=1, keepdims=True))
           - jnp.exp(jnp.sum(lp[2:3] * lp[3:4], axis=1, keepdims=True)) + lam_init)
    lane = lax.broadcasted_iota(jnp.int32, q_ref.shape, 1)

    def attend(qb, nk):
        qb = qb * (DF_HD ** -0.5)
        qs = jnp.concatenate([jnp.where(lane < DF_HD, qb, 0.0), jnp.where(lane >= DF_HD, qb, 0.0)],
                             axis=0).astype(BF16)
        rg = 2 * tq // DF_ROW_GROUPS
        scs = [_dot_nt(qs[g * rg:(g + 1) * rg], kr_ref[0:nk, :]) for g in range(DF_ROW_GROUPS)]
        ovs = []
        for sc in scs:
            p = jnp.exp(sc - jnp.max(sc, axis=1, keepdims=True))
            rl = 1.0 / jnp.sum(p, axis=1, keepdims=True)
            ovs.append((p.astype(BF16), rl))
        ov = jnp.concatenate([_dot(p, v_ref[0:nk, :]) * rl for p, rl in ovs], axis=0)
        o = ov[0:tq] - lam * ov[tq:2 * tq]
        out_ref[...] = (_rms(o, ng_ref[...]) * (1.0 - lam_init)).astype(BF16)

    @pl.when(i < nct)
    def _():
        attend(q_ref[...], n_ctx)

    @pl.when(i >= nct)
    def _():
        attend(_rope(q_ref[...], cq_ref[...], saq_ref[...], sbq_ref[...]), s)


def _diff_attn(oa, of, rope, df_lambda, norm_g, tq, nct, n_ctx, lam_init):
    b, s, _ = oa.shape
    n = s - n_ctx
    cos, sa, sb = rope
    lam_p = jnp.zeros((4, LANE), F32).at[:, :DF_HD].set(df_lambda)
    qblk = lambda bi, h, i: (jnp.maximum(i - nct, 0), 0)
    full = lambda bi, h, i: (0, 0)
    return pl.pallas_call(
        functools.partial(_diff_kernel, nct=nct, n_ctx=n_ctx, lam_init=lam_init),
        out_shape=jax.ShapeDtypeStruct((b, s, DF_HEADS * DF_DV), BF16),
        grid=(b, DF_HEADS, s // tq),
        in_specs=[pl.BlockSpec((None, tq, LANE), lambda bi, h, i: (bi, i, F_DQ + h)),
                  pl.BlockSpec((None, s, LANE), lambda bi, h, i: (bi, 0, F_DK + h)),
                  pl.BlockSpec((None, s, LANE), lambda bi, h, i: (bi, 0, A_DV + h)),
                  pl.BlockSpec((tq, LANE), qblk), pl.BlockSpec((tq, LANE), qblk), pl.BlockSpec((tq, LANE), qblk),
                  pl.BlockSpec((n, LANE), full), pl.BlockSpec((n, LANE), full), pl.BlockSpec((n, LANE), full),
                  pl.BlockSpec((4, LANE), full), pl.BlockSpec((1, DF_DV), full)],
        out_specs=pl.BlockSpec((None, tq, LANE), lambda bi, h, i: (bi, i, h)),
        scratch_shapes=[pltpu.VMEM((s, LANE), BF16)],
        compiler_params=_cparams(("parallel", "parallel", "arbitrary")),
        name="diff_attn",
    )(of, of, oa, cos, sa, sb, cos, sa, sb, lam_p, norm_g.reshape(1, DF_DV))


def _merge_kernel(ml_ref, df_ref, gl_ref, g0_ref, g1_ref, g2_ref, wb_ref, wo_ref, x_ref, mod_ref, nf_ref,
                  wr_ref, br_ref, tri_ref, xo_ref, h2_ref, rid_ref, rw_ref, cnt_ref):
    y = (jax.nn.sigmoid(g0_ref[...]) * _dot(ml_ref[...], wb_ref[0])
         + jax.nn.sigmoid(g1_ref[...]) * _dot(df_ref[...], wb_ref[1])
         + jax.nn.sigmoid(g2_ref[...]) * _dot(gl_ref[...], wb_ref[2]))
    xn = x_ref[...] + mod_ref[2:3, :] * _dot(y.astype(BF16), wo_ref[...])
    xo_ref[...] = xn
    h2 = _rms(xn, nf_ref[...]) * (1.0 + mod_ref[4:5, :]) + mod_ref[3:4, :]
    h2_hi = h2.astype(BF16)
    h2_ref[...] = h2_hi
    h2_lo = (h2 - h2_hi.astype(F32)).astype(BF16)
    logits = (_dot(h2_hi, wr_ref[0]) + (_dot(h2_lo, wr_ref[0]) + _dot(h2_hi, wr_ref[1]))) + br_ref[...]
    lane = lax.broadcasted_iota(jnp.int32, logits.shape, 1)
    lane_f = lane.astype(F32)
    neg = -jnp.inf

    def first_max(vals):
        mx = jnp.max(vals, axis=1, keepdims=True)
        return mx, jnp.min(jnp.where(vals == mx, lane_f, float(LANE)), axis=1, keepdims=True)

    is_grp = lane < N_GROUPS
    gmax, gidx = first_max(jnp.where(is_grp, logits, neg))
    pg_top = 1.0 / jnp.sum(jnp.where(is_grp, jnp.exp(logits - gmax), 0.0), axis=1, keepdims=True)
    lo = N_GROUPS + gidx * EXPERTS_PER_GROUP
    in_grp = (lane_f >= lo) & (lane_f < lo + EXPERTS_PER_GROUP)
    le = jnp.where(in_grp, logits, neg)
    m1, e1 = first_max(le)
    m2, e2 = first_max(jnp.where(lane_f == e1, neg, le))
    r = jnp.exp(m2 - m1)
    w1 = pg_top / (1.0 + r)
    w2 = pg_top * r / (1.0 + r)

    @pl.when((pl.program_id(0) == 0) & (pl.program_id(1) == 0))
    def _():
        cnt_ref[...] = jnp.zeros(cnt_ref.shape, F32)

    hot1 = jnp.where(lane_f == e1, 1.0, 0.0)
    hot2 = jnp.where(lane_f == e2, 1.0, 0.0)
    tot1 = jnp.sum(hot1, axis=0, keepdims=True)
    tot2 = jnp.sum(hot2, axis=0, keepdims=True)
    cnt = cnt_ref[...]
    before1 = cnt + _dot(tri_ref[...], hot1.astype(BF16))
    before2 = cnt + tot1 + _dot(tri_ref[...], hot2.astype(BF16))
    rank1 = jnp.sum(hot1 * before1, axis=1, keepdims=True)
    rank2 = jnp.sum(hot2 * before2, axis=1, keepdims=True)
    cnt_ref[...] = cnt + tot1 + tot2
    rid = jnp.where(lane == 0, e1 - N_GROUPS, jnp.where(lane == 1, e2 - N_GROUPS,
                    jnp.where(lane == 2, rank1, jnp.where(lane == 3, rank2, 0.0))))
    rid_ref[...] = rid.astype(jnp.int32)
    rw_ref[...] = jnp.where(lane == 0, w1, jnp.where(lane == 1, w2, 0.0))


def _merge(ml, df, gl, of, wb, wo, xs, modtab, nf, wr, br, tm, nct, first_block):
    b, s, d = xs.shape
    nb = s // tm - first_block
    so = nb * tm
    kind = lambda i: jnp.where(i + first_block >= nct, 1, 0)
    row = lambda bi, i: (bi, i + first_block, 0)
    gate = lambda br_: (lambda bi, i: (bi, i + first_block, F_GT // (d // LANE) + br_))
    outrow = lambda bi, i: (bi, i, 0)
    tri = jnp.asarray(np.tril(np.ones((tm, tm), np.float32), -1), BF16)
    return pl.pallas_call(
        _merge_kernel,
        out_shape=(jax.ShapeDtypeStruct((b, so, d), F32), jax.ShapeDtypeStruct((b, so, d), BF16),
                   jax.ShapeDtypeStruct((b, so, LANE), jnp.int32), jax.ShapeDtypeStruct((b, so, LANE), F32),
                   jax.ShapeDtypeStruct((1, LANE), F32)),
        grid=(b, nb),
        in_specs=[pl.BlockSpec((None, tm, BRANCH_W), row), pl.BlockSpec((None, tm, BRANCH_W), row),
                  pl.BlockSpec((None, tm, BRANCH_W), row),
                  pl.BlockSpec((None, tm, d), gate(0)), pl.BlockSpec((None, tm, d), gate(1)),
                  pl.BlockSpec((None, tm, d), gate(2)),
                  _const_spec((N_BRANCH, BRANCH_W, d)), _const_spec((d, d)),
                  pl.BlockSpec((None, tm, d), row),
                  pl.BlockSpec((None, None, SUBLANE, d), lambda bi, i: (bi, kind(i), 0, 0)),
                  _const_spec((1, d)), _const_spec((2, d, LANE)), _const_spec((1, LANE)), _const_spec((tm, tm))],
        out_specs=(pl.BlockSpec((None, tm, d), outrow), pl.BlockSpec((None, tm, d), outrow),
                   pl.BlockSpec((None, tm, LANE), outrow), pl.BlockSpec((None, tm, LANE), outrow),
                   pl.BlockSpec((1, LANE), lambda bi, i: (0, 0))),
        compiler_params=_cparams(("arbitrary", "arbitrary")),
        name="merge_route",
    )(ml, df, gl, of, of, of, wb, wo, xs, modtab, nf.reshape(1, d), wr, br, tri)


def _gmm_kernel(te_ref, tv_ref, x_ref, wg_ref, wu_ref, wd_ref, y_ref, wgb_ref, wub_ref, wdb_ref):
    i = pl.program_id(0)

    @pl.when((i == 0) | (te_ref[i] != te_ref[jnp.maximum(i - 1, 0)]))
    def _():
        wgb_ref[...] = wg_ref[...].astype(BF16)
        wub_ref[...] = wu_ref[...].astype(BF16)
        wdb_ref[...] = wd_ref[...].astype(BF16)

    @pl.when(tv_ref[i] > 0)
    def _():
        x = x_ref[...]
        a = _dot(x, wgb_ref[...])
        hid = (a * jax.nn.sigmoid(a)) * _dot(x, wub_ref[...])
        y_ref[...] = _dot(hid.astype(BF16), wdb_ref[...]).astype(y_ref.dtype)

    @pl.when(tv_ref[i] == 0)
    def _():
        y_ref[...] = jnp.zeros(y_ref.shape, y_ref.dtype)


def _gmm(tile_expert, tile_valid, xs, wg, wu, wd, li):
    npad, d = xs.shape
    tm = MOE_TILE
    de = wg.shape[3]
    return pl.pallas_call(
        _gmm_kernel,
        out_shape=jax.ShapeDtypeStruct((npad, d), BF16),
        grid_spec=pltpu.PrefetchScalarGridSpec(
            num_scalar_prefetch=2,
            grid=(npad // tm,),
            in_specs=[pl.BlockSpec((tm, d), lambda i, te, tv: (i, 0)),
                      pl.BlockSpec((None, None, d, de), lambda i, te, tv: (li, te[i], 0, 0)),
                      pl.BlockSpec((None, None, d, de), lambda i, te, tv: (li, te[i], 0, 0)),
                      pl.BlockSpec((None, None, de, d), lambda i, te, tv: (li, te[i], 0, 0))],
            out_specs=pl.BlockSpec((tm, d), lambda i, te, tv: (i, 0)),
            scratch_shapes=[pltpu.VMEM((d, de), BF16), pltpu.VMEM((d, de), BF16), pltpu.VMEM((de, d), BF16)]),
        compiler_params=_cparams(("arbitrary",)),
        name="moe_gmm",
    )(tile_expert, tile_valid, xs, wg, wu, wd)


def _route_plan(rid, cnt):
    t = rid.shape[0]
    tm = MOE_TILE
    n_tiles = (TOP_K * t + tm - 1) // tm + N_EXPERTS
    counts = cnt[0, N_GROUPS:N_GROUPS + N_EXPERTS].astype(jnp.int32)
    tiles_e = (counts + tm - 1) // tm
    tile_end = jnp.cumsum(tiles_e)
    pad_start = (tile_end - tiles_e) * tm
    eid, rank = rid[:, 0:TOP_K], rid[:, TOP_K:2 * TOP_K]
    pos = rank + jnp.sum(jnp.where(eid[:, :, None] == jnp.arange(N_EXPERTS), pad_start, 0), axis=-1)
    tile = jnp.arange(n_tiles, dtype=jnp.int32)
    tile_expert = jnp.minimum(jnp.sum(tile[:, None] >= tile_end[None, :], axis=1), N_EXPERTS - 1).astype(jnp.int32)
    tile_valid = (tile < tile_end[-1]).astype(jnp.int32)
    return tile_expert, tile_valid, pos.astype(jnp.int32), n_tiles * tm


def _combine_kernel(x_ref, y0_ref, y1_ref, rw_ref, mod_ref, fg_ref, o_ref, *, final):
    rw = rw_ref[...]
    y = rw[:, 0:1] * y0_ref[...].astype(F32) + rw[:, 1:2] * y1_ref[...].astype(F32)
    xn = x_ref[...] + mod_ref[5:6, :] * y
    o_ref[...] = _rms(xn, fg_ref[...]) if final else xn


def _combine(xs, yy, rw, modtab, fg, tm, nct, first_block, final):
    b, s, d = xs.shape
    kind = lambda i: jnp.where(i + first_block >= nct, 1, 0)
    row = lambda bi, i: (bi, i, 0)
    slot = lambda k: (lambda bi, i: (k, bi, i, 0))
    return pl.pallas_call(
        functools.partial(_combine_kernel, final=final),
        out_shape=jax.ShapeDtypeStruct((b, s, d), F32),
        grid=(b, s // tm),
        in_specs=[pl.BlockSpec((None, tm, d), row), pl.BlockSpec((None, None, tm, d), slot(0)),
                  pl.BlockSpec((None, None, tm, d), slot(1)), pl.BlockSpec((None, tm, LANE), row),
                  pl.BlockSpec((None, None, SUBLANE, d), lambda bi, i: (bi, kind(i), 0, 0)),
                  _const_spec((1, d))],
        out_specs=pl.BlockSpec((None, tm, d), row),
        compiler_params=_cparams(("parallel", "parallel")),
        name="moe_combine",
    )(xs, yy, yy, rw, modtab, fg.reshape(1, d))


def _pack_w_in(w):
    d = w.shape[0]
    mq, mk, mv, mo, mg, dq, dk, dv, gq, gk, gv, gg, ga, gt = jnp.split(
        w, [int(i) for i in np.cumsum(IN_SIZES)[:-1]], axis=1)

    def pad_cols(t):
        return jnp.pad(t, ((0, 0), (0, LANE - t.shape[1])))

    wa = jnp.concatenate([mq, mk, mv, gq, gk, gv, dv], axis=1).astype(BF16)
    wf = jnp.concatenate([dq, dk, mo, gg, gt, pad_cols(mg), pad_cols(ga)], axis=1).astype(BF16)
    return wa, wf


def _dispatch_rows(h2, pos, npad):
    t = h2.shape[0]
    tok = jnp.arange(t, dtype=jnp.int32)
    src = jnp.zeros((npad,), jnp.int32).at[pos[:, 0]].set(tok).at[pos[:, 1]].set(tok)
    return jnp.take(h2, src, axis=0, mode="clip")


def kernel(x, c, ctx, c_ctx, w_mod, b_mod, norm_mix_g, norm_ffn_g, w_in, ml_gate_b, ml_norm_g, df_lambda,
           df_norm_g, gl_w_alpha, gl_b_alpha, gl_norm_g, w_branch, w_out, router_group_w, router_group_b,
           router_expert_w, router_expert_b, moe_w_gate, moe_w_up, moe_w_down, final_norm_g):
    b, n, d = x.shape
    n_ctx = ctx.shape[1]
    s = n_ctx + n
    depth = w_mod.shape[0]
    tm = 256 if n_ctx % 256 == 0 else 128
    assert n_ctx % tm == 0 and n % tm == 0 and n % GRID_W == 0 and n_ctx % CHUNK == 0 and d == D_MODEL
    nct = n_ctx // tm
    nch, nch_ctx = s // CHUNK, n_ctx // CHUNK
    rope = _rope_tables(n)
    xs = jnp.concatenate([ctx, x], axis=1)
    mod_rows = -(-(b + 1) // SUBLANE) * SUBLANE
    cc = jnp.zeros((mod_rows, d), F32).at[:b].set(c).at[b].set(c_ctx)
    for li in range(depth):
        last = li == depth - 1
        lam_init = 0.8 - 0.6 * math.exp(-0.3 * li)
        mod = _modulation(cc, w_mod, b_mod[li], li).reshape(mod_rows, 6, d)
        mod = jnp.pad(mod, ((0, 0), (0, SUBLANE - 6), (0, 0)))
        modtab = jnp.stack([jnp.broadcast_to(mod[b], (b, SUBLANE, d)), mod[:b]], axis=1)
        wa, wf = _pack_w_in(w_in[li])
        oa, of = _inproj(xs, modtab, norm_mix_g[li], wa, wf, tm, nct)
        grow = of[:, :, F_MG * LANE:F_MG * LANE + 4 * ML_HEADS]
        grow = grow.reshape(b, nch, CHUNK, 4 * ML_HEADS).transpose(0, 1, 3, 2)
        ml = _mlstm(oa, of, grow, ml_gate_b[li], ml_norm_g[li], nch_ctx, nch)
        df = _diff_attn(oa, of, rope, df_lambda[li], df_norm_g[li], tm, nct, n_ctx, lam_init)
        gl = _gla(oa, of, gl_w_alpha[li], gl_b_alpha[li], gl_norm_g[li], nch_ctx, nch)
        wr = jnp.zeros((d, LANE), F32).at[:, :N_GROUPS].set(router_group_w[li])
        wr = wr.at[:, N_GROUPS:N_GROUPS + N_EXPERTS].set(router_expert_w[li])
        wr_hi = wr.astype(BF16)
        wr = jnp.stack([wr_hi, (wr - wr_hi.astype(F32)).astype(BF16)])
        br = jnp.zeros((1, LANE), F32).at[0, :N_GROUPS].set(router_group_b[li])
        br = br.at[0, N_GROUPS:N_GROUPS + N_EXPERTS].set(router_expert_b[li])
        first_block = nct if last else 0
        xn, h2, rid, rw, cnt = _merge(ml, df, gl, of, w_branch[li].astype(BF16), w_out[li].astype(BF16), xs,
                                      modtab, norm_ffn_g[li], wr, br, tm, nct, first_block)
        so = xn.shape[1]
        t = b * so
        te, tv, pos, npad = _route_plan(rid.reshape(t, LANE), cnt)
        xsorted = _dispatch_rows(h2.reshape(t, d), pos, npad)
        ys = _gmm(te, tv, xsorted, moe_w_gate, moe_w_up, moe_w_down, li)
        yy = jnp.take(ys, pos.T.reshape(-1), axis=0, mode="clip").reshape(TOP_K, b, so, d)
        xs = _combine(xn, yy, rw, modtab, final_norm_g, tm, nct, first_block, last)
    return xs
```

```python
import functools
import math

import numpy as np
import jax
import jax.numpy as jnp
from jax import lax
from jax.experimental import pallas as pl
from jax.experimental.pallas import tpu as pltpu

F32 = jnp.float32
BF16 = jnp.bfloat16
HIGHEST = lax.Precision.HIGHEST

D_MODEL = 1024
GRID_W = 64
NORM_EPS = 1e-6
CHUNK = 64
ROPE_THETA = 10000.0
ML_HEADS, ML_DK, ML_DV = 4, 128, 128
DF_HEADS, DF_HD, DF_DV = 4, 64, 128
GL_HEADS, GL_DK, GL_DV, GL_RANK, GL_TAU = 4, 64, 128, 16, 16.0
N_BRANCH, BRANCH_W = 3, 512
N_GROUPS, EXPERTS_PER_GROUP, TOP_K, D_EXPERT = 4, 8, 2, 512
N_EXPERTS = N_GROUPS * EXPERTS_PER_GROUP
IN_SIZES = (
    ML_HEADS * ML_DK, ML_HEADS * ML_DK, ML_HEADS * ML_DV, ML_HEADS * ML_DV, 4 * ML_HEADS,
    DF_HEADS * 2 * DF_HD, DF_HEADS * 2 * DF_HD, DF_HEADS * DF_DV,
    GL_HEADS * GL_DK, GL_HEADS * GL_DK, GL_HEADS * GL_DV, GL_HEADS * GL_DV, 2 * GL_RANK,
    N_BRANCH * D_MODEL,
)

LANE = 128
SUBLANE = 8
VMEM_LIMIT = 56 * 1024 * 1024
MOE_TILE = 512

A_MQ, A_MK, A_MV, A_GQ, A_GK, A_GV, A_DV = 0, 4, 8, 12, 14, 16, 20
A_WIDTH = 24 * LANE
F_DQ, F_DK, F_MO, F_GG, F_GT, F_MG, F_GA = 0, 4, 8, 12, 16, 40, 41
F_WIDTH = 42 * LANE
HEAD_W = 4 * LANE


def _cparams(sem):
    return pltpu.CompilerParams(dimension_semantics=sem, vmem_limit_bytes=VMEM_LIMIT)


def _const_spec(shape):
    nd = len(shape)
    return pl.BlockSpec(shape, lambda *_: (0,) * nd, pipeline_mode=pl.Buffered(1))


def _logsig(x):
    return jnp.minimum(x, 0.0) - jnp.log1p(jnp.exp(-jnp.abs(x)))


def _dot(a, b):
    return jnp.dot(a, b, preferred_element_type=F32)


def _dot_nt(a, b):
    return lax.dot_general(a, b, (((1,), (1,)), ((), ())), preferred_element_type=F32)


def _dot_tn(a, b):
    return lax.dot_general(a, b, (((0,), (0,)), ((), ())), preferred_element_type=F32)


def _rms(x, g):
    return x * lax.rsqrt(jnp.mean(x * x, axis=-1, keepdims=True) + NORM_EPS) * g


def _mod_kernel(c_ref, w_ref, b_ref, o_ref):
    c = c_ref[...]
    s = c * jax.nn.sigmoid(c)
    o_ref[...] = jnp.dot(s, w_ref[...], precision=HIGHEST, preferred_element_type=F32) + b_ref[...]


def _modulation(cc, w_mod, b_mod, li):
    rows, d = cc.shape
    n = w_mod.shape[2]
    tn = 512
    return pl.pallas_call(
        _mod_kernel,
        out_shape=jax.ShapeDtypeStruct((rows, n), F32),
        grid=(n // tn,),
        in_specs=[pl.BlockSpec((rows, d), lambda j: (0, 0)),
                  pl.BlockSpec((None, d, tn), lambda j: (li, 0, j)),
                  pl.BlockSpec((1, tn), lambda j: (0, j))],
        out_specs=pl.BlockSpec((rows, tn), lambda j: (0, j)),
        compiler_params=_cparams(("parallel",)),
        name="adaln_mod",
    )(cc, w_mod, b_mod.reshape(1, n))


def _inproj_kernel(x_ref, mod_ref, g_ref, wa_ref, wf_ref, oa_ref, of_ref):
    y = _rms(x_ref[...], g_ref[...])
    h = (y * (1.0 + mod_ref[1:2, :]) + mod_ref[0:1, :]).astype(BF16)
    oa_ref[...] = _dot(h, wa_ref[...]).astype(BF16)
    of_ref[...] = _dot(h, wf_ref[...])


def _inproj(xs, modtab, g, wa, wf, tm, nct):
    b, s, d = xs.shape
    kind = lambda i: jnp.where(i >= nct, 1, 0)
    return pl.pallas_call(
        _inproj_kernel,
        out_shape=(jax.ShapeDtypeStruct((b, s, A_WIDTH), BF16), jax.ShapeDtypeStruct((b, s, F_WIDTH), F32)),
        grid=(b, s // tm),
        in_specs=[pl.BlockSpec((None, tm, d), lambda bi, i: (bi, i, 0)),
                  pl.BlockSpec((None, None, SUBLANE, d), lambda bi, i: (bi, kind(i), 0, 0)),
                  _const_spec((1, d)), _const_spec((d, A_WIDTH)), _const_spec((d, F_WIDTH))],
        out_specs=(pl.BlockSpec((None, tm, A_WIDTH), lambda bi, i: (bi, i, 0)),
                   pl.BlockSpec((None, tm, F_WIDTH), lambda bi, i: (bi, i, 0))),
        compiler_params=_cparams(("parallel", "parallel")),
        name="in_proj",
    )(xs, modtab, g.reshape(1, d), wa, wf)


def _chunk_ids(t, nct, nch):
    return t, jnp.where(t < nct, nct - 1 - t, nch - 1 - (t - nct))


def _head_cols(h):
    return slice(h * LANE, (h + 1) * LANE)


def _transpose_chunks(src_ref, dst_ref, nch):
    eye = (lax.broadcasted_iota(jnp.int32, (LANE, LANE), 0)
           == lax.broadcasted_iota(jnp.int32, (LANE, LANE), 1)).astype(BF16)

    def body(c, carry):
        blk = src_ref[pl.ds(pl.multiple_of(c * CHUNK, CHUNK), CHUNK), :]
        for h in range(HEAD_W // LANE):
            dst_ref[c, h] = _dot_nt(eye, blk[:, _head_cols(h)]).astype(BF16)
        return carry

    lax.fori_loop(0, nch, body, 0, unroll=2)


def _mlstm_kernel(q_ref, k_ref, v_ref, o_ref, gcol_ref, grow_ref, bcol_ref, brow_ref, ng_ref, out_ref,
                  hf_ref, hb_ref, ct_ref, kt_ref, *, nct, nch):
    L = CHUNK
    H = ML_HEADS
    _transpose_chunks(k_ref, kt_ref, nch)
    jj = lax.broadcasted_iota(jnp.int32, (L, L), 0)
    ii = lax.broadcasted_iota(jnp.int32, (L, L), 1)
    lane = lax.broadcasted_iota(jnp.int32, (L, LANE), 1)
    ones_col = jnp.where(lane == 0, 1.0, 0.0).astype(BF16)
    scale = ML_DK ** -0.5
    vis = ((ii <= jj), (ii >= jj))
    cum = tuple(jnp.where(m_, 1.0, 0.0).astype(BF16) for m_ in vis)
    cum_t = tuple(jnp.where(m_, 1.0, 0.0).astype(BF16) for m_ in ((jj <= ii), (jj >= ii)))
    ct_ref[...] = jnp.zeros(ct_ref.shape, F32)

    def split2(x):
        hi = x.astype(BF16)
        return hi, (x - hi.astype(F32)).astype(BF16)

    def step(t, carry):
        chunks = _chunk_ids(t, nct, nch)
        rows = [pl.ds(pl.multiple_of(c * L, L), L) for c in chunks]
        chains = [(d, h) for d in range(2) for h in range(H)]
        q, kt, vext, ct, sq, qc = {}, {}, {}, {}, {}, {}
        for d, h in chains:
            q[d, h] = q_ref[rows[d], _head_cols(h)]
            kt[d, h] = kt_ref[chunks[d], h]
            vext[d, h] = jnp.concatenate([v_ref[rows[d], _head_cols(h)], ones_col], axis=1)
            ct[d, h] = ct_ref[d, h]
            sq[d, h] = _dot(q[d, h], kt[d, h])
            qc[d, h] = _dot(q[d, h], ct[d, h].astype(BF16))
        gr, bc_all, br_all = [], [], []
        for d in range(2):
            gc = gcol_ref[rows[d], :] + bcol_ref[...]
            gr.append(grow_ref[chunks[d]] + brow_ref[...])
            chi, clo = split2(_logsig(gc))
            rhi, rlo = split2(_logsig(gr[d]))
            bc_all.append(_dot(cum[d], chi) + _dot(cum[d], clo))
            br_all.append(_dot(rhi, cum_t[d]) + _dot(rlo, cum_t[d]))
        w, w_inter, einv, decay, kw, new = {}, {}, {}, {}, {}, []
        for d, h in chains:
            m = carry[d * H + h]
            gi = 2 * d * H + h
            b_row = br_all[d][gi + H:gi + H + 1, :]
            b_col = bc_all[d][:, gi + H:gi + H + 1]
            b_last = b_row[:, L - 1:L] if d == 0 else b_row[:, 0:1]
            u_row = gr[d][gi:gi + 1, :] - b_row
            g_col = jnp.maximum(jnp.max(jnp.where(vis[d], u_row, -jnp.inf), axis=1, keepdims=True), m)
            w[d, h] = jnp.exp(jnp.where(vis[d], u_row - g_col, -jnp.inf)) * scale
            w_inter[d, h] = jnp.exp(m - g_col)
            einv[d, h] = jnp.exp(-(b_col + g_col))
            m_new = b_last + jnp.maximum(m, jnp.max(u_row, axis=1, keepdims=True))
            wk_row = jnp.exp(b_last + u_row - m_new) * scale
            decay[d, h] = jnp.exp(b_last + m - m_new)
            kw[d, h] = (kt[d, h].astype(F32) * wk_row).astype(BF16)
            new.append(m_new)
        for d, h in chains:
            ct_ref[d, h] = decay[d, h] * ct[d, h] + _dot(kw[d, h], vext[d, h])
        sv = {}
        for d, h in chains:
            sv[d, h] = _dot((sq[d, h] * w[d, h]).astype(BF16), vext[d, h])
        for d, h in chains:
            num = sv[d, h][:, :ML_DV] + w_inter[d, h] * qc[d, h][:, :ML_DV]
            den = sv[d, h][:, ML_DV:ML_DV + 1] + w_inter[d, h] * qc[d, h][:, ML_DV:ML_DV + 1]
            hh = num / jnp.maximum(jnp.abs(den), einv[d, h])
            if d == 0:
                hf_ref[rows[d], _head_cols(h)] = hh
            else:
                hb_ref[rows[d], _head_cols(h)] = hh
        return tuple(new)

    zero = jnp.zeros((1, 1), F32)
    lax.fori_loop(0, nch, step, (zero,) * (2 * H), unroll=2)
    for h in range(H):
        y = _rms(hf_ref[:, _head_cols(h)] + hb_ref[:, _head_cols(h)], ng_ref[...])
        out_ref[:, _head_cols(h)] = (y * jax.nn.sigmoid(o_ref[:, _head_cols(h)])).astype(BF16)


def _mlstm(oa, of, grow, gate_b, norm_g, nct, nch):
    b, s, _ = oa.shape
    bcol = jnp.zeros((1, LANE), F32).at[0, :4 * ML_HEADS].set(gate_b)
    brow = gate_b.reshape(4 * ML_HEADS, 1)
    sect = lambda base: (lambda bi: (bi, 0, base // 4))
    return pl.pallas_call(
        functools.partial(_mlstm_kernel, nct=nct, nch=nch),
        out_shape=jax.ShapeDtypeStruct((b, s, HEAD_W), BF16),
        grid=(b,),
        in_specs=[pl.BlockSpec((None, s, HEAD_W), sect(A_MQ)),
                  pl.BlockSpec((None, s, HEAD_W), sect(A_MK)),
                  pl.BlockSpec((None, s, HEAD_W), sect(A_MV)),
                  pl.BlockSpec((None, s, HEAD_W), sect(F_MO)),
                  pl.BlockSpec((None, s, LANE), lambda bi: (bi, 0, F_MG)),
                  pl.BlockSpec((None, nch, 4 * ML_HEADS, CHUNK), lambda bi: (bi, 0, 0, 0)),
                  pl.BlockSpec((1, LANE), lambda bi: (0, 0)),
                  pl.BlockSpec((4 * ML_HEADS, 1), lambda bi: (0, 0)),
                  pl.BlockSpec((1, ML_DV), lambda bi: (0, 0))],
        out_specs=pl.BlockSpec((None, s, HEAD_W), lambda bi: (bi, 0, 0)),
        scratch_shapes=[pltpu.VMEM((s, HEAD_W), F32), pltpu.VMEM((s, HEAD_W), F32),
                        pltpu.VMEM((2, ML_HEADS, ML_DK, 2 * LANE), F32),
                        pltpu.VMEM((nch, ML_HEADS, ML_DK, CHUNK), BF16)],
        compiler_params=_cparams(("parallel",)),
        name="mlstm",
    )(oa, oa, oa, of, of, grow, bcol, brow, norm_g.reshape(1, ML_DV))


def _gla_tables():
    L = CHUNK
    G = np.zeros((2, 7 * L, L), np.float32)
    lvl = np.full((2, L, L), 7, np.int32)
    for d in range(2):
        for p in range(L):
            if d == 0:
                G[d, p, :p + 1] = 1
            else:
                G[d, p, p:] = 1
        for li, s in enumerate((32, 16, 8, 4, 2, 1)):
            for p in range(L):
                base = (p // (2 * s)) * 2 * s
                row = (li + 1) * L + p
                if d == 0:
                    mid = base + s
                    if p >= mid:
                        G[d, row, mid + 1:p + 1] = 1
                    else:
                        G[d, row, p + 1:mid + 1] = 1
                else:
                    mid = base + s - 1
                    if p <= mid:
                        G[d, row, p:mid] = 1
                    else:
                        G[d, row, mid:p] = 1
            blk = np.arange(L) // (2 * s)
            upper = (np.arange(L) % (2 * s)) >= s
            same = blk[:, None] == blk[None, :]
            if d == 0:
                sel = same & upper[:, None] & ~upper[None, :]
            else:
                sel = same & ~upper[:, None] & upper[None, :]
            lvl[d][sel] = li
        lvl[d][np.arange(L), np.arange(L)] = 6
    return G, lvl


GL_PAIRS = GL_HEADS * GL_DK // LANE


def _gla_kernel(q_ref, k_ref, v_ref, g_ref, a_ref, wa_ref, ba_ref, gm_ref, lvl_ref, ng_ref, out_ref,
                acc_ref, la_ref, st_ref, vt_ref, *, nct, nch):
    L = CHUNK
    _transpose_chunks(v_ref, vt_ref, nch)
    for d in range(2):
        pre = jnp.dot(a_ref[...], wa_ref[d], precision=HIGHEST, preferred_element_type=F32) + ba_ref[d]
        la_ref[d] = _logsig(pre) * (1.0 / GL_TAU)
    acc_ref[...] = jnp.zeros(acc_ref.shape, F32)
    st_ref[...] = jnp.zeros(st_ref.shape, F32)
    first = lax.broadcasted_iota(jnp.int32, (L, LANE), 1) < GL_DK
    first2 = lax.broadcasted_iota(jnp.int32, (GL_DV, LANE), 1) < GL_DK

    def split(x):
        return jnp.concatenate([jnp.where(first, x, 0.0), jnp.where(first, 0.0, x)], axis=0).astype(BF16)

    def step(t, carry):
        chunks = _chunk_ids(t, nct, nch)
        rows = [pl.ds(pl.multiple_of(c * L, L), L) for c in chunks]
        chains = [(d, p) for d in range(2) for p in range(GL_PAIRS)]
        xs = []
        for d in range(2):
            lac = la_ref[d, rows[d], :]
            hi = lac.astype(BF16)
            lo = (lac - hi.astype(F32)).astype(BF16)
            xs.append(_dot(gm_ref[d], hi) + _dot(gm_ref[d], lo))
        q, k, cs, tot, st, inter, upd = {}, {}, {}, {}, {}, {}, {}
        for d, p in chains:
            q[d, p] = q_ref[rows[d], _head_cols(p)].astype(F32) * (GL_DK ** -0.5)
            k[d, p] = k_ref[rows[d], _head_cols(p)].astype(F32)
            cs[d, p] = xs[d][0:L, _head_cols(p)]
            tot[d, p] = cs[d, p][L - 1:L] if d == 0 else cs[d, p][0:1]
            st[d, p] = st_ref[d, p]
        for d, p in chains:
            inter[d, p] = _dot_nt(split(q[d, p] * jnp.exp(cs[d, p])), st[d, p].astype(BF16))
            ke = (k[d, p] * jnp.exp(tot[d, p] - cs[d, p])).astype(BF16)
            u = [_dot(vt_ref[chunks[d], 2 * p + hh], ke) for hh in range(2)]
            st_ref[d, p] = st[d, p] * jnp.exp(tot[d, p]) + jnp.where(first2, u[0], u[1])
        amat = {}
        for d, p in chains:
            amat[d, p] = jnp.where(lvl_ref[d] == 6, _dot_nt(split(q[d, p]), k[d, p].astype(BF16)), 0.0)
        for li in range(6):
            for d, p in chains:
                e = jnp.exp(xs[d][(li + 1) * L:(li + 2) * L, _head_cols(p)])
                lev = _dot_nt(split(q[d, p] * e), (k[d, p] * e).astype(BF16))
                amat[d, p] = jnp.where(lvl_ref[d] == li, lev, amat[d, p])
        for d, p in chains:
            a = amat[d, p].astype(BF16)
            for hh in range(2):
                cols = _head_cols(2 * p + hh)
                acc_ref[rows[d], cols] += (_dot(a[hh * L:(hh + 1) * L], v_ref[rows[d], cols])
                                           + inter[d, p][hh * L:(hh + 1) * L])
        return carry

    lax.fori_loop(0, nch, step, 0, unroll=4)
    for h in range(GL_HEADS):
        cols = _head_cols(h)
        g = g_ref[:, cols]
        out_ref[:, cols] = (_rms(acc_ref[:, cols], ng_ref[...]) * (g * jax.nn.sigmoid(g))).astype(BF16)


def _gla(oa, of, w_alpha, b_alpha, norm_g, nct, nch):
    b, s, _ = oa.shape
    gmat, lvl = _gla_tables()
    lvl = np.concatenate([lvl, lvl], axis=1)
    qk_w = GL_HEADS * GL_DK
    wa = jnp.zeros((2, LANE, qk_w), F32)
    for d in range(2):
        wa = wa.at[d, d * GL_RANK:(d + 1) * GL_RANK, :].set(w_alpha[d])
    ba = b_alpha.reshape(2, 1, qk_w)
    full = lambda nd: (lambda bi: (0,) * nd)
    return pl.pallas_call(
        functools.partial(_gla_kernel, nct=nct, nch=nch),
        out_shape=jax.ShapeDtypeStruct((b, s, HEAD_W), BF16),
        grid=(b,),
        in_specs=[pl.BlockSpec((None, s, qk_w), lambda bi: (bi, 0, A_GQ * LANE // qk_w)),
                  pl.BlockSpec((None, s, qk_w), lambda bi: (bi, 0, A_GK * LANE // qk_w)),
                  pl.BlockSpec((None, s, HEAD_W), lambda bi: (bi, 0, A_GV // 4)),
                  pl.BlockSpec((None, s, HEAD_W), lambda bi: (bi, 0, F_GG // 4)),
                  pl.BlockSpec((None, s, LANE), lambda bi: (bi, 0, F_GA)),
                  pl.BlockSpec((2, LANE, qk_w), full(3)),
                  pl.BlockSpec((2, 1, qk_w), full(3)),
                  pl.BlockSpec((2, 7 * CHUNK, CHUNK), full(3)),
                  pl.BlockSpec((2, 2 * CHUNK, CHUNK), full(3)),
                  pl.BlockSpec((1, GL_DV), full(2))],
        out_specs=pl.BlockSpec((None, s, HEAD_W), lambda bi: (bi, 0, 0)),
        scratch_shapes=[pltpu.VMEM((s, HEAD_W), F32), pltpu.VMEM((2, s, qk_w), F32),
                        pltpu.VMEM((2, GL_PAIRS, GL_DV, LANE), F32),
                        pltpu.VMEM((nch, GL_HEADS, GL_DV, CHUNK), BF16)],
        compiler_params=_cparams(("parallel",)),
        name="gla",
    )(oa, oa, oa, of, of, wa, ba, jnp.asarray(gmat, BF16), jnp.asarray(lvl), norm_g.reshape(1, GL_DV))


def _rope_tables(n):
    t = np.arange(n)
    n_freq = DF_HD // 4
    inv = jnp.asarray(ROPE_THETA, F32) ** (-jnp.arange(n_freq, dtype=F32) / n_freq)
    ang_r = jnp.asarray(t // GRID_W, F32)[:, None] * inv
    ang_c = jnp.asarray(t % GRID_W, F32)[:, None] * inv
    ang = jnp.concatenate([ang_r, ang_r, ang_c, ang_c] * 2, axis=1)
    first = (np.arange(LANE) % 32) < 16
    cos, sin = jnp.cos(ang), jnp.sin(ang)
    return cos, jnp.where(first, -sin, 0.0), jnp.where(first, 0.0, sin)


DF_ROW_GROUPS = 4


def _rope(x, cos, sa, sb):
    return x * cos + pltpu.roll(x, LANE - 16, 1) * sa + pltpu.roll(x, 16, 1) * sb


def _diff_kernel(q_ref, k_ref, v_ref, cq_ref, saq_ref, sbq_ref, ck_ref, sak_ref, sbk_ref, lam_ref, ng_ref,
                 out_ref, kr_ref, *, nct, n_ctx, lam_init):
    s = k_ref.shape[0]
    tq = q_ref.shape[0]
    i = pl.program_id(2)

    @pl.when(i == 0)
    def _():
        kr_ref[0:n_ctx, :] = k_ref[0:n_ctx, :].astype(BF16)
        kr_ref[n_ctx:s, :] = _rope(k_ref[n_ctx:s, :], ck_ref[...], sak_ref[...], sbk_ref[...]).astype(BF16)

    lp = lam_ref[...]
    lam = (jnp.exp(jnp.sum(lp[0:1] * lp[1:2], axis=1, keepdims=True))
           - jnp.exp(jnp.sum(lp[2:3] * lp[3:4], axis=1, keepdims=True)) + lam_init)
    lane = lax.broadcasted_iota(jnp.int32, q_ref.shape, 1)

    def attend(qb, nk):
        qb = qb * (DF_HD ** -0.5)
        qs = jnp.concatenate([jnp.where(lane < DF_HD, qb, 0.0), jnp.where(lane >= DF_HD, qb, 0.0)],
                             axis=0).astype(BF16)
        rg = 2 * tq // DF_ROW_GROUPS
        scs = [_dot_nt(qs[g * rg:(g + 1) * rg], kr_ref[0:nk, :]) for g in range(DF_ROW_GROUPS)]
        ovs = []
        for sc in scs:
            p = jnp.exp(sc - jnp.max(sc, axis=1, keepdims=True))
            rl = 1.0 / jnp.sum(p, axis=1, keepdims=True)
            ovs.append((p.astype(BF16), rl))
        ov = jnp.concatenate([_dot(p, v_ref[0:nk, :]) * rl for p, rl in ovs], axis=0)
        o = ov[0:tq] - lam * ov[tq:2 * tq]
        out_ref[...] = (_rms(o, ng_ref[...]) * (1.0 - lam_init)).astype(BF16)

    @pl.when(i < nct)
    def _():
        attend(q_ref[...], n_ctx)

    @pl.when(i >= nct)
    def _():
        attend(_rope(q_ref[...], cq_ref[...], saq_ref[...], sbq_ref[...]), s)


def _diff_attn(oa, of, rope, df_lambda, norm_g, tq, nct, n_ctx, lam_init):
    b, s, _ = oa.shape
    n = s - n_ctx
    cos, sa, sb = rope
    lam_p = jnp.zeros((4, LANE), F32).at[:, :DF_HD].set(df_lambda)
    qblk = lambda bi, h, i: (jnp.maximum(i - nct, 0), 0)
    full = lambda bi, h, i: (0, 0)
    return pl.pallas_call(
        functools.partial(_diff_kernel, nct=nct, n_ctx=n_ctx, lam_init=lam_init),
        out_shape=jax.ShapeDtypeStruct((b, s, DF_HEADS * DF_DV), BF16),
        grid=(b, DF_HEADS, s // tq),
        in_specs=[pl.BlockSpec((None, tq, LANE), lambda bi, h, i: (bi, i, F_DQ + h)),
                  pl.BlockSpec((None, s, LANE), lambda bi, h, i: (bi, 0, F_DK + h)),
                  pl.BlockSpec((None, s, LANE), lambda bi, h, i: (bi, 0, A_DV + h)),
                  pl.BlockSpec((tq, LANE), qblk), pl.BlockSpec((tq, LANE), qblk), pl.BlockSpec((tq, LANE), qblk),
                  pl.BlockSpec((n, LANE), full), pl.BlockSpec((n, LANE), full), pl.BlockSpec((n, LANE), full),
                  pl.BlockSpec((4, LANE), full), pl.BlockSpec((1, DF_DV), full)],
        out_specs=pl.BlockSpec((None, tq, LANE), lambda bi, h, i: (bi, i, h)),
        scratch_shapes=[pltpu.VMEM((s, LANE), BF16)],
        compiler_params=_cparams(("parallel", "parallel", "arbitrary")),
        name="diff_attn",
    )(of, of, oa, cos, sa, sb, cos, sa, sb, lam_p, norm_g.reshape(1, DF_DV))


def _merge_kernel(ml_ref, df_ref, gl_ref, g0_ref, g1_ref, g2_ref, wb_ref, wo_ref, x_ref, mod_ref, nf_ref,
                  wr_ref, br_ref, tri_ref, xo_ref, h2_ref, rid_ref, rw_ref, cnt_ref):
    y = (jax.nn.sigmoid(g0_ref[...]) * _dot(ml_ref[...], wb_ref[0])
         + jax.nn.sigmoid(g1_ref[...]) * _dot(df_ref[...], wb_ref[1])
         + jax.nn.sigmoid(g2_ref[...]) * _dot(gl_ref[...], wb_ref[2]))
    xn = x_ref[...] + mod_ref[2:3, :] * _dot(y.astype(BF16), wo_ref[...])
    xo_ref[...] = xn
    h2 = _rms(xn, nf_ref[...]) * (1.0 + mod_ref[4:5, :]) + mod_ref[3:4, :]
    h2_hi = h2.astype(BF16)
    h2_ref[...] = h2_hi
    h2_lo = (h2 - h2_hi.astype(F32)).astype(BF16)
    logits = (_dot(h2_hi, wr_ref[0]) + (_dot(h2_lo, wr_ref[0]) + _dot(h2_hi, wr_ref[1]))) + br_ref[...]
    lane = lax.broadcasted_iota(jnp.int32, logits.shape, 1)
    lane_f = lane.astype(F32)
    neg = -jnp.inf

    def first_max(vals):
        mx = jnp.max(vals, axis=1, keepdims=True)
        return mx, jnp.min(jnp.where(vals == mx, lane_f, float(LANE)), axis=1, keepdims=True)

    is_grp = lane < N_GROUPS
    gmax, gidx = first_max(jnp.where(is_grp, logits, neg))
    pg_top = 1.0 / jnp.sum(jnp.where(is_grp, jnp.exp(logits - gmax), 0.0), axis=1, keepdims=True)
    lo = N_GROUPS + gidx * EXPERTS_PER_GROUP
    in_grp = (lane_f >= lo) & (lane_f < lo + EXPERTS_PER_GROUP)
    le = jnp.where(in_grp, logits, neg)
    m1, e1 = first_max(le)
    m2, e2 = first_max(jnp.where(lane_f == e1, neg, le))
    r = jnp.exp(m2 - m1)
    w1 = pg_top / (1.0 + r)
    w2 = pg_top * r / (1.0 + r)

    @pl.when((pl.program_id(0) == 0) & (pl.program_id(1) == 0))
    def _():
        cnt_ref[...] = jnp.zeros(cnt_ref.shape, F32)

    hot1 = jnp.where(lane_f == e1, 1.0, 0.0)
    hot2 = jnp.where(lane_f == e2, 1.0, 0.0)
    tot1 = jnp.sum(hot1, axis=0, keepdims=True)
    tot2 = jnp.sum(hot2, axis=0, keepdims=True)
    cnt = cnt_ref[...]
    before1 = cnt + _dot(tri_ref[...], hot1.astype(BF16))
    before2 = cnt + tot1 + _dot(tri_ref[...], hot2.astype(BF16))
    rank1 = jnp.sum(hot1 * before1, axis=1, keepdims=True)
    rank2 = jnp.sum(hot2 * before2, axis=1, keepdims=True)
    cnt_ref[...] = cnt + tot1 + tot2
    rid = jnp.where(lane == 0, e1 - N_GROUPS, jnp.where(lane == 1, e2 - N_GROUPS,
                    jnp.where(lane == 2, rank1, jnp.where(lane == 3, rank2, 0.0))))
    rid_ref[...] = rid.astype(jnp.int32)
    rw_ref[...] = jnp.where(lane == 0, w1, jnp.where(lane == 1, w2, 0.0))


def _merge(ml, df, gl, of, wb, wo, xs, modtab, nf, wr, br, tm, nct, first_block):
    b, s, d = xs.shape
    nb = s // tm - first_block
    so = nb * tm
    kind = lambda i: jnp.where(i + first_block >= nct, 1, 0)
    row = lambda bi, i: (bi, i + first_block, 0)
    gate = lambda br_: (lambda bi, i: (bi, i + first_block, F_GT // (d // LANE) + br_))
    outrow = lambda bi, i: (bi, i, 0)
    tri = jnp.asarray(np.tril(np.ones((tm, tm), np.float32), -1), BF16)
    return pl.pallas_call(
        _merge_kernel,
        out_shape=(jax.ShapeDtypeStruct((b, so, d), F32), jax.ShapeDtypeStruct((b, so, d), BF16),
                   jax.ShapeDtypeStruct((b, so, LANE), jnp.int32), jax.ShapeDtypeStruct((b, so, LANE), F32),
                   jax.ShapeDtypeStruct((1, LANE), F32)),
        grid=(b, nb),
        in_specs=[pl.BlockSpec((None, tm, BRANCH_W), row), pl.BlockSpec((None, tm, BRANCH_W), row),
                  pl.BlockSpec((None, tm, BRANCH_W), row),
                  pl.BlockSpec((None, tm, d), gate(0)), pl.BlockSpec((None, tm, d), gate(1)),
                  pl.BlockSpec((None, tm, d), gate(2)),
                  _const_spec((N_BRANCH, BRANCH_W, d)), _const_spec((d, d)),
                  pl.BlockSpec((None, tm, d), row),
                  pl.BlockSpec((None, None, SUBLANE, d), lambda bi, i: (bi, kind(i), 0, 0)),
                  _const_spec((1, d)), _const_spec((2, d, LANE)), _const_spec((1, LANE)), _const_spec((tm, tm))],
        out_specs=(pl.BlockSpec((None, tm, d), outrow), pl.BlockSpec((None, tm, d), outrow),
                   pl.BlockSpec((None, tm, LANE), outrow), pl.BlockSpec((None, tm, LANE), outrow),
                   pl.BlockSpec((1, LANE), lambda bi, i: (0, 0))),
        compiler_params=_cparams(("arbitrary", "arbitrary")),
        name="merge_route",
    )(ml, df, gl, of, of, of, wb, wo, xs, modtab, nf.reshape(1, d), wr, br, tri)


def _gmm_kernel(te_ref, tv_ref, x_ref, wg_ref, wu_ref, wd_ref, y_ref, wgb_ref, wub_ref, wdb_ref):
    i = pl.program_id(0)

    @pl.when((i == 0) | (te_ref[i] != te_ref[jnp.maximum(i - 1, 0)]))
    def _():
        wgb_ref[...] = wg_ref[...].astype(BF16)
        wub_ref[...] = wu_ref[...].astype(BF16)
        wdb_ref[...] = wd_ref[...].astype(BF16)

    @pl.when(tv_ref[i] > 0)
    def _():
        x = x_ref[...]
        a = _dot(x, wgb_ref[...])
        hid = (a * jax.nn.sigmoid(a)) * _dot(x, wub_ref[...])
        y_ref[...] = _dot(hid.astype(BF16), wdb_ref[...]).astype(y_ref.dtype)

    @pl.when(tv_ref[i] == 0)
    def _():
        y_ref[...] = jnp.zeros(y_ref.shape, y_ref.dtype)


def _gmm(tile_expert, tile_valid, xs, wg, wu, wd, li):
    npad, d = xs.shape
    tm = MOE_TILE
    de = wg.shape[3]
    return pl.pallas_call(
        _gmm_kernel,
        out_shape=jax.ShapeDtypeStruct((npad, d), BF16),
        grid_spec=pltpu.PrefetchScalarGridSpec(
            num_scalar_prefetch=2,
            grid=(npad // tm,),
            in_specs=[pl.BlockSpec((tm, d), lambda i, te, tv: (i, 0)),
                      pl.BlockSpec((None, None, d, de), lambda i, te, tv: (li, te[i], 0, 0)),
                      pl.BlockSpec((None, None, d, de), lambda i, te, tv: (li, te[i], 0, 0)),
                      pl.BlockSpec((None, None, de, d), lambda i, te, tv: (li, te[i], 0, 0))],
            out_specs=pl.BlockSpec((tm, d), lambda i, te, tv: (i, 0)),
            scratch_shapes=[pltpu.VMEM((d, de), BF16), pltpu.VMEM((d, de), BF16), pltpu.VMEM((de, d), BF16)]),
        compiler_params=_cparams(("arbitrary",)),
        name="moe_gmm",
    )(tile_expert, tile_valid, xs, wg, wu, wd)


def _route_plan(rid, cnt):
    t = rid.shape[0]
    tm = MOE_TILE
    n_tiles = (TOP_K * t + tm - 1) // tm + N_EXPERTS
    counts = cnt[0, N_GROUPS:N_GROUPS + N_EXPERTS].astype(jnp.int32)
    tiles_e = (counts + tm - 1) // tm
    tile_end = jnp.cumsum(tiles_e)
    pad_start = (tile_end - tiles_e) * tm
    eid, rank = rid[:, 0:TOP_K], rid[:, TOP_K:2 * TOP_K]
    pos = rank + jnp.sum(jnp.where(eid[:, :, None] == jnp.arange(N_EXPERTS), pad_start, 0), axis=-1)
    tile = jnp.arange(n_tiles, dtype=jnp.int32)
    tile_expert = jnp.minimum(jnp.sum(tile[:, None] >= tile_end[None, :], axis=1), N_EXPERTS - 1).astype(jnp.int32)
    tile_valid = (tile < tile_end[-1]).astype(jnp.int32)
    return tile_expert, tile_valid, pos.astype(jnp.int32), n_tiles * tm


def _combine_kernel(x_ref, y0_ref, y1_ref, rw_ref, mod_ref, fg_ref, o_ref, *, final):
    rw = rw_ref[...]
    y = rw[:, 0:1] * y0_ref[...].astype(F32) + rw[:, 1:2] * y1_ref[...].astype(F32)
    xn = x_ref[...] + mod_ref[5:6, :] * y
    o_ref[...] = _rms(xn, fg_ref[...]) if final else xn


def _combine(xs, y0, y1, rw, modtab, fg, tm, nct, first_block, final):
    b, s, d = xs.shape
    kind = lambda i: jnp.where(i + first_block >= nct, 1, 0)
    row = lambda bi, i: (bi, i, 0)
    return pl.pallas_call(
        functools.partial(_combine_kernel, final=final),
        out_shape=jax.ShapeDtypeStruct((b, s, d), F32),
        grid=(b, s // tm),
        in_specs=[pl.BlockSpec((None, tm, d), row), pl.BlockSpec((None, tm, d), row),
                  pl.BlockSpec((None, tm, d), row), pl.BlockSpec((None, tm, LANE), row),
                  pl.BlockSpec((None, None, SUBLANE, d), lambda bi, i: (bi, kind(i), 0, 0)),
                  _const_spec((1, d))],
        out_specs=pl.BlockSpec((None, tm, d), row),
        compiler_params=_cparams(("parallel", "parallel")),
        name="moe_combine",
    )(xs, y0, y1, rw, modtab, fg.reshape(1, d))


def _pack_w_in(w):
    d = w.shape[0]
    mq, mk, mv, mo, mg, dq, dk, dv, gq, gk, gv, gg, ga, gt = jnp.split(
        w, [int(i) for i in np.cumsum(IN_SIZES)[:-1]], axis=1)

    def pad_cols(t):
        return jnp.pad(t, ((0, 0), (0, LANE - t.shape[1])))

    wa = jnp.concatenate([mq, mk, mv, gq, gk, gv, dv], axis=1).astype(BF16)
    wf = jnp.concatenate([dq, dk, mo, gg, gt, pad_cols(mg), pad_cols(ga)], axis=1).astype(BF16)
    return wa, wf


def _dispatch_rows(h2, pos, npad):
    t = h2.shape[0]
    tok = jnp.arange(t, dtype=jnp.int32)
    src = jnp.zeros((npad,), jnp.int32).at[pos[:, 0]].set(tok).at[pos[:, 1]].set(tok)
    return jnp.take(h2, src, axis=0, mode="clip")


def kernel(x, c, ctx, c_ctx, w_mod, b_mod, norm_mix_g, norm_ffn_g, w_in, ml_gate_b, ml_norm_g, df_lambda,
           df_norm_g, gl_w_alpha, gl_b_alpha, gl_norm_g, w_branch, w_out, router_group_w, router_group_b,
           router_expert_w, router_expert_b, moe_w_gate, moe_w_up, moe_w_down, final_norm_g):
    b, n, d = x.shape
    n_ctx = ctx.shape[1]
    s = n_ctx + n
    depth = w_mod.shape[0]
    tm = 256 if n_ctx % 256 == 0 else 128
    assert n_ctx % tm == 0 and n % tm == 0 and n % GRID_W == 0 and n_ctx % CHUNK == 0 and d == D_MODEL
    nct = n_ctx // tm
    nch, nch_ctx = s // CHUNK, n_ctx // CHUNK
    rope = _rope_tables(n)
    xs = jnp.concatenate([ctx, x], axis=1)
    mod_rows = -(-(b + 1) // SUBLANE) * SUBLANE
    cc = jnp.zeros((mod_rows, d), F32).at[:b].set(c).at[b].set(c_ctx)
    for li in range(depth):
        last = li == depth - 1
        lam_init = 0.8 - 0.6 * math.exp(-0.3 * li)
        mod = _modulation(cc, w_mod, b_mod[li], li).reshape(mod_rows, 6, d)
        mod = jnp.pad(mod, ((0, 0), (0, SUBLANE - 6), (0, 0)))
        modtab = jnp.stack([jnp.broadcast_to(mod[b], (b, SUBLANE, d)), mod[:b]], axis=1)
        wa, wf = _pack_w_in(w_in[li])
        oa, of = _inproj(xs, modtab, norm_mix_g[li], wa, wf, tm, nct)
        grow = of[:, :, F_MG * LANE:F_MG * LANE + 4 * ML_HEADS]
        grow = grow.reshape(b, nch, CHUNK, 4 * ML_HEADS).transpose(0, 1, 3, 2)
        ml = _mlstm(oa, of, grow, ml_gate_b[li], ml_norm_g[li], nch_ctx, nch)
        df = _diff_attn(oa, of, rope, df_lambda[li], df_norm_g[li], tm, nct, n_ctx, lam_init)
        gl = _gla(oa, of, gl_w_alpha[li], gl_b_alpha[li], gl_norm_g[li], nch_ctx, nch)
        wr = jnp.zeros((d, LANE), F32).at[:, :N_GROUPS].set(router_group_w[li])
        wr = wr.at[:, N_GROUPS:N_GROUPS + N_EXPERTS].set(router_expert_w[li])
        wr_hi = wr.astype(BF16)
        wr = jnp.stack([wr_hi, (wr - wr_hi.astype(F32)).astype(BF16)])
        br = jnp.zeros((1, LANE), F32).at[0, :N_GROUPS].set(router_group_b[li])
        br = br.at[0, N_GROUPS:N_GROUPS + N_EXPERTS].set(router_expert_b[li])
        first_block = nct if last else 0
        xn, h2, rid, rw, cnt = _merge(ml, df, gl, of, w_branch[li].astype(BF16), w_out[li].astype(BF16), xs,
                                      modtab, norm_ffn_g[li], wr, br, tm, nct, first_block)
        so = xn.shape[1]
        t = b * so
        te, tv, pos, npad = _route_plan(rid.reshape(t, LANE), cnt)
        xsorted = _dispatch_rows(h2.reshape(t, d), pos, npad)
        ys = _gmm(te, tv, xsorted, moe_w_gate, moe_w_up, moe_w_down, li)
        y0 = jnp.take(ys, pos[:, 0], axis=0, mode="clip").reshape(b, so, d)
        y1 = jnp.take(ys, pos[:, 1], axis=0, mode="clip").reshape(b, so, d)
        xs = _combine(xn, y0, y1, rw, modtab, final_norm_g, tm, nct, first_block, last)
    return xs
```
